```python
import math
import jax, jax.numpy as jnp
from jax import lax
import numpy as np

D_MODEL = 2048
BATCH = 4
SEQ = 2048
DEPTH = 4

D_MIX = D_MODEL
A_WIDTH = D_MIX // 2
A_HEADS = 8
A_HEAD_DIM = A_WIDTH // A_HEADS
IDX_HEADS = 16
IDX_DIM = 64
TOPK_MAX = 256
Q_BLOCK = 64
B_WIDTH = D_MIX - A_WIDTH
B_HEADS = 4
B_KEY_WIDTH = B_WIDTH // 2
B_DK = B_KEY_WIDTH // B_HEADS
B_DV = B_WIDTH // B_HEADS
GATE_RANK = 16
GATE_TEMP = 16.0
CHUNK = 64
NUM_BUCKETS = 32
MAX_DISTANCE = 128
EPS = 1e-6

IN_WIDTHS = (A_WIDTH, A_WIDTH, A_WIDTH, A_WIDTH,
             IDX_HEADS * IDX_DIM, IDX_DIM, IDX_HEADS,
             B_KEY_WIDTH, B_KEY_WIDTH, B_WIDTH, B_WIDTH,
             GATE_RANK)
IN_COLS = sum(IN_WIDTHS)

kernel_name = "hymba_dsa_gla_adaln_trunk"


def rmsnorm(x, g):
    x32 = x.astype(jnp.float32)
    r = x32 * lax.rsqrt(jnp.mean(x32 * x32, axis=-1, keepdims=True) + EPS)
    return (r * g.astype(jnp.float32)).astype(x.dtype)


def t5_bucket(dist):
    max_exact = NUM_BUCKETS // 2
    d = jnp.maximum(dist, 0)
    df = jnp.maximum(d, 1).astype(jnp.float32)
    large = max_exact + (jnp.log(df / max_exact) / math.log(MAX_DISTANCE / max_exact)
                         * (NUM_BUCKETS - max_exact)).astype(jnp.int32)
    large = jnp.minimum(large, NUM_BUCKETS - 1)
    return jnp.where(d < max_exact, d, large)


def dsa_attention(q, k, v, iq, ik, iw, rel_bias):
    bsz, T, H, dh = q.shape
    topk = min(TOPK_MAX, T // 4)
    nb = T // Q_BLOCK

    def blockify(a):
        return jnp.moveaxis(a.reshape((bsz, nb, Q_BLOCK) + a.shape[2:]), 1, 0)

    key_pos = jnp.arange(T, dtype=jnp.int32)
    ik32 = ik.astype(jnp.float32)
    iw32 = iw.astype(jnp.float32) * (IDX_HEADS ** -0.5 * IDX_DIM ** -0.5)
    gather = jax.vmap(lambda kb, ib: kb[ib])

    def one_block(args):
        qb, iqb, iwb, start = args
        qpos = start + jnp.arange(Q_BLOCK, dtype=jnp.int32)
        s = jnp.einsum('bqjd,bsd->bqsj', iqb.astype(jnp.float32), ik32)
        score = jnp.einsum('bqsj,bqj->bqs', jax.nn.relu(s), iwb)
        causal = key_pos[None, :] <= qpos[:, None]
        score = jnp.where(causal[None], score, -jnp.inf)
        _, idx = lax.top_k(score, topk)
        valid = idx <= qpos[None, :, None]
        ks = gather(k, idx)
        vs = gather(v, idx)
        logits = jnp.einsum('bqhd,bqkhd->bhqk', qb, ks,
                            preferred_element_type=jnp.float32) * (dh ** -0.5)
        bucket = t5_bucket(qpos[None, :, None] - idx)
        logits = logits + jnp.moveaxis(rel_bias.astype(jnp.float32)[bucket], -1, 1)
        logits = jnp.where(valid[:, None], logits, -jnp.inf)
        p = jax.nn.softmax(logits, axis=-1)
        return jnp.einsum('bhqk,bqkhd->bqhd', p.astype(v.dtype), vs)

    starts = jnp.arange(nb, dtype=jnp.int32) * Q_BLOCK
    out = lax.map(one_block, (blockify(q), blockify(iq), blockify(iw32), starts))
    return jnp.moveaxis(out, 0, 1).reshape(bsz, T, H, dh)


def gla_chunked(q, k, v, log_a):
    bsz, T, H, dk = q.shape
    dv = v.shape[-1]
    nc = T // CHUNK
    f32 = jnp.float32

    def chunks(a):
        return jnp.moveaxis(a.astype(f32).reshape((bsz, nc, CHUNK) + a.shape[2:]), 1, 0)

    tri = jnp.tril(jnp.ones((CHUNK, CHUNK), dtype=bool))[None, :, :, None, None]

    def step(S, inp):
        qc, kc, vc, gc = inp
        b = jnp.cumsum(gc, axis=1)
        b_last = b[:, -1]
        o_inter = jnp.einsum('bthk,bhkv->bthv', qc * jnp.exp(b), S)
        diff = b[:, :, None] - b[:, None, :]
        decay = jnp.exp(jnp.where(tri, diff, -jnp.inf))
        attn = jnp.einsum('bthk,bshk,btshk->bhts', qc, kc, decay)
        o_intra = jnp.einsum('bhts,bshv->bthv', attn, vc)
        k_dec = kc * jnp.exp(b_last[:, None] - b)
        S_new = jnp.exp(b_last)[..., None] * S + jnp.einsum('bshk,bshv->bhkv', k_dec, vc)
        return S_new, o_inter + o_intra

    S0 = jnp.zeros((bsz, H, dk, dv), f32)
    _, o = lax.scan(step, S0, (chunks(q), chunks(k), chunks(v), chunks(log_a)))
    return jnp.moveaxis(o, 0, 1).reshape(bsz, T, H, dv).astype(v.dtype)


def hybrid_layer(x, shift, scale, gate, norm_g, w_in, w_alpha_up, b_alpha, gla_g,
                 w_out, rel_bias):
    bsz, T, _ = x.shape
    h = rmsnorm(x, norm_g) * (1.0 + scale[:, None]) + shift[:, None]
    proj = jnp.einsum('btd,dc->btc', h, w_in)
    points = np.cumsum(IN_WIDTHS)[:-1].tolist()
    aq, ak, av, ag, iq, ik, iw, bq, bk, bv, bg, ba = jnp.split(proj, points, axis=-1)

    heads_a = lambda a: a.reshape(bsz, T, A_HEADS, A_HEAD_DIM)
    a_out = dsa_attention(heads_a(aq), heads_a(ak), heads_a(av),
                          iq.reshape(bsz, T, IDX_HEADS, IDX_DIM), ik, iw, rel_bias)
    a_out = a_out.reshape(bsz, T, A_WIDTH) * jax.nn.silu(ag)

    pre = jnp.einsum('btr,rk->btk', ba, w_alpha_up) + b_alpha
    log_a = jax.nn.log_sigmoid(pre.astype(jnp.float32)) / GATE_TEMP
    heads_k = lambda a: a.reshape(bsz, T, B_HEADS, B_DK)
    b_out = gla_chunked(heads_k(bq) * (B_DK ** -0.5), heads_k(bk),
                        bv.reshape(bsz, T, B_HEADS, B_DV), heads_k(log_a))
    b_out = rmsnorm(b_out, gla_g).reshape(bsz, T, B_WIDTH) * jax.nn.silu(bg)

    y = jnp.einsum('btc,cd->btd', jnp.concatenate([a_out, b_out], axis=-1), w_out)
    return x + gate[:, None] * y


def setup_inputs(seed: int = 0) -> dict:
    key = jax.random.key(seed)
    ks = jax.random.split(key, 12)
    f32 = jnp.float32
    nrm = lambda k, shape, s: jax.random.normal(k, shape, f32) * s
    return {
        "x": nrm(ks[0], (BATCH, SEQ, D_MODEL), 1.0),
        "c": nrm(ks[1], (BATCH, D_MODEL), 1.0),
        "w_ada": nrm(ks[2], (DEPTH, D_MODEL, 3 * D_MODEL), 0.5 * D_MODEL ** -0.5),
        "b_ada": nrm(ks[3], (DEPTH, 3 * D_MODEL), 0.02),
        "norm_g": 1.0 + nrm(ks[4], (DEPTH, D_MODEL), 0.02),
        "w_in": nrm(ks[5], (DEPTH, D_MODEL, IN_COLS), D_MODEL ** -0.5),
        "w_alpha_up": nrm(ks[6], (DEPTH, GATE_RANK, B_KEY_WIDTH), GATE_RANK ** -0.5),
        "b_alpha": nrm(ks[7], (DEPTH, B_KEY_WIDTH), 0.02),
        "gla_g": 1.0 + nrm(ks[8], (DEPTH, B_DV), 0.02),
        "w_out": nrm(ks[9], (DEPTH, D_MIX, D_MODEL), D_MIX ** -0.5),
        "rel_bias": nrm(ks[10], (NUM_BUCKETS, A_HEADS), 0.5),
        "final_g": 1.0 + nrm(ks[11], (D_MODEL,), 0.02),
    }


def reference(x, c, w_ada, b_ada, norm_g, w_in, w_alpha_up, b_alpha, gla_g, w_out,
              rel_bias, final_g):
    c_act = jax.nn.silu(c)
    for l in range(DEPTH):
        mod = jnp.einsum('bd,de->be', c_act, w_ada[l]) + b_ada[l]
        shift, scale, gate = jnp.split(mod, 3, axis=-1)
        x = hybrid_layer(x, shift, scale, gate, norm_g[l], w_in[l], w_alpha_up[l],
                         b_alpha[l], gla_g[l], w_out[l], rel_bias)
    return rmsnorm(x, final_g)
```

```python
import functools
import math

import numpy as np
import jax
import jax.numpy as jnp
from jax import lax
from jax.experimental import pallas as pl
from jax.experimental.pallas import tpu as pltpu

D_MODEL = 2048
BATCH = 4
SEQ = 2048
DEPTH = 4
A_WIDTH = 1024
A_HEADS = 8
A_HEAD_DIM = 128
IDX_HEADS = 16
IDX_DIM = 64
TOPK = min(256, SEQ // 4)
B_WIDTH = 1024
B_HEADS = 4
B_KEY_WIDTH = 512
B_DK = 128
B_DV = 256
GATE_RANK = 16
GATE_TEMP = 16.0
NUM_BUCKETS = 32
MAX_DISTANCE = 128
EPS = 1e-6

IN_WIDTHS = (A_WIDTH, A_WIDTH, A_WIDTH, A_WIDTH, IDX_HEADS * IDX_DIM, IDX_DIM, IDX_HEADS,
             B_KEY_WIDTH, B_KEY_WIDTH, B_WIDTH, B_WIDTH, GATE_RANK)
IN_NAMES = ("aq", "ak", "av", "ag", "iq", "ik", "iw", "bq", "bk", "bv", "bg", "ba")
IN_OFFSETS = dict(zip(IN_NAMES, np.concatenate([[0], np.cumsum(IN_WIDTHS)[:-1]]).tolist()))
IN_WIDTH_OF = dict(zip(IN_NAMES, IN_WIDTHS))

LANE = 128
VMEM_LIMIT = 52 * 1024 * 1024

PROJ_TN = 512
BF_SEGS = (("aq", 1024), ("ak", 1024), ("av", 1024))
F32_SEGS = (("ag", 1024), ("iq", 1024), ("bv", 1024), ("bg", 1024), ("bq", 512), ("bk", 512),
            ("ik", LANE), ("iw", LANE), ("ba", LANE))
NB_COLS = sum(w for _, w in BF_SEGS)
NF_USED = sum(w for _, w in F32_SEGS)
NF_COLS = -(-NF_USED // PROJ_TN) * PROJ_TN
NP_COLS = NB_COLS + NF_COLS


def _seg_offsets(segs):
    offs, o = {}, 0
    for name, w in segs:
        offs[name] = o
        o += w
    return offs


BF_OFF = _seg_offsets(BF_SEGS)
F32_OFF = _seg_offsets(F32_SEGS)

INT_MIN = -2 ** 31
NEG_BIG = -1e30


def _pack_w_in(w_in):
    parts = []
    for name, w in BF_SEGS + F32_SEGS:
        seg = w_in[:, :, IN_OFFSETS[name]:IN_OFFSETS[name] + IN_WIDTH_OF[name]]
        pad = w - IN_WIDTH_OF[name]
        if pad:
            seg = jnp.pad(seg, ((0, 0), (0, 0), (0, pad)))
        parts.append(seg)
    tail = NP_COLS - NB_COLS - NF_USED
    if tail:
        parts.append(jnp.zeros(w_in.shape[:2] + (tail,), w_in.dtype))
    return jnp.concatenate(parts, axis=-1).astype(jnp.bfloat16)


MOD_TN = 768


def _mod_kernel(c_ref, w_ref, b_ref, o_ref):
    c = c_ref[...]
    c_act = c * jax.nn.sigmoid(c)
    acc = jnp.dot(c_act.astype(jnp.bfloat16), w_ref[0].astype(jnp.bfloat16),
                  preferred_element_type=jnp.float32)
    o_ref[0] = acc + b_ref[0]


def _adaln_mod(c, w_ada, b_ada):
    cp = jnp.pad(c, ((0, 8 - BATCH), (0, 0)))
    out = pl.pallas_call(
        _mod_kernel,
        grid=(DEPTH, 3 * D_MODEL // MOD_TN),
        in_specs=[
            pl.BlockSpec((8, D_MODEL), lambda l, j: (0, 0)),
            pl.BlockSpec((1, D_MODEL, MOD_TN), lambda l, j: (l, 0, j)),
            pl.BlockSpec((1, 1, MOD_TN), lambda l, j: (l, 0, j)),
        ],
        out_specs=pl.BlockSpec((1, 8, MOD_TN), lambda l, j: (l, 0, j)),
        out_shape=jax.ShapeDtypeStruct((DEPTH, 8, 3 * D_MODEL), jnp.float32),
        compiler_params=pltpu.CompilerParams(
            dimension_semantics=("arbitrary", "arbitrary"), vmem_limit_bytes=VMEM_LIMIT),
        name="adaln_mod",
    )(cp, w_ada, b_ada.reshape(DEPTH, 1, 3 * D_MODEL))
    return out[:, :BATCH]


PROJ_TM = 1024
NB_TILES = NB_COLS // PROJ_TN


def _proj_kernel(x_ref, shift_ref, scale_ref, g_ref, w_ref, ob_ref, of_ref, h_scr):
    j = pl.program_id(1)

    @pl.when(j == 0)
    def _():
        x = x_ref[...]
        r = x * lax.rsqrt(jnp.mean(x * x, axis=-1, keepdims=True) + EPS)
        h = (r * g_ref[...]) * (1.0 + scale_ref[0]) + shift_ref[0]
        h_scr[...] = h.astype(jnp.bfloat16)

    acc = jnp.dot(h_scr[...], w_ref[...], preferred_element_type=jnp.float32)

    @pl.when(j < NB_TILES)
    def _():
        ob_ref[...] = acc.astype(jnp.bfloat16)

    @pl.when(j >= NB_TILES)
    def _():
        of_ref[...] = acc


def _norm_proj(x2d, shift, scale, norm_g, w_p, layer):
    m = x2d.shape[0]
    tiles_per_batch = SEQ // PROJ_TM
    return pl.pallas_call(
        _proj_kernel,
        grid=(m // PROJ_TM, NP_COLS // PROJ_TN),
        in_specs=[
            pl.BlockSpec((PROJ_TM, D_MODEL), lambda i, j: (i, 0)),
            pl.BlockSpec((1, 1, D_MODEL), lambda i, j: (i // tiles_per_batch, 0, 0)),
            pl.BlockSpec((1, 1, D_MODEL), lambda i, j: (i // tiles_per_batch, 0, 0)),
            pl.BlockSpec((1, D_MODEL), lambda i, j: (0, 0)),
            pl.BlockSpec((None, D_MODEL, PROJ_TN), lambda i, j: (layer, 0, j)),
        ],
        out_specs=[
            pl.BlockSpec((PROJ_TM, PROJ_TN), lambda i, j: (i, jnp.minimum(j, NB_TILES - 1))),
            pl.BlockSpec((PROJ_TM, PROJ_TN), lambda i, j: (i, jnp.maximum(j - NB_TILES, 0))),
        ],
        out_shape=[
            jax.ShapeDtypeStruct((m, NB_COLS), jnp.bfloat16),
            jax.ShapeDtypeStruct((m, NF_COLS), jnp.float32),
        ],
        scratch_shapes=[pltpu.VMEM((PROJ_TM, D_MODEL), jnp.bfloat16)],
        compiler_params=pltpu.CompilerParams(
            dimension_semantics=("arbitrary", "arbitrary"), vmem_limit_bytes=VMEM_LIMIT),
        name="norm_proj",
    )(x2d, shift, scale, norm_g.reshape(1, D_MODEL), w_p)


TQ = 128
KB = 128
NQ = SEQ // TQ


def _t5_bucket_np(d):
    max_exact = NUM_BUCKETS // 2
    d = np.maximum(d, 0)
    df = np.maximum(d, 1).astype(np.float32)
    large = max_exact + (np.log(df / np.float32(max_exact)) / np.float32(math.log(MAX_DISTANCE / max_exact))
                         * np.float32(NUM_BUCKETS - max_exact)).astype(np.int32)
    large = np.minimum(large, NUM_BUCKETS - 1)
    return np.where(d < max_exact, d, large).astype(np.int32)


assert int(_t5_bucket_np(np.arange(KB, 2 * SEQ)).min()) == NUM_BUCKETS - 1


def _band_buckets():
    tl = np.arange(TQ)[:, None]
    u = np.arange(KB)[None, :]
    prev = _t5_bucket_np(KB + tl - u)
    diag = _t5_bucket_np(tl - u)
    return np.stack([prev, diag]).astype(np.int32)


def _band_kernel(rb_ref, bucket_ref, o_ref):
    bucket = bucket_ref[...]
    for h in range(A_HEADS):
        far = rb_ref[NUM_BUCKETS - 1, h]
        acc = jnp.zeros(bucket.shape, jnp.float32)
        for b in range(NUM_BUCKETS):
            acc = jnp.where(bucket == b, rb_ref[b, h] - far, acc)
        o_ref[h] = acc


def _bias_band(rel_bias):
    return pl.pallas_call(
        _band_kernel,
        in_specs=[
            pl.BlockSpec(memory_space=pltpu.SMEM),
            pl.BlockSpec((2, TQ, KB), lambda: (0, 0, 0)),
        ],
        out_specs=pl.BlockSpec((A_HEADS, 2, TQ, KB), lambda: (0, 0, 0, 0)),
        out_shape=jax.ShapeDtypeStruct((A_HEADS, 2, TQ, KB), jnp.float32),
        name="bias_band",
    )(rel_bias, jnp.asarray(_band_buckets()))


def _dot_nt(a, b):
    return lax.dot_general(a, b, (((1,), (1,)), ((), ())), preferred_element_type=jnp.float32)


def _dot_tn(a, b):
    return lax.dot_general(a, b, (((0,), (0,)), ((), ())), preferred_element_type=jnp.float32)


def _dsa_kernel(q_ref, k_ref, v_ref, iq_ref, ik_ref, iw_ref, ag_ref, band_ref, o_ref,
                key_scr, wb_scr):
    qi = pl.program_id(1)
    nkb = qi + 1

    iw = iw_ref[0][:, :IDX_HEADS] * (IDX_HEADS ** -0.5 * IDX_DIM ** -0.5)
    for j in range(IDX_HEADS):
        wb_scr[j] = jnp.broadcast_to(iw[:, j:j + 1], (TQ, KB))

    iq = iq_ref[0].astype(jnp.bfloat16)
    row = lax.broadcasted_iota(jnp.int32, (TQ, KB), 0)
    col = lax.broadcasted_iota(jnp.int32, (TQ, KB), 1)

    def score_block(kb, carry):
        r0 = pl.multiple_of(kb * KB, KB)
        ik = ik_ref[0, pl.ds(r0, KB), :][:, :IDX_DIM].astype(jnp.bfloat16)
        acc = jnp.zeros((TQ, KB), jnp.float32)
        for j in range(IDX_HEADS):
            s = _dot_nt(iq[:, j * IDX_DIM:(j + 1) * IDX_DIM], ik)
            acc = acc + jnp.maximum(s, 0.0) * wb_scr[j]
        bits = pltpu.bitcast(acc, jnp.int32)
        key = jnp.where(bits >= 0, bits, bits ^ 0x7FFFFFFF)
        causal = (kb * KB + col) <= (qi * TQ + row)
        key_scr[kb] = jnp.where(causal, key, INT_MIN)
        return carry

    lax.fori_loop(0, nkb, score_block, 0)

    def count_ge(cand):
        def body(kb, cnt):
            return cnt + (key_scr[kb] >= cand).astype(jnp.int32)
        cnt = lax.fori_loop(0, nkb, body, jnp.zeros((TQ, KB), jnp.int32))
        return jnp.sum(cnt, axis=-1, keepdims=True)

    zero = jnp.zeros((TQ, 1), jnp.int32)
    prefix = jnp.where(count_ge(zero) >= TOPK, zero, INT_MIN)

    def bit_pass(it, prefix):
        cand = prefix + lax.shift_left(jnp.int32(1), 30 - it)
        return jnp.where(count_ge(cand) >= TOPK, cand, prefix)

    prefix = lax.fori_loop(0, 31, bit_pass, prefix)
    thr = jnp.maximum(prefix, INT_MIN + 1)

    scale = A_HEAD_DIM ** -0.5
    for h in range(A_HEADS):
        hs = slice(h * A_HEAD_DIM, (h + 1) * A_HEAD_DIM)
        qh = q_ref[0, :, hs]

        def kv_block(kb, carry, hs=hs, qh=qh, h=h):
            m, l, acc = carry
            r0 = pl.multiple_of(kb * KB, KB)
            kblk = k_ref[0, pl.ds(r0, KB), hs]
            vblk = v_ref[0, pl.ds(r0, KB), hs]
            lg = _dot_nt(qh, kblk) * scale
            slot = kb - qi + 1
            band = band_ref[h, jnp.maximum(slot, 0)]
            lg = lg + jnp.where(slot >= 0, band, 0.0)
            lg = jnp.where(key_scr[kb] >= thr, lg, NEG_BIG)
            m_new = jnp.maximum(m, jnp.max(lg, axis=-1, keepdims=True))
            alpha = jnp.exp(m - m_new)
            p = jnp.exp(lg - m_new)
            l = alpha * l + jnp.sum(p, axis=-1, keepdims=True)
            acc = alpha * acc + jnp.dot(p.astype(jnp.bfloat16), vblk,
                                        preferred_element_type=jnp.float32)
            return m_new, l, acc

        m0 = jnp.full((TQ, 1), NEG_BIG, jnp.float32)
        l0 = jnp.zeros((TQ, 1), jnp.float32)
        a0 = jnp.zeros((TQ, A_HEAD_DIM), jnp.float32)
        _, l, acc = lax.fori_loop(0, nkb, kv_block, (m0, l0, a0))
        g = ag_ref[0][:, hs]
        o_ref[0, :, hs] = ((acc / l) * (g * jax.nn.sigmoid(g))).astype(o_ref.dtype)


def _dsa_attention(proj_b, proj_f, band):
    cb = lambda name: BF_OFF[name] // A_WIDTH
    return pl.pallas_call(
        _dsa_kernel,
        grid=(BATCH, NQ),
        in_specs=[
            pl.BlockSpec((1, TQ, A_WIDTH), lambda b, i: (b, i, cb("aq"))),
            pl.BlockSpec((1, SEQ, A_WIDTH), lambda b, i: (b, 0, cb("ak"))),
            pl.BlockSpec((1, SEQ, A_WIDTH), lambda b, i: (b, 0, cb("av"))),
            pl.BlockSpec((1, TQ, A_WIDTH), lambda b, i: (b, i, F32_OFF["iq"] // A_WIDTH)),
            pl.BlockSpec((1, SEQ, LANE), lambda b, i: (b, 0, F32_OFF["ik"] // LANE)),
            pl.BlockSpec((1, TQ, LANE), lambda b, i: (b, i, F32_OFF["iw"] // LANE)),
            pl.BlockSpec((1, TQ, A_WIDTH), lambda b, i: (b, i, F32_OFF["ag"] // A_WIDTH)),
            pl.BlockSpec((A_HEADS, 2, TQ, KB), lambda b, i: (0, 0, 0, 0)),
        ],
        out_specs=pl.BlockSpec((1, TQ, A_WIDTH), lambda b, i: (b, i, 0)),
        out_shape=jax.ShapeDtypeStruct((BATCH, SEQ, A_WIDTH), jnp.bfloat16),
        scratch_shapes=[
            pltpu.VMEM((SEQ // KB, TQ, KB), jnp.int32),
            pltpu.VMEM((IDX_HEADS, TQ, KB), jnp.float32),
        ],
        compiler_params=pltpu.CompilerParams(
            dimension_semantics=("arbitrary", "arbitrary"), vmem_limit_bytes=VMEM_LIMIT),
        name="dsa_attention",
    )(proj_b, proj_b, proj_b, proj_f, proj_f, proj_f, proj_f, band)


GLA_CT = 512
GLA_C = 64


def _gla_kernel(bq_ref, bk_ref, bv_ref, bg_ref, ba_ref, wup_ref, balpha_ref, g_ref, o_ref, st_scr):
    @pl.when(pl.program_id(1) == 0)
    def _():
        st_scr[...] = jnp.zeros_like(st_scr)

    rr = lax.broadcasted_iota(jnp.int32, (GLA_C, GLA_C), 0)
    cc = lax.broadcasted_iota(jnp.int32, (GLA_C, GLA_C), 1)
    tri = rr >= cc
    tri_bf = tri.astype(jnp.bfloat16)
    wup = wup_ref[...].astype(jnp.bfloat16)
    balpha = balpha_ref[...]
    gain = g_ref[...]

    def chunk(c, carry):
        r0 = pl.multiple_of(c * GLA_C, GLA_C)
        rows = pl.ds(r0, GLA_C)
        ba = ba_ref[0, rows, :][:, :GATE_RANK].astype(jnp.bfloat16)
        pre = jnp.dot(ba, wup, preferred_element_type=jnp.float32) + balpha
        log_a = (jnp.minimum(pre, 0.0) - jnp.log1p(jnp.exp(-jnp.abs(pre)))) * (1.0 / GATE_TEMP)
        la_hi = log_a.astype(jnp.bfloat16)
        la_lo = (log_a - la_hi.astype(jnp.float32)).astype(jnp.bfloat16)
        bcum = (jnp.dot(tri_bf, la_hi, preferred_element_type=jnp.float32)
                + jnp.dot(tri_bf, la_lo, preferred_element_type=jnp.float32))
        for h in range(B_HEADS):
            ks = slice(h * B_DK, (h + 1) * B_DK)
            vs = slice(h * B_DV, (h + 1) * B_DV)
            b = bcum[:, ks]
            b_last = b[GLA_C - 1:GLA_C, :]
            q = bq_ref[0, rows, ks] * (B_DK ** -0.5)
            k = bk_ref[0, rows, ks]
            v = bv_ref[0, rows, vs].astype(jnp.bfloat16)
            qe = (q * jnp.exp(b)).astype(jnp.bfloat16)
            ke = (k * jnp.exp(-b)).astype(jnp.bfloat16)
            kd = (k * jnp.exp(b_last - b)).astype(jnp.bfloat16)
            attn = jnp.where(tri, _dot_nt(qe, ke), 0.0)
            st = st_scr[h]
            o = _dot_nt(qe, st.astype(jnp.bfloat16)) + jnp.dot(
                attn.astype(jnp.bfloat16), v, preferred_element_type=jnp.float32)
            st_scr[h] = st * jnp.exp(b_last) + _dot_tn(v, kd)
            on = o * lax.rsqrt(jnp.mean(o * o, axis=-1, keepdims=True) + EPS) * gain
            g = bg_ref[0, rows, vs]
            o_ref[0, rows, vs] = (on * (g * jax.nn.sigmoid(g))).astype(o_ref.dtype)
        return carry

    lax.fori_loop(0, GLA_CT // GLA_C, chunk, 0)


def _gla(proj_f, w_alpha_up, b_alpha, gla_g):
    return pl.pallas_call(
        _gla_kernel,
        grid=(BATCH, SEQ // GLA_CT),
        in_specs=[
            pl.BlockSpec((1, GLA_CT, B_KEY_WIDTH), lambda b, t: (b, t, F32_OFF["bq"] // B_KEY_WIDTH)),
            pl.BlockSpec((1, GLA_CT, B_KEY_WIDTH), lambda b, t: (b, t, F32_OFF["bk"] // B_KEY_WIDTH)),
            pl.BlockSpec((1, GLA_CT, B_WIDTH), lambda b, t: (b, t, F32_OFF["bv"] // B_WIDTH)),
            pl.BlockSpec((1, GLA_CT, B_WIDTH), lambda b, t: (b, t, F32_OFF["bg"] // B_WIDTH)),
            pl.BlockSpec((1, GLA_CT, LANE), lambda b, t: (b, t, F32_OFF["ba"] // LANE)),
            pl.BlockSpec((GATE_RANK, B_KEY_WIDTH), lambda b, t: (0, 0)),
            pl.BlockSpec((1, B_KEY_WIDTH), lambda b, t: (0, 0)),
            pl.BlockSpec((1, B_DV), lambda b, t: (0, 0)),
        ],
        out_specs=pl.BlockSpec((1, GLA_CT, B_WIDTH), lambda b, t: (b, t, 0)),
        out_shape=jax.ShapeDtypeStruct((BATCH, SEQ, B_WIDTH), jnp.bfloat16),
        scratch_shapes=[pltpu.VMEM((B_HEADS, B_DV, B_DK), jnp.float32)],
        compiler_params=pltpu.CompilerParams(
            dimension_semantics=("arbitrary", "arbitrary"), vmem_limit_bytes=VMEM_LIMIT),
        name="gla",
    )(proj_f, proj_f, proj_f, proj_f, proj_f, w_alpha_up, b_alpha.reshape(1, B_KEY_WIDTH),
      gla_g.reshape(1, B_DV))


OUT_TM = 512


def _out_kernel(a_ref, b_ref, wa_ref, wb_ref, x_ref, gate_ref, fg_ref, o_ref, *, final_norm):
    y = (jnp.dot(a_ref[...], wa_ref[...], preferred_element_type=jnp.float32)
         + jnp.dot(b_ref[...], wb_ref[...], preferred_element_type=jnp.float32))
    xn = x_ref[...] + gate_ref[0] * y
    if final_norm:
        r = xn * lax.rsqrt(jnp.mean(xn * xn, axis=-1, keepdims=True) + EPS)
        xn = r * fg_ref[...]
    o_ref[...] = xn


def _out_proj(a_out, b_out, w_out_bf, x2d, gate, final_g, layer, final_norm):
    m = x2d.shape[0]
    tiles_per_batch = SEQ // OUT_TM
    return pl.pallas_call(
        functools.partial(_out_kernel, final_norm=final_norm),
        grid=(m // OUT_TM,),
        in_specs=[
            pl.BlockSpec((OUT_TM, A_WIDTH), lambda i: (i, 0)),
            pl.BlockSpec((OUT_TM, B_WIDTH), lambda i: (i, 0)),
            pl.BlockSpec((None, A_WIDTH, D_MODEL), lambda i: (layer, 0, 0)),
            pl.BlockSpec((None, B_WIDTH, D_MODEL), lambda i: (layer, 1, 0)),
            pl.BlockSpec((OUT_TM, D_MODEL), lambda i: (i, 0)),
            pl.BlockSpec((1, 1, D_MODEL), lambda i: (i // tiles_per_batch, 0, 0)),
            pl.BlockSpec((1, D_MODEL), lambda i: (0, 0)),
        ],
        out_specs=pl.BlockSpec((OUT_TM, D_MODEL), lambda i: (i, 0)),
        out_shape=jax.ShapeDtypeStruct((m, D_MODEL), jnp.float32),
        compiler_params=pltpu.CompilerParams(
            dimension_semantics=("arbitrary",), vmem_limit_bytes=VMEM_LIMIT),
        name="out_proj",
    )(a_out, b_out, w_out_bf, w_out_bf, x2d, gate, final_g.reshape(1, D_MODEL))


def kernel(x, c, w_ada, b_ada, norm_g, w_in, w_alpha_up, b_alpha, gla_g, w_out, rel_bias, final_g):
    mod = _adaln_mod(c, w_ada, b_ada)
    band = _bias_band(rel_bias)
    w_in_p = _pack_w_in(w_in)
    w_out_bf = w_out.astype(jnp.bfloat16)
    x2d = x.reshape(BATCH * SEQ, D_MODEL)
    for l in range(DEPTH):
        shift = mod[l, :, 0:D_MODEL].reshape(BATCH, 1, D_MODEL)
        scale = mod[l, :, D_MODEL:2 * D_MODEL].reshape(BATCH, 1, D_MODEL)
        gate = mod[l, :, 2 * D_MODEL:].reshape(BATCH, 1, D_MODEL)
        proj_b, proj_f = _norm_proj(x2d, shift, scale, norm_g[l], w_in_p, l)
        proj_b = proj_b.reshape(BATCH, SEQ, NB_COLS)
        proj_f = proj_f.reshape(BATCH, SEQ, NF_COLS)
        a_out = _dsa_attention(proj_b, proj_f, band)
        b_out = _gla(proj_f, w_alpha_up[l], b_alpha[l], gla_g[l])
        x2d = _out_proj(a_out.reshape(BATCH * SEQ, A_WIDTH), b_out.reshape(BATCH * SEQ, B_WIDTH),
                        w_out_bf, x2d, gate, final_g, l, final_norm=(l == DEPTH - 1))
    return x2d.reshape(BATCH, SEQ, D_MODEL)
```

```python
import functools
import math

import numpy as np
import jax
import jax.numpy as jnp
from jax import lax
from jax.experimental import pallas as pl
from jax.experimental.pallas import tpu as pltpu

D_MODEL = 2048
BATCH = 4
SEQ = 2048
DEPTH = 4
A_WIDTH = 1024
A_HEADS = 8
A_HEAD_DIM = 128
IDX_HEADS = 16
IDX_DIM = 64
TOPK = min(256, SEQ // 4)
B_WIDTH = 1024
B_HEADS = 4
B_KEY_WIDTH = 512
B_DK = 128
B_DV = 256
GATE_RANK = 16
GATE_TEMP = 16.0
NUM_BUCKETS = 32
MAX_DISTANCE = 128
EPS = 1e-6

IN_WIDTHS = (A_WIDTH, A_WIDTH, A_WIDTH, A_WIDTH, IDX_HEADS * IDX_DIM, IDX_DIM, IDX_HEADS,
             B_KEY_WIDTH, B_KEY_WIDTH, B_WIDTH, B_WIDTH, GATE_RANK)
IN_NAMES = ("aq", "ak", "av", "ag", "iq", "ik", "iw", "bq", "bk", "bv", "bg", "ba")
IN_OFFSETS = dict(zip(IN_NAMES, np.concatenate([[0], np.cumsum(IN_WIDTHS)[:-1]]).tolist()))
IN_WIDTH_OF = dict(zip(IN_NAMES, IN_WIDTHS))

LANE = 128
VMEM_LIMIT = 52 * 1024 * 1024

PROJ_TN = 512
BF_SEGS = (("aq", 1024), ("ak", 1024), ("av", 1024))
F32_SEGS = (("ag", 1024), ("iq", 1024), ("bv", 1024), ("bg", 1024), ("bq", 512), ("bk", 512),
            ("ik", LANE), ("iw", LANE), ("ba", LANE))
NB_COLS = sum(w for _, w in BF_SEGS)
NF_USED = sum(w for _, w in F32_SEGS)
NF_COLS = -(-NF_USED // PROJ_TN) * PROJ_TN
NP_COLS = NB_COLS + NF_COLS


def _seg_offsets(segs):
    offs, o = {}, 0
    for name, w in segs:
        offs[name] = o
        o += w
    return offs


BF_OFF = _seg_offsets(BF_SEGS)
F32_OFF = _seg_offsets(F32_SEGS)

INT_MIN = -2 ** 31
NEG_BIG = -1e30


def _pack_w_in(w_in):
    parts = []
    for name, w in BF_SEGS + F32_SEGS:
        seg = w_in[:, :, IN_OFFSETS[name]:IN_OFFSETS[name] + IN_WIDTH_OF[name]]
        pad = w - IN_WIDTH_OF[name]
        if pad:
            seg = jnp.pad(seg, ((0, 0), (0, 0), (0, pad)))
        parts.append(seg)
    tail = NP_COLS - NB_COLS - NF_USED
    if tail:
        parts.append(jnp.zeros(w_in.shape[:2] + (tail,), w_in.dtype))
    return jnp.concatenate(parts, axis=-1).astype(jnp.bfloat16)


MOD_TN = 768


def _mod_kernel(c_ref, w_ref, b_ref, o_ref):
    c = c_ref[...]
    c_act = c * jax.nn.sigmoid(c)
    acc = jnp.dot(c_act.astype(jnp.bfloat16), w_ref[0].astype(jnp.bfloat16),
                  preferred_element_type=jnp.float32)
    o_ref[0] = acc + b_ref[0]


def _adaln_mod(c, w_ada, b_ada):
    cp = jnp.pad(c, ((0, 8 - BATCH), (0, 0)))
    out = pl.pallas_call(
        _mod_kernel,
        grid=(DEPTH, 3 * D_MODEL // MOD_TN),
        in_specs=[
            pl.BlockSpec((8, D_MODEL), lambda l, j: (0, 0)),
            pl.BlockSpec((1, D_MODEL, MOD_TN), lambda l, j: (l, 0, j)),
            pl.BlockSpec((1, 1, MOD_TN), lambda l, j: (l, 0, j)),
        ],
        out_specs=pl.BlockSpec((1, 8, MOD_TN), lambda l, j: (l, 0, j)),
        out_shape=jax.ShapeDtypeStruct((DEPTH, 8, 3 * D_MODEL), jnp.float32),
        compiler_params=pltpu.CompilerParams(
            dimension_semantics=("arbitrary", "arbitrary"), vmem_limit_bytes=VMEM_LIMIT),
        name="adaln_mod",
    )(cp, w_ada, b_ada.reshape(DEPTH, 1, 3 * D_MODEL))
    return out[:, :BATCH]


PROJ_TM = 1024
NB_TILES = NB_COLS // PROJ_TN


def _proj_kernel(x_ref, shift_ref, scale_ref, g_ref, w_ref, ob_ref, of_ref, h_scr):
    j = pl.program_id(1)

    @pl.when(j == 0)
    def _():
        x = x_ref[...]
        r = x * lax.rsqrt(jnp.mean(x * x, axis=-1, keepdims=True) + EPS)
        h = (r * g_ref[...]) * (1.0 + scale_ref[0]) + shift_ref[0]
        h_scr[...] = h.astype(jnp.bfloat16)

    acc = jnp.dot(h_scr[...], w_ref[...], preferred_element_type=jnp.float32)

    @pl.when(j < NB_TILES)
    def _():
        ob_ref[...] = acc.astype(jnp.bfloat16)

    @pl.when(j >= NB_TILES)
    def _():
        of_ref[...] = acc


def _norm_proj(x2d, shift, scale, norm_g, w_p, layer):
    m = x2d.shape[0]
    tiles_per_batch = SEQ // PROJ_TM
    return pl.pallas_call(
        _proj_kernel,
        grid=(m // PROJ_TM, NP_COLS // PROJ_TN),
        in_specs=[
            pl.BlockSpec((PROJ_TM, D_MODEL), lambda i, j: (i, 0)),
            pl.BlockSpec((1, 1, D_MODEL), lambda i, j: (i // tiles_per_batch, 0, 0)),
            pl.BlockSpec((1, 1, D_MODEL), lambda i, j: (i // tiles_per_batch, 0, 0)),
            pl.BlockSpec((1, D_MODEL), lambda i, j: (0, 0)),
            pl.BlockSpec((None, D_MODEL, PROJ_TN), lambda i, j: (layer, 0, j)),
        ],
        out_specs=[
            pl.BlockSpec((PROJ_TM, PROJ_TN), lambda i, j: (i, jnp.minimum(j, NB_TILES - 1))),
            pl.BlockSpec((PROJ_TM, PROJ_TN), lambda i, j: (i, jnp.maximum(j - NB_TILES, 0))),
        ],
        out_shape=[
            jax.ShapeDtypeStruct((m, NB_COLS), jnp.bfloat16),
            jax.ShapeDtypeStruct((m, NF_COLS), jnp.float32),
        ],
        scratch_shapes=[pltpu.VMEM((PROJ_TM, D_MODEL), jnp.bfloat16)],
        compiler_params=pltpu.CompilerParams(
            dimension_semantics=("arbitrary", "arbitrary"), vmem_limit_bytes=VMEM_LIMIT),
        name="norm_proj",
    )(x2d, shift, scale, norm_g.reshape(1, D_MODEL), w_p)


TQ = 256
KB = 256
NQ = SEQ // TQ
NKB = SEQ // KB
HALF = KB // 2


def _t5_bucket_np(d):
    max_exact = NUM_BUCKETS // 2
    d = np.maximum(d, 0)
    df = np.maximum(d, 1).astype(np.float32)
    large = max_exact + (np.log(df / np.float32(max_exact)) / np.float32(math.log(MAX_DISTANCE / max_exact))
                         * np.float32(NUM_BUCKETS - max_exact)).astype(np.int32)
    large = np.minimum(large, NUM_BUCKETS - 1)
    return np.where(d < max_exact, d, large).astype(np.int32)


assert int(_t5_bucket_np(np.arange(KB, 2 * SEQ)).min()) == NUM_BUCKETS - 1


def _band_buckets():
    tl = np.arange(TQ)[:, None]
    u = np.arange(KB)[None, :]
    prev = _t5_bucket_np(KB + tl - u)
    diag = _t5_bucket_np(tl - u)
    return np.stack([prev, diag]).astype(np.int32)


BAND_ROWS = 64


def _band_kernel(rb_ref, bucket_ref, o_ref):
    h = pl.program_id(0)
    far = rb_ref[NUM_BUCKETS - 1, h]

    def rows(r, carry):
        rs = pl.ds(pl.multiple_of(r * BAND_ROWS, BAND_ROWS), BAND_ROWS)
        bucket = bucket_ref[0, rs, :]
        acc = jnp.zeros(bucket.shape, jnp.float32)
        for b in range(NUM_BUCKETS):
            acc = jnp.where(bucket == b, rb_ref[b, h] - far, acc)
        o_ref[0, 0, rs, :] = acc
        return carry

    lax.fori_loop(0, TQ // BAND_ROWS, rows, 0)


def _bias_band(rel_bias):
    return pl.pallas_call(
        _band_kernel,
        grid=(A_HEADS, 2),
        in_specs=[
            pl.BlockSpec(memory_space=pltpu.SMEM),
            pl.BlockSpec((1, TQ, KB), lambda h, s: (s, 0, 0)),
        ],
        out_specs=pl.BlockSpec((1, 1, TQ, KB), lambda h, s: (h, s, 0, 0)),
        out_shape=jax.ShapeDtypeStruct((A_HEADS, 2, TQ, KB), jnp.float32),
        name="bias_band",
    )(rel_bias, jnp.asarray(_band_buckets()))


def _dot_nt(a, b):
    return lax.dot_general(a, b, (((1,), (1,)), ((), ())), preferred_element_type=jnp.float32)


def _dot_tn(a, b):
    return lax.dot_general(a, b, (((0,), (0,)), ((), ())), preferred_element_type=jnp.float32)


def _dsa_kernel(q_ref, k_ref, v_ref, iq_ref, ik_ref, iw_ref, ag_ref, band_ref, o_ref,
                iklo_scr, ikhi_scr, vext_scr, wb_scr, iq2_scr, key_scr, keyt_scr, thrn_scr,
                madd_scr, lg_scr, acc_scr, mb_scr):
    qi = pl.program_id(1)
    nkb = qi + 1
    bf16 = jnp.bfloat16

    def key_rows(kb):
        return pl.ds(pl.multiple_of(kb * KB, KB), KB)

    @pl.when(qi == 0)
    def _():
        def prep(c, carry):
            rs = key_rows(c)
            ik = ik_ref[0, rs, :]
            iklo_scr[rs, :] = ik.astype(bf16)
            ikhi_scr[rs, :] = pltpu.roll(ik, IDX_DIM, axis=1).astype(bf16)
            for h in range(A_HEADS):
                vext_scr[h, rs, :A_HEAD_DIM] = v_ref[0, rs, h * A_HEAD_DIM:(h + 1) * A_HEAD_DIM]
                vext_scr[h, rs, A_HEAD_DIM:] = jnp.ones((KB, A_HEAD_DIM), bf16)
            return carry

        lax.fori_loop(0, NKB, prep, 0)

    iw = iw_ref[0][:, :IDX_HEADS] * (IDX_HEADS ** -0.5 * IDX_DIM ** -0.5)
    for j in range(IDX_HEADS):
        wb_scr[j] = jnp.broadcast_to(iw[:, j:j + 1], (TQ, HALF))
    for jp in range(IDX_HEADS // 2):
        iq2_scr[jp * TQ:(jp + 1) * TQ, :] = iq_ref[0, :, jp * LANE:(jp + 1) * LANE].astype(bf16)

    row = lax.broadcasted_iota(jnp.int32, (TQ, HALF), 0)
    col = lax.broadcasted_iota(jnp.int32, (TQ, HALF), 1)

    def score_block(kb, carry):
        rs = key_rows(kb)
        s_lo = _dot_nt(iq2_scr[...], iklo_scr[rs, :])
        s_hi = _dot_nt(iq2_scr[...], ikhi_scr[rs, :])
        for half in range(2):
            ls = slice(half * HALF, (half + 1) * HALF)
            acc = jnp.zeros((TQ, HALF), jnp.float32)
            for jp in range(IDX_HEADS // 2):
                rj = slice(jp * TQ, (jp + 1) * TQ)
                acc = (acc + jnp.maximum(s_lo[rj, ls], 0.0) * wb_scr[2 * jp]
                       + jnp.maximum(s_hi[rj, ls], 0.0) * wb_scr[2 * jp + 1])
            bits = pltpu.bitcast(acc, jnp.int32)
            key = jnp.where(bits >= 0, bits, bits ^ 0x7FFFFFFF)
            causal = (kb * KB + half * HALF + col) <= (qi * TQ + row)
            key_scr[kb, :, ls] = jnp.where(causal, key, INT_MIN)
        keyt_scr[kb] = key_scr[kb].T
        return carry

    lax.fori_loop(0, nkb, score_block, 0)

    def count_ge(cand):
        def body(kb, cnt):
            hit = (keyt_scr[kb] >= cand).astype(jnp.int32)
            return cnt + jnp.sum(hit.reshape(KB // 8, 8, TQ), axis=0)
        cnt = lax.fori_loop(0, nkb, body, jnp.zeros((8, TQ), jnp.int32))
        return jnp.sum(cnt, axis=0, keepdims=True)

    zero = jnp.zeros((1, TQ), jnp.int32)
    prefix = jnp.where(count_ge(zero) >= TOPK, zero, INT_MIN)

    def bit_pass(it, prefix):
        cand = prefix + lax.shift_left(jnp.int32(1), 30 - it)
        return jnp.where(count_ge(cand) >= TOPK, cand, prefix)

    prefix = lax.fori_loop(0, 31, bit_pass, prefix)
    thr = jnp.maximum(prefix, INT_MIN + 1)
    thrn_scr[...] = jnp.broadcast_to(thr, (KB, TQ)).T

    def madd_block(kb, carry):
        madd_scr[kb] = jnp.where(key_scr[kb] >= thrn_scr[...], 0.0, NEG_BIG)
        return carry

    lax.fori_loop(0, nkb, madd_block, 0)

    scale = A_HEAD_DIM ** -0.5
    kb_prev = jnp.maximum(qi - 1, 0)
    for h in range(A_HEADS):
        hs = slice(h * A_HEAD_DIM, (h + 1) * A_HEAD_DIM)
        qh = q_ref[0, :, hs]

        def logits(kb, qh=qh, hs=hs):
            return _dot_nt(qh, k_ref[0, key_rows(kb), hs]) * scale + madd_scr[kb]

        def fold_max(m_run, lg):
            return jnp.maximum(m_run, jnp.maximum(lg[:, :HALF], lg[:, HALF:]))

        def far_block(kb, m_run, logits=logits):
            lg = logits(kb)
            lg_scr[kb] = lg
            return fold_max(m_run, lg)

        m_run = lax.fori_loop(0, kb_prev, far_block, jnp.full((TQ, HALF), NEG_BIG, jnp.float32))
        lg = jnp.where(qi >= 1, logits(kb_prev) + band_ref[h, 0], NEG_BIG)
        lg_scr[kb_prev] = lg
        m_run = fold_max(m_run, lg)
        lg = logits(qi) + band_ref[h, 1]
        lg_scr[qi] = lg
        m_run = fold_max(m_run, lg)
        mb_scr[...] = jnp.broadcast_to(jnp.max(m_run, axis=-1, keepdims=True), (TQ, KB))
        acc_scr[...] = jnp.zeros_like(acc_scr)

        def pv_block(kb, carry, h=h):
            p = jnp.exp(lg_scr[kb] - mb_scr[...]).astype(bf16)
            acc_scr[...] += jnp.dot(p, vext_scr[h, key_rows(kb), :], preferred_element_type=jnp.float32)
            return carry

        lax.fori_loop(0, nkb, pv_block, 0)
        acc = acc_scr[...]
        g = ag_ref[0, :, hs]
        o_ref[0, :, hs] = (acc[:, :A_HEAD_DIM] / acc[:, A_HEAD_DIM:]
                           * (g * jax.nn.sigmoid(g))).astype(o_ref.dtype)


def _dsa_attention(proj_b, proj_f, band):
    cb = lambda name: BF_OFF[name] // A_WIDTH
    once = pl.Buffered(1)
    return pl.pallas_call(
        _dsa_kernel,
        grid=(BATCH, NQ),
        in_specs=[
            pl.BlockSpec((1, TQ, A_WIDTH), lambda b, i: (b, i, cb("aq"))),
            pl.BlockSpec((1, SEQ, A_WIDTH), lambda b, i: (b, 0, cb("ak")), pipeline_mode=once),
            pl.BlockSpec((1, SEQ, A_WIDTH), lambda b, i: (b, 0, cb("av")), pipeline_mode=once),
            pl.BlockSpec((1, TQ, A_WIDTH), lambda b, i: (b, i, F32_OFF["iq"] // A_WIDTH)),
            pl.BlockSpec((1, SEQ, LANE), lambda b, i: (b, 0, F32_OFF["ik"] // LANE), pipeline_mode=once),
            pl.BlockSpec((1, TQ, LANE), lambda b, i: (b, i, F32_OFF["iw"] // LANE)),
            pl.BlockSpec((1, TQ, A_WIDTH), lambda b, i: (b, i, F32_OFF["ag"] // A_WIDTH)),
            pl.BlockSpec((A_HEADS, 2, TQ, KB), lambda b, i: (0, 0, 0, 0), pipeline_mode=once),
        ],
        out_specs=pl.BlockSpec((1, TQ, A_WIDTH), lambda b, i: (b, i, 0)),
        out_shape=jax.ShapeDtypeStruct((BATCH, SEQ, A_WIDTH), jnp.bfloat16),
        scratch_shapes=[
            pltpu.VMEM((SEQ, LANE), jnp.bfloat16),
            pltpu.VMEM((SEQ, LANE), jnp.bfloat16),
            pltpu.VMEM((A_HEADS, SEQ, 2 * A_HEAD_DIM), jnp.bfloat16),
            pltpu.VMEM((IDX_HEADS, TQ, HALF), jnp.float32),
            pltpu.VMEM((IDX_HEADS // 2 * TQ, LANE), jnp.bfloat16),
            pltpu.VMEM((NKB, TQ, KB), jnp.int32),
            pltpu.VMEM((NKB, KB, TQ), jnp.int32),
            pltpu.VMEM((TQ, KB), jnp.int32),
            pltpu.VMEM((NKB, TQ, KB), jnp.float32),
            pltpu.VMEM((NKB, TQ, KB), jnp.float32),
            pltpu.VMEM((TQ, 2 * A_HEAD_DIM), jnp.float32),
            pltpu.VMEM((TQ, KB), jnp.float32),
        ],
        compiler_params=pltpu.CompilerParams(
            dimension_semantics=("arbitrary", "arbitrary"), vmem_limit_bytes=VMEM_LIMIT),
        name="dsa_attention",
    )(proj_b, proj_b, proj_b, proj_f, proj_f, proj_f, proj_f, band)


GLA_CT = 512
GLA_C = 64


def _gla_kernel(bq_ref, bk_ref, bv_ref, bg_ref, ba_ref, wup_ref, balpha_ref, g_ref, o_ref, st_scr):
    @pl.when(pl.program_id(1) == 0)
    def _():
        st_scr[...] = jnp.zeros_like(st_scr)

    rr = lax.broadcasted_iota(jnp.int32, (GLA_C, GLA_C), 0)
    cc = lax.broadcasted_iota(jnp.int32, (GLA_C, GLA_C), 1)
    tri = rr >= cc
    tri_bf = tri.astype(jnp.bfloat16)
    wup = wup_ref[...].astype(jnp.bfloat16)
    balpha = balpha_ref[...]
    gain = g_ref[...]

    def chunk(c, carry):
        r0 = pl.multiple_of(c * GLA_C, GLA_C)
        rows = pl.ds(r0, GLA_C)
        ba = ba_ref[0, rows, :][:, :GATE_RANK].astype(jnp.bfloat16)
        pre = jnp.dot(ba, wup, preferred_element_type=jnp.float32) + balpha
        log_a = (jnp.minimum(pre, 0.0) - jnp.log1p(jnp.exp(-jnp.abs(pre)))) * (1.0 / GATE_TEMP)
        la_hi = log_a.astype(jnp.bfloat16)
        la_lo = (log_a - la_hi.astype(jnp.float32)).astype(jnp.bfloat16)
        bcum = (jnp.dot(tri_bf, la_hi, preferred_element_type=jnp.float32)
                + jnp.dot(tri_bf, la_lo, preferred_element_type=jnp.float32))
        for h in range(B_HEADS):
            ks = slice(h * B_DK, (h + 1) * B_DK)
            vs = slice(h * B_DV, (h + 1) * B_DV)
            b = bcum[:, ks]
            b_last = b[GLA_C - 1:GLA_C, :]
            q = bq_ref[0, rows, ks] * (B_DK ** -0.5)
            k = bk_ref[0, rows, ks]
            v = bv_ref[0, rows, vs].astype(jnp.bfloat16)
            qe = (q * jnp.exp(b)).astype(jnp.bfloat16)
            ke = (k * jnp.exp(-b)).astype(jnp.bfloat16)
            kd = (k * jnp.exp(b_last - b)).astype(jnp.bfloat16)
            attn = jnp.where(tri, _dot_nt(qe, ke), 0.0)
            st = st_scr[h]
            o = _dot_nt(qe, st.astype(jnp.bfloat16)) + jnp.dot(
                attn.astype(jnp.bfloat16), v, preferred_element_type=jnp.float32)
            st_scr[h] = st * jnp.exp(b_last) + _dot_tn(v, kd)
            on = o * lax.rsqrt(jnp.mean(o * o, axis=-1, keepdims=True) + EPS) * gain
            g = bg_ref[0, rows, vs]
            o_ref[0, rows, vs] = (on * (g * jax.nn.sigmoid(g))).astype(o_ref.dtype)
        return carry

    lax.fori_loop(0, GLA_CT // GLA_C, chunk, 0)


def _gla(proj_f, w_alpha_up, b_alpha, gla_g):
    return pl.pallas_call(
        _gla_kernel,
        grid=(BATCH, SEQ // GLA_CT),
        in_specs=[
            pl.BlockSpec((1, GLA_CT, B_KEY_WIDTH), lambda b, t: (b, t, F32_OFF["bq"] // B_KEY_WIDTH)),
            pl.BlockSpec((1, GLA_CT, B_KEY_WIDTH), lambda b, t: (b, t, F32_OFF["bk"] // B_KEY_WIDTH)),
            pl.BlockSpec((1, GLA_CT, B_WIDTH), lambda b, t: (b, t, F32_OFF["bv"] // B_WIDTH)),
            pl.BlockSpec((1, GLA_CT, B_WIDTH), lambda b, t: (b, t, F32_OFF["bg"] // B_WIDTH)),
            pl.BlockSpec((1, GLA_CT, LANE), lambda b, t: (b, t, F32_OFF["ba"] // LANE)),
            pl.BlockSpec((GATE_RANK, B_KEY_WIDTH), lambda b, t: (0, 0)),
            pl.BlockSpec((1, B_KEY_WIDTH), lambda b, t: (0, 0)),
            pl.BlockSpec((1, B_DV), lambda b, t: (0, 0)),
        ],
        out_specs=pl.BlockSpec((1, GLA_CT, B_WIDTH), lambda b, t: (b, t, 0)),
        out_shape=jax.ShapeDtypeStruct((BATCH, SEQ, B_WIDTH), jnp.bfloat16),
        scratch_shapes=[pltpu.VMEM((B_HEADS, B_DV, B_DK), jnp.float32)],
        compiler_params=pltpu.CompilerParams(
            dimension_semantics=("arbitrary", "arbitrary"), vmem_limit_bytes=VMEM_LIMIT),
        name="gla",
    )(proj_f, proj_f, proj_f, proj_f, proj_f, w_alpha_up, b_alpha.reshape(1, B_KEY_WIDTH),
      gla_g.reshape(1, B_DV))


OUT_TM = 512


def _out_kernel(a_ref, b_ref, wa_ref, wb_ref, x_ref, gate_ref, fg_ref, o_ref, *, final_norm):
    y = (jnp.dot(a_ref[...], wa_ref[...], preferred_element_type=jnp.float32)
         + jnp.dot(b_ref[...], wb_ref[...], preferred_element_type=jnp.float32))
    xn = x_ref[...] + gate_ref[0] * y
    if final_norm:
        r = xn * lax.rsqrt(jnp.mean(xn * xn, axis=-1, keepdims=True) + EPS)
        xn = r * fg_ref[...]
    o_ref[...] = xn


def _out_proj(a_out, b_out, w_out_bf, x2d, gate, final_g, layer, final_norm):
    m = x2d.shape[0]
    tiles_per_batch = SEQ // OUT_TM
    return pl.pallas_call(
        functools.partial(_out_kernel, final_norm=final_norm),
        grid=(m // OUT_TM,),
        in_specs=[
            pl.BlockSpec((OUT_TM, A_WIDTH), lambda i: (i, 0)),
            pl.BlockSpec((OUT_TM, B_WIDTH), lambda i: (i, 0)),
            pl.BlockSpec((None, A_WIDTH, D_MODEL), lambda i: (layer, 0, 0)),
            pl.BlockSpec((None, B_WIDTH, D_MODEL), lambda i: (layer, 1, 0)),
            pl.BlockSpec((OUT_TM, D_MODEL), lambda i: (i, 0)),
            pl.BlockSpec((1, 1, D_MODEL), lambda i: (i // tiles_per_batch, 0, 0)),
            pl.BlockSpec((1, D_MODEL), lambda i: (0, 0)),
        ],
        out_specs=pl.BlockSpec((OUT_TM, D_MODEL), lambda i: (i, 0)),
        out_shape=jax.ShapeDtypeStruct((m, D_MODEL), jnp.float32),
        compiler_params=pltpu.CompilerParams(
            dimension_semantics=("arbitrary",), vmem_limit_bytes=VMEM_LIMIT),
        name="out_proj",
    )(a_out, b_out, w_out_bf, w_out_bf, x2d, gate, final_g.reshape(1, D_MODEL))


def kernel(x, c, w_ada, b_ada, norm_g, w_in, w_alpha_up, b_alpha, gla_g, w_out, rel_bias, final_g):
    mod = _adaln_mod(c, w_ada, b_ada)
    band = _bias_band(rel_bias)
    w_in_p = _pack_w_in(w_in)
    w_out_bf = w_out.astype(jnp.bfloat16)
    x2d = x.reshape(BATCH * SEQ, D_MODEL)
    for l in range(DEPTH):
        shift = mod[l, :, 0:D_MODEL].reshape(BATCH, 1, D_MODEL)
        scale = mod[l, :, D_MODEL:2 * D_MODEL].reshape(BATCH, 1, D_MODEL)
        gate = mod[l, :, 2 * D_MODEL:].reshape(BATCH, 1, D_MODEL)
        proj_b, proj_f = _norm_proj(x2d, shift, scale, norm_g[l], w_in_p, l)
        proj_b = proj_b.reshape(BATCH, SEQ, NB_COLS)
        proj_f = proj_f.reshape(BATCH, SEQ, NF_COLS)
        a_out = _dsa_attention(proj_b, proj_f, band)
        b_out = _gla(proj_f, w_alpha_up[l], b_alpha[l], gla_g[l])
        x2d = _out_proj(a_out.reshape(BATCH * SEQ, A_WIDTH), b_out.reshape(BATCH * SEQ, B_WIDTH),
                        w_out_bf, x2d, gate, final_g, l, final_norm=(l == DEPTH - 1))
    return x2d.reshape(BATCH, SEQ, D_MODEL)
```

```python
import functools
import math

import numpy as np
import jax
import jax.numpy as jnp
from jax import lax
from jax.experimental import pallas as pl
from jax.experimental.pallas import tpu as pltpu

D_MODEL = 2048
BATCH = 4
SEQ = 2048
DEPTH = 4
A_WIDTH = 1024
A_HEADS = 8
A_HEAD_DIM = 128
IDX_HEADS = 16
IDX_DIM = 64
TOPK = min(256, SEQ // 4)
B_WIDTH = 1024
B_HEADS = 4
B_KEY_WIDTH = 512
B_DK = 128
B_DV = 256
GATE_RANK = 16
GATE_TEMP = 16.0
NUM_BUCKETS = 32
MAX_DISTANCE = 128
EPS = 1e-6

IN_WIDTHS = (A_WIDTH, A_WIDTH, A_WIDTH, A_WIDTH, IDX_HEADS * IDX_DIM, IDX_DIM, IDX_HEADS,
             B_KEY_WIDTH, B_KEY_WIDTH, B_WIDTH, B_WIDTH, GATE_RANK)
IN_NAMES = ("aq", "ak", "av", "ag", "iq", "ik", "iw", "bq", "bk", "bv", "bg", "ba")
IN_OFFSETS = dict(zip(IN_NAMES, np.concatenate([[0], np.cumsum(IN_WIDTHS)[:-1]]).tolist()))
IN_WIDTH_OF = dict(zip(IN_NAMES, IN_WIDTHS))

LANE = 128
VMEM_LIMIT = 52 * 1024 * 1024

PROJ_TN = 512
BF_SEGS = (("aq", 1024), ("ak", 1024), ("av", 1024))
F32_SEGS = (("ag", 1024), ("iq", 1024), ("bv", 1024), ("bg", 1024), ("bq", 512), ("bk", 512),
            ("ik", LANE), ("iw", LANE), ("ba", LANE))
NB_COLS = sum(w for _, w in BF_SEGS)
NF_USED = sum(w for _, w in F32_SEGS)
NF_COLS = -(-NF_USED // PROJ_TN) * PROJ_TN
NP_COLS = NB_COLS + NF_COLS


def _seg_offsets(segs):
    offs, o = {}, 0
    for name, w in segs:
        offs[name] = o
        o += w
    return offs


BF_OFF = _seg_offsets(BF_SEGS)
F32_OFF = _seg_offsets(F32_SEGS)

INT_MIN = -2 ** 31
NEG_BIG = -1e30
LOG2E = math.log2(math.e)


def _pack_w_in(w_in):
    parts = []
    for name, w in BF_SEGS + F32_SEGS:
        seg = w_in[:, :, IN_OFFSETS[name]:IN_OFFSETS[name] + IN_WIDTH_OF[name]]
        pad = w - IN_WIDTH_OF[name]
        if pad:
            seg = jnp.pad(seg, ((0, 0), (0, 0), (0, pad)))
        parts.append(seg)
    tail = NP_COLS - NB_COLS - NF_USED
    if tail:
        parts.append(jnp.zeros(w_in.shape[:2] + (tail,), w_in.dtype))
    return jnp.concatenate(parts, axis=-1).astype(jnp.bfloat16)


MOD_TN = 768


def _mod_kernel(c_ref, w_ref, b_ref, o_ref):
    c = c_ref[...]
    c_act = c * jax.nn.sigmoid(c)
    acc = jnp.dot(c_act.astype(jnp.bfloat16), w_ref[0].astype(jnp.bfloat16),
                  preferred_element_type=jnp.float32)
    o_ref[0] = acc + b_ref[0]


def _adaln_mod(c, w_ada, b_ada):
    cp = jnp.pad(c, ((0, 8 - BATCH), (0, 0)))
    out = pl.pallas_call(
        _mod_kernel,
        grid=(DEPTH, 3 * D_MODEL // MOD_TN),
        in_specs=[
            pl.BlockSpec((8, D_MODEL), lambda l, j: (0, 0)),
            pl.BlockSpec((1, D_MODEL, MOD_TN), lambda l, j: (l, 0, j)),
            pl.BlockSpec((1, 1, MOD_TN), lambda l, j: (l, 0, j)),
        ],
        out_specs=pl.BlockSpec((1, 8, MOD_TN), lambda l, j: (l, 0, j)),
        out_shape=jax.ShapeDtypeStruct((DEPTH, 8, 3 * D_MODEL), jnp.float32),
        compiler_params=pltpu.CompilerParams(
            dimension_semantics=("arbitrary", "arbitrary"), vmem_limit_bytes=VMEM_LIMIT),
        name="adaln_mod",
    )(cp, w_ada, b_ada.reshape(DEPTH, 1, 3 * D_MODEL))
    return out[:, :BATCH]


PROJ_TM = 1024
NB_TILES = NB_COLS // PROJ_TN


def _proj_kernel(x_ref, shift_ref, scale_ref, g_ref, w_ref, ob_ref, of_ref, h_scr):
    j = pl.program_id(1)

    @pl.when(j == 0)
    def _():
        x = x_ref[...]
        r = x * lax.rsqrt(jnp.mean(x * x, axis=-1, keepdims=True) + EPS)
        h = (r * g_ref[...]) * (1.0 + scale_ref[0]) + shift_ref[0]
        h_scr[...] = h.astype(jnp.bfloat16)

    acc = jnp.dot(h_scr[...], w_ref[...], preferred_element_type=jnp.float32)

    @pl.when(j < NB_TILES)
    def _():
        ob_ref[...] = acc.astype(jnp.bfloat16)

    @pl.when(j >= NB_TILES)
    def _():
        of_ref[...] = acc


def _norm_proj(x2d, shift, scale, norm_g, w_p, layer):
    m = x2d.shape[0]
    tiles_per_batch = SEQ // PROJ_TM
    return pl.pallas_call(
        _proj_kernel,
        grid=(m // PROJ_TM, NP_COLS // PROJ_TN),
        in_specs=[
            pl.BlockSpec((PROJ_TM, D_MODEL), lambda i, j: (i, 0)),
            pl.BlockSpec((1, 1, D_MODEL), lambda i, j: (i // tiles_per_batch, 0, 0)),
            pl.BlockSpec((1, 1, D_MODEL), lambda i, j: (i // tiles_per_batch, 0, 0)),
            pl.BlockSpec((1, D_MODEL), lambda i, j: (0, 0)),
            pl.BlockSpec((None, D_MODEL, PROJ_TN), lambda i, j: (layer, 0, j)),
        ],
        out_specs=[
            pl.BlockSpec((PROJ_TM, PROJ_TN), lambda i, j: (i, jnp.minimum(j, NB_TILES - 1))),
            pl.BlockSpec((PROJ_TM, PROJ_TN), lambda i, j: (i, jnp.maximum(j - NB_TILES, 0))),
        ],
        out_shape=[
            jax.ShapeDtypeStruct((m, NB_COLS), jnp.bfloat16),
            jax.ShapeDtypeStruct((m, NF_COLS), jnp.float32),
        ],
        scratch_shapes=[pltpu.VMEM((PROJ_TM, D_MODEL), jnp.bfloat16)],
        compiler_params=pltpu.CompilerParams(
            dimension_semantics=("arbitrary", "arbitrary"), vmem_limit_bytes=VMEM_LIMIT),
        name="norm_proj",
    )(x2d, shift, scale, norm_g.reshape(1, D_MODEL), w_p)


TQ = 256
KB = 256
NQ = SEQ // TQ
NKB = SEQ // KB
HALF = KB // 2


def _t5_bucket_np(d):
    max_exact = NUM_BUCKETS // 2
    d = np.maximum(d, 0)
    df = np.maximum(d, 1).astype(np.float32)
    large = max_exact + (np.log(df / np.float32(max_exact)) / np.float32(math.log(MAX_DISTANCE / max_exact))
                         * np.float32(NUM_BUCKETS - max_exact)).astype(np.int32)
    large = np.minimum(large, NUM_BUCKETS - 1)
    return np.where(d < max_exact, d, large).astype(np.int32)


assert int(_t5_bucket_np(np.arange(KB, 2 * SEQ)).min()) == NUM_BUCKETS - 1


def _band_buckets():
    tl = np.arange(TQ)[:, None]
    u = np.arange(KB)[None, :]
    prev = _t5_bucket_np(KB + tl - u)
    diag = _t5_bucket_np(tl - u)
    return np.stack([prev, diag]).astype(np.int32)


BAND_ROWS = 64


def _band_kernel(rb_ref, bucket_ref, o_ref):
    h = pl.program_id(0)
    far = rb_ref[NUM_BUCKETS - 1, h]

    def rows(r, carry):
        rs = pl.ds(pl.multiple_of(r * BAND_ROWS, BAND_ROWS), BAND_ROWS)
        bucket = bucket_ref[0, rs, :]
        acc = jnp.zeros(bucket.shape, jnp.float32)
        for b in range(NUM_BUCKETS):
            acc = jnp.where(bucket == b, rb_ref[b, h] - far, acc)
        o_ref[0, 0, rs, :] = acc * LOG2E
        return carry

    lax.fori_loop(0, TQ // BAND_ROWS, rows, 0)


def _bias_band(rel_bias):
    return pl.pallas_call(
        _band_kernel,
        grid=(A_HEADS, 2),
        in_specs=[
            pl.BlockSpec(memory_space=pltpu.SMEM),
            pl.BlockSpec((1, TQ, KB), lambda h, s: (s, 0, 0)),
        ],
        out_specs=pl.BlockSpec((1, 1, TQ, KB), lambda h, s: (h, s, 0, 0)),
        out_shape=jax.ShapeDtypeStruct((A_HEADS, 2, TQ, KB), jnp.float32),
        name="bias_band",
    )(rel_bias, jnp.asarray(_band_buckets()))


def _dot_nt(a, b):
    return lax.dot_general(a, b, (((1,), (1,)), ((), ())), preferred_element_type=jnp.float32)


def _dot_tn(a, b):
    return lax.dot_general(a, b, (((0,), (0,)), ((), ())), preferred_element_type=jnp.float32)


def _dsa_kernel(q_ref, k_ref, v_ref, iq_ref, ik_ref, iw_ref, ag_ref, band_ref, o_ref,
                iklo_scr, ikhi_scr, vext_scr, wb_scr, iq2_scr, key_scr, keyt_scr, thrn_scr,
                madd_scr, m_scr, accv_scr, accl_scr):
    qi = pl.program_id(1)
    nkb = qi + 1
    bf16 = jnp.bfloat16

    def key_rows(kb):
        return pl.ds(pl.multiple_of(kb * KB, KB), KB)

    @pl.when(qi == 0)
    def _():
        def prep(c, carry):
            rs = key_rows(c)
            ik = ik_ref[0, rs, :]
            iklo_scr[rs, :] = ik.astype(bf16)
            ikhi_scr[rs, :] = pltpu.roll(ik, IDX_DIM, axis=1).astype(bf16)
            for h in range(A_HEADS):
                vext_scr[h, rs, :A_HEAD_DIM] = v_ref[0, rs, h * A_HEAD_DIM:(h + 1) * A_HEAD_DIM]
                vext_scr[h, rs, A_HEAD_DIM:] = jnp.ones((KB, A_HEAD_DIM), bf16)
            return carry

        lax.fori_loop(0, NKB, prep, 0)

    iw = iw_ref[0][:, :IDX_HEADS] * (IDX_HEADS ** -0.5 * IDX_DIM ** -0.5)
    for j in range(IDX_HEADS):
        wb_scr[j] = jnp.broadcast_to(iw[:, j:j + 1], (TQ, HALF))
    for jp in range(IDX_HEADS // 2):
        iq2_scr[jp * TQ:(jp + 1) * TQ, :] = iq_ref[0, :, jp * LANE:(jp + 1) * LANE].astype(bf16)

    row = lax.broadcasted_iota(jnp.int32, (TQ, HALF), 0)
    col = lax.broadcasted_iota(jnp.int32, (TQ, HALF), 1)

    def score_block(kb, carry):
        rs = key_rows(kb)
        s_lo = _dot_nt(iq2_scr[...], iklo_scr[rs, :])
        s_hi = _dot_nt(iq2_scr[...], ikhi_scr[rs, :])
        for half in range(2):
            ls = slice(half * HALF, (half + 1) * HALF)
            acc = jnp.zeros((TQ, HALF), jnp.float32)
            for jp in range(IDX_HEADS // 2):
                rj = slice(jp * TQ, (jp + 1) * TQ)
                acc = (acc + jnp.maximum(s_lo[rj, ls], 0.0) * wb_scr[2 * jp]
                       + jnp.maximum(s_hi[rj, ls], 0.0) * wb_scr[2 * jp + 1])
            bits = pltpu.bitcast(acc, jnp.int32)
            key = jnp.where(bits >= 0, bits, bits ^ 0x7FFFFFFF)
            causal = (kb * KB + half * HALF + col) <= (qi * TQ + row)
            key_scr[kb, :, ls] = jnp.where(causal, key, INT_MIN)
        keyt_scr[kb] = key_scr[kb].T
        return carry

    lax.fori_loop(0, nkb, score_block, 0)

    def count_ge(cand):
        def body(kb, cnt):
            hit = (keyt_scr[kb] >= cand).astype(jnp.int32)
            return cnt + jnp.sum(hit.reshape(KB // 8, 8, TQ), axis=0)
        cnt = lax.fori_loop(0, nkb, body, jnp.zeros((8, TQ), jnp.int32))
        return jnp.sum(cnt, axis=0, keepdims=True)

    zero = jnp.zeros((1, TQ), jnp.int32)
    prefix = jnp.where(count_ge(zero) >= TOPK, zero, INT_MIN)

    def bit_pass(it, prefix):
        cand = prefix + lax.shift_left(jnp.int32(1), 30 - it)
        return jnp.where(count_ge(cand) >= TOPK, cand, prefix)

    prefix = lax.fori_loop(0, 31, bit_pass, prefix)
    thr = jnp.maximum(prefix, INT_MIN + 1)
    thrn_scr[...] = jnp.broadcast_to(thr, (KB, TQ)).T

    def madd_block(kb, carry):
        madd_scr[kb] = jnp.where(key_scr[kb] >= thrn_scr[...], 0.0, NEG_BIG)
        return carry

    lax.fori_loop(0, nkb, madd_block, 0)

    m_scr[...] = jnp.full(m_scr.shape, NEG_BIG, jnp.float32)
    accv_scr[...] = jnp.zeros_like(accv_scr)
    accl_scr[...] = jnp.zeros_like(accl_scr)
    scale2 = A_HEAD_DIM ** -0.5 * LOG2E

    def attend(kb, slot):
        rs = key_rows(kb)
        for h in range(A_HEADS):
            hs = slice(h * A_HEAD_DIM, (h + 1) * A_HEAD_DIM)
            lg = _dot_nt(q_ref[0, :, hs], k_ref[0, rs, hs]) * scale2 + madd_scr[kb]
            if slot is not None:
                lg = lg + band_ref[h, slot]
            m_old = m_scr[h]
            m_new = jnp.maximum(m_old, jnp.max(lg, axis=-1, keepdims=True))
            alpha = jnp.exp2(m_old - m_new)
            p = jnp.concatenate([jnp.exp2(lg[:, :HALF] - m_new), jnp.exp2(lg[:, HALF:] - m_new)], axis=1)
            pv = jnp.dot(p.astype(bf16), vext_scr[h, rs, :], preferred_element_type=jnp.float32)
            accv_scr[h] = accv_scr[h] * alpha + pv[:, :A_HEAD_DIM]
            accl_scr[h] = accl_scr[h] * alpha + pv[:, A_HEAD_DIM:]
            m_scr[h] = m_new

    def far_block(kb, carry):
        attend(kb, None)
        return carry

    lax.fori_loop(0, qi - 1, far_block, 0)

    @pl.when(qi >= 1)
    def _():
        attend(qi - 1, 0)

    attend(qi, 1)

    for h in range(A_HEADS):
        hs = slice(h * A_HEAD_DIM, (h + 1) * A_HEAD_DIM)
        g = ag_ref[0, :, hs]
        o_ref[0, :, hs] = (accv_scr[h] / accl_scr[h] * (g * jax.nn.sigmoid(g))).astype(o_ref.dtype)


def _dsa_attention(proj_b, proj_f, band):
    cb = lambda name: BF_OFF[name] // A_WIDTH
    once = pl.Buffered(1)
    return pl.pallas_call(
        _dsa_kernel,
        grid=(BATCH, NQ),
        in_specs=[
            pl.BlockSpec((1, TQ, A_WIDTH), lambda b, i: (b, i, cb("aq"))),
            pl.BlockSpec((1, SEQ, A_WIDTH), lambda b, i: (b, 0, cb("ak")), pipeline_mode=once),
            pl.BlockSpec((1, SEQ, A_WIDTH), lambda b, i: (b, 0, cb("av")), pipeline_mode=once),
            pl.BlockSpec((1, TQ, A_WIDTH), lambda b, i: (b, i, F32_OFF["iq"] // A_WIDTH)),
            pl.BlockSpec((1, SEQ, LANE), lambda b, i: (b, 0, F32_OFF["ik"] // LANE), pipeline_mode=once),
            pl.BlockSpec((1, TQ, LANE), lambda b, i: (b, i, F32_OFF["iw"] // LANE)),
            pl.BlockSpec((1, TQ, A_WIDTH), lambda b, i: (b, i, F32_OFF["ag"] // A_WIDTH)),
            pl.BlockSpec((A_HEADS, 2, TQ, KB), lambda b, i: (0, 0, 0, 0), pipeline_mode=once),
        ],
        out_specs=pl.BlockSpec((1, TQ, A_WIDTH), lambda b, i: (b, i, 0)),
        out_shape=jax.ShapeDtypeStruct((BATCH, SEQ, A_WIDTH), jnp.bfloat16),
        scratch_shapes=[
            pltpu.VMEM((SEQ, LANE), jnp.bfloat16),
            pltpu.VMEM((SEQ, LANE), jnp.bfloat16),
            pltpu.VMEM((A_HEADS, SEQ, 2 * A_HEAD_DIM), jnp.bfloat16),
            pltpu.VMEM((IDX_HEADS, TQ, HALF), jnp.float32),
            pltpu.VMEM((IDX_HEADS // 2 * TQ, LANE), jnp.bfloat16),
            pltpu.VMEM((NKB, TQ, KB), jnp.int32),
            pltpu.VMEM((NKB, KB, TQ), jnp.int32),
            pltpu.VMEM((TQ, KB), jnp.int32),
            pltpu.VMEM((NKB, TQ, KB), jnp.float32),
            pltpu.VMEM((A_HEADS, TQ, HALF), jnp.float32),
            pltpu.VMEM((A_HEADS, TQ, A_HEAD_DIM), jnp.float32),
            pltpu.VMEM((A_HEADS, TQ, A_HEAD_DIM), jnp.float32),
        ],
        compiler_params=pltpu.CompilerParams(
            dimension_semantics=("arbitrary", "arbitrary"), vmem_limit_bytes=VMEM_LIMIT),
        name="dsa_attention",
    )(proj_b, proj_b, proj_b, proj_f, proj_f, proj_f, proj_f, band)


GLA_CT = 512
GLA_C = 64


def _gla_kernel(bq_ref, bk_ref, bv_ref, bg_ref, ba_ref, wup_ref, balpha_ref, g_ref, o_ref, st_scr):
    @pl.when(pl.program_id(1) == 0)
    def _():
        st_scr[...] = jnp.zeros_like(st_scr)

    rr = lax.broadcasted_iota(jnp.int32, (GLA_C, GLA_C), 0)
    cc = lax.broadcasted_iota(jnp.int32, (GLA_C, GLA_C), 1)
    tri = rr >= cc
    tri_bf = tri.astype(jnp.bfloat16)
    wup = wup_ref[...].astype(jnp.bfloat16)
    balpha = balpha_ref[...]
    gain = g_ref[...]

    def chunk(c, carry):
        r0 = pl.multiple_of(c * GLA_C, GLA_C)
        rows = pl.ds(r0, GLA_C)
        ba = ba_ref[0, rows, :][:, :GATE_RANK].astype(jnp.bfloat16)
        pre = jnp.dot(ba, wup, preferred_element_type=jnp.float32) + balpha
        log_a = (jnp.minimum(pre, 0.0) - jnp.log1p(jnp.exp(-jnp.abs(pre)))) * (1.0 / GATE_TEMP)
        la_hi = log_a.astype(jnp.bfloat16)
        la_lo = (log_a - la_hi.astype(jnp.float32)).astype(jnp.bfloat16)
        bcum = (jnp.dot(tri_bf, la_hi, preferred_element_type=jnp.float32)
                + jnp.dot(tri_bf, la_lo, preferred_element_type=jnp.float32))
        for h in range(B_HEADS):
            ks = slice(h * B_DK, (h + 1) * B_DK)
            vs = slice(h * B_DV, (h + 1) * B_DV)
            b = bcum[:, ks]
            b_last = b[GLA_C - 1:GLA_C, :]
            q = bq_ref[0, rows, ks] * (B_DK ** -0.5)
            k = bk_ref[0, rows, ks]
            v = bv_ref[0, rows, vs].astype(jnp.bfloat16)
            qe = (q * jnp.exp(b)).astype(jnp.bfloat16)
            ke = (k * jnp.exp(-b)).astype(jnp.bfloat16)
            kd = (k * jnp.exp(b_last - b)).astype(jnp.bfloat16)
            attn = jnp.where(tri, _dot_nt(qe, ke), 0.0)
            st = st_scr[h]
            o = _dot_nt(qe, st.astype(jnp.bfloat16)) + jnp.dot(
                attn.astype(jnp.bfloat16), v, preferred_element_type=jnp.float32)
            st_scr[h] = st * jnp.exp(b_last) + _dot_tn(v, kd)
            on = o * lax.rsqrt(jnp.mean(o * o, axis=-1, keepdims=True) + EPS) * gain
            g = bg_ref[0, rows, vs]
            o_ref[0, rows, vs] = (on * (g * jax.nn.sigmoid(g))).astype(o_ref.dtype)
        return carry

    lax.fori_loop(0, GLA_CT // GLA_C, chunk, 0)


def _gla(proj_f, w_alpha_up, b_alpha, gla_g):
    return pl.pallas_call(
        _gla_kernel,
        grid=(BATCH, SEQ // GLA_CT),
        in_specs=[
            pl.BlockSpec((1, GLA_CT, B_KEY_WIDTH), lambda b, t: (b, t, F32_OFF["bq"] // B_KEY_WIDTH)),
            pl.BlockSpec((1, GLA_CT, B_KEY_WIDTH), lambda b, t: (b, t, F32_OFF["bk"] // B_KEY_WIDTH)),
            pl.BlockSpec((1, GLA_CT, B_WIDTH), lambda b, t: (b, t, F32_OFF["bv"] // B_WIDTH)),
            pl.BlockSpec((1, GLA_CT, B_WIDTH), lambda b, t: (b, t, F32_OFF["bg"] // B_WIDTH)),
            pl.BlockSpec((1, GLA_CT, LANE), lambda b, t: (b, t, F32_OFF["ba"] // LANE)),
            pl.BlockSpec((GATE_RANK, B_KEY_WIDTH), lambda b, t: (0, 0)),
            pl.BlockSpec((1, B_KEY_WIDTH), lambda b, t: (0, 0)),
            pl.BlockSpec((1, B_DV), lambda b, t: (0, 0)),
        ],
        out_specs=pl.BlockSpec((1, GLA_CT, B_WIDTH), lambda b, t: (b, t, 0)),
        out_shape=jax.ShapeDtypeStruct((BATCH, SEQ, B_WIDTH), jnp.bfloat16),
        scratch_shapes=[pltpu.VMEM((B_HEADS, B_DV, B_DK), jnp.float32)],
        compiler_params=pltpu.CompilerParams(
            dimension_semantics=("arbitrary", "arbitrary"), vmem_limit_bytes=VMEM_LIMIT),
        name="gla",
    )(proj_f, proj_f, proj_f, proj_f, proj_f, w_alpha_up, b_alpha.reshape(1, B_KEY_WIDTH),
      gla_g.reshape(1, B_DV))


OUT_TM = 512


def _out_kernel(a_ref, b_ref, wa_ref, wb_ref, x_ref, gate_ref, fg_ref, o_ref, *, final_norm):
    y = (jnp.dot(a_ref[...], wa_ref[...], preferred_element_type=jnp.float32)
         + jnp.dot(b_ref[...], wb_ref[...], preferred_element_type=jnp.float32))
    xn = x_ref[...] + gate_ref[0] * y
    if final_norm:
        r = xn * lax.rsqrt(jnp.mean(xn * xn, axis=-1, keepdims=True) + EPS)
        xn = r * fg_ref[...]
    o_ref[...] = xn


def _out_proj(a_out, b_out, w_out_bf, x2d, gate, final_g, layer, final_norm):
    m = x2d.shape[0]
    tiles_per_batch = SEQ // OUT_TM
    return pl.pallas_call(
        functools.partial(_out_kernel, final_norm=final_norm),
        grid=(m // OUT_TM,),
        in_specs=[
            pl.BlockSpec((OUT_TM, A_WIDTH), lambda i: (i, 0)),
            pl.BlockSpec((OUT_TM, B_WIDTH), lambda i: (i, 0)),
            pl.BlockSpec((None, A_WIDTH, D_MODEL), lambda i: (layer, 0, 0)),
            pl.BlockSpec((None, B_WIDTH, D_MODEL), lambda i: (layer, 1, 0)),
            pl.BlockSpec((OUT_TM, D_MODEL), lambda i: (i, 0)),
            pl.BlockSpec((1, 1, D_MODEL), lambda i: (i // tiles_per_batch, 0, 0)),
            pl.BlockSpec((1, D_MODEL), lambda i: (0, 0)),
        ],
        out_specs=pl.BlockSpec((OUT_TM, D_MODEL), lambda i: (i, 0)),
        out_shape=jax.ShapeDtypeStruct((m, D_MODEL), jnp.float32),
        compiler_params=pltpu.CompilerParams(
            dimension_semantics=("arbitrary",), vmem_limit_bytes=VMEM_LIMIT),
        name="out_proj",
    )(a_out, b_out, w_out_bf, w_out_bf, x2d, gate, final_g.reshape(1, D_MODEL))


def kernel(x, c, w_ada, b_ada, norm_g, w_in, w_alpha_up, b_alpha, gla_g, w_out, rel_bias, final_g):
    mod = _adaln_mod(c, w_ada, b_ada)
    band = _bias_band(rel_bias)
    w_in_p = _pack_w_in(w_in)
    w_out_bf = w_out.astype(jnp.bfloat16)
    x2d = x.reshape(BATCH * SEQ, D_MODEL)
    for l in range(DEPTH):
        shift = mod[l, :, 0:D_MODEL].reshape(BATCH, 1, D_MODEL)
        scale = mod[l, :, D_MODEL:2 * D_MODEL].reshape(BATCH, 1, D_MODEL)
        gate = mod[l, :, 2 * D_MODEL:].reshape(BATCH, 1, D_MODEL)
        proj_b, proj_f = _norm_proj(x2d, shift, scale, norm_g[l], w_in_p, l)
        proj_b = proj_b.reshape(BATCH, SEQ, NB_COLS)
        proj_f = proj_f.reshape(BATCH, SEQ, NF_COLS)
        a_out = _dsa_attention(proj_b, proj_f, band)
        b_out = _gla(proj_f, w_alpha_up[l], b_alpha[l], gla_g[l])
        x2d = _out_proj(a_out.reshape(BATCH * SEQ, A_WIDTH), b_out.reshape(BATCH * SEQ, B_WIDTH),
                        w_out_bf, x2d, gate, final_g, l, final_norm=(l == DEPTH - 1))
    return x2d.reshape(BATCH, SEQ, D_MODEL)
```

```python
import functools
import math

import numpy as np
import jax
import jax.numpy as jnp
from jax import lax
from jax.experimental import pallas as pl
from jax.experimental.pallas import tpu as pltpu

D_MODEL = 2048
BATCH = 4
SEQ = 2048
DEPTH = 4
A_WIDTH = 1024
A_HEADS = 8
A_HEAD_DIM = 128
IDX_HEADS = 16
IDX_DIM = 64
TOPK = min(256, SEQ // 4)
B_WIDTH = 1024
B_HEADS = 4
B_KEY_WIDTH = 512
B_DK = 128
B_DV = 256
GATE_RANK = 16
GATE_TEMP = 16.0
NUM_BUCKETS = 32
MAX_DISTANCE = 128
EPS = 1e-6

IN_WIDTHS = (A_WIDTH, A_WIDTH, A_WIDTH, A_WIDTH, IDX_HEADS * IDX_DIM, IDX_DIM, IDX_HEADS,
             B_KEY_WIDTH, B_KEY_WIDTH, B_WIDTH, B_WIDTH, GATE_RANK)
IN_NAMES = ("aq", "ak", "av", "ag", "iq", "ik", "iw", "bq", "bk", "bv", "bg", "ba")
IN_OFFSETS = dict(zip(IN_NAMES, np.concatenate([[0], np.cumsum(IN_WIDTHS)[:-1]]).tolist()))
IN_WIDTH_OF = dict(zip(IN_NAMES, IN_WIDTHS))
IN_COLS = sum(IN_WIDTHS)

LANE = 128
VMEM_LIMIT = 52 * 1024 * 1024

PROJ_TN = 512
BF_SEGS = (("aq", 1024), ("ak", 1024), ("av", 1024))
F32_SEGS = (("ag", 1024), ("iq", 1024), ("bv", 1024), ("bg", 1024), ("bq", 512), ("bk", 512),
            ("ik", LANE), ("iw", LANE), ("ba", LANE))
NB_COLS = sum(w for _, w in BF_SEGS)
NF_USED = sum(w for _, w in F32_SEGS)
NF_COLS = -(-NF_USED // PROJ_TN) * PROJ_TN
NP_COLS = NB_COLS + NF_COLS


def _seg_offsets(segs):
    offs, o = {}, 0
    for name, w in segs:
        offs[name] = o
        o += w
    return offs


BF_OFF = _seg_offsets(BF_SEGS)
F32_OFF = _seg_offsets(F32_SEGS)

INT_MIN = -2 ** 31
NEG_BIG = -1e30
LOG2E = math.log2(math.e)


PACK_ROWS = 256


def _pack_kernel(w_ref, o_ref):
    dst = 0
    for name, width in BF_SEGS + F32_SEGS:
        src, used = IN_OFFSETS[name], IN_WIDTH_OF[name]
        o_ref[0, :, dst:dst + used] = w_ref[0, :, src:src + used].astype(jnp.bfloat16)
        if width > used:
            o_ref[0, :, dst + used:dst + width] = jnp.zeros((PACK_ROWS, width - used), jnp.bfloat16)
        dst += width
    if dst < NP_COLS:
        o_ref[0, :, dst:] = jnp.zeros((PACK_ROWS, NP_COLS - dst), jnp.bfloat16)


def _pack_w_in(w_in):
    return pl.pallas_call(
        _pack_kernel,
        grid=(DEPTH, D_MODEL // PACK_ROWS),
        in_specs=[pl.BlockSpec((1, PACK_ROWS, IN_COLS), lambda l, r: (l, r, 0))],
        out_specs=pl.BlockSpec((1, PACK_ROWS, NP_COLS), lambda l, r: (l, r, 0)),
        out_shape=jax.ShapeDtypeStruct((DEPTH, D_MODEL, NP_COLS), jnp.bfloat16),
        compiler_params=pltpu.CompilerParams(
            dimension_semantics=("arbitrary", "arbitrary"), vmem_limit_bytes=VMEM_LIMIT),
        name="pack_w_in",
    )(w_in)


MOD_TN = 768


def _mod_kernel(c_ref, w_ref, b_ref, o_ref):
    c = c_ref[...]
    c_act = c * jax.nn.sigmoid(c)
    acc = jnp.dot(c_act.astype(jnp.bfloat16), w_ref[0].astype(jnp.bfloat16),
                  preferred_element_type=jnp.float32)
    o_ref[0] = acc + b_ref[0]


def _adaln_mod(c, w_ada, b_ada):
    cp = jnp.pad(c, ((0, 8 - BATCH), (0, 0)))
    out = pl.pallas_call(
        _mod_kernel,
        grid=(DEPTH, 3 * D_MODEL // MOD_TN),
        in_specs=[
            pl.BlockSpec((8, D_MODEL), lambda l, j: (0, 0)),
            pl.BlockSpec((1, D_MODEL, MOD_TN), lambda l, j: (l, 0, j)),
            pl.BlockSpec((1, 1, MOD_TN), lambda l, j: (l, 0, j)),
        ],
        out_specs=pl.BlockSpec((1, 8, MOD_TN), lambda l, j: (l, 0, j)),
        out_shape=jax.ShapeDtypeStruct((DEPTH, 8, 3 * D_MODEL), jnp.float32),
        compiler_params=pltpu.CompilerParams(
            dimension_semantics=("arbitrary", "arbitrary"), vmem_limit_bytes=VMEM_LIMIT),
        name="adaln_mod",
    )(cp, w_ada, b_ada.reshape(DEPTH, 1, 3 * D_MODEL))
    return out[:, :BATCH]


PROJ_TM = 1024
NB_TILES = NB_COLS // PROJ_TN


def _proj_kernel(x_ref, shift_ref, scale_ref, g_ref, w_ref, ob_ref, of_ref, h_scr):
    j = pl.program_id(1)

    @pl.when(j == 0)
    def _():
        x = x_ref[...]
        r = x * lax.rsqrt(jnp.mean(x * x, axis=-1, keepdims=True) + EPS)
        h = (r * g_ref[...]) * (1.0 + scale_ref[0]) + shift_ref[0]
        h_scr[...] = h.astype(jnp.bfloat16)

    @pl.when(j < NB_TILES)
    def _():
        ob_ref[...] = jnp.dot(h_scr[...], w_ref[...],
                              preferred_element_type=jnp.float32).astype(jnp.bfloat16)

    @pl.when(j >= NB_TILES)
    def _():
        of_ref[...] = jnp.dot(h_scr[...], w_ref[...], preferred_element_type=jnp.float32)


def _norm_proj(x2d, shift, scale, norm_g, w_p, layer):
    m = x2d.shape[0]
    tiles_per_batch = SEQ // PROJ_TM
    return pl.pallas_call(
        _proj_kernel,
        grid=(m // PROJ_TM, NP_COLS // PROJ_TN),
        in_specs=[
            pl.BlockSpec((PROJ_TM, D_MODEL), lambda i, j: (i, 0)),
            pl.BlockSpec((1, 1, D_MODEL), lambda i, j: (i // tiles_per_batch, 0, 0)),
            pl.BlockSpec((1, 1, D_MODEL), lambda i, j: (i // tiles_per_batch, 0, 0)),
            pl.BlockSpec((1, D_MODEL), lambda i, j: (0, 0)),
            pl.BlockSpec((None, D_MODEL, PROJ_TN), lambda i, j: (layer, 0, j)),
        ],
        out_specs=[
            pl.BlockSpec((PROJ_TM, PROJ_TN), lambda i, j: (i, jnp.minimum(j, NB_TILES - 1))),
            pl.BlockSpec((PROJ_TM, PROJ_TN), lambda i, j: (i, jnp.maximum(j - NB_TILES, 0))),
        ],
        out_shape=[
            jax.ShapeDtypeStruct((m, NB_COLS), jnp.bfloat16),
            jax.ShapeDtypeStruct((m, NF_COLS), jnp.float32),
        ],
        scratch_shapes=[pltpu.VMEM((PROJ_TM, D_MODEL), jnp.bfloat16)],
        compiler_params=pltpu.CompilerParams(
            dimension_semantics=("arbitrary", "arbitrary"), vmem_limit_bytes=VMEM_LIMIT),
        name="norm_proj",
    )(x2d, shift, scale, norm_g.reshape(1, D_MODEL), w_p)


TQ = 256
KB = 256
NQ = SEQ // TQ
NKB = SEQ // KB
HALF = KB // 2


def _t5_bucket_np(d):
    max_exact = NUM_BUCKETS // 2
    d = np.maximum(d, 0)
    df = np.maximum(d, 1).astype(np.float32)
    large = max_exact + (np.log(df / np.float32(max_exact)) / np.float32(math.log(MAX_DISTANCE / max_exact))
                         * np.float32(NUM_BUCKETS - max_exact)).astype(np.int32)
    large = np.minimum(large, NUM_BUCKETS - 1)
    return np.where(d < max_exact, d, large).astype(np.int32)


assert int(_t5_bucket_np(np.arange(KB, 2 * SEQ)).min()) == NUM_BUCKETS - 1


def _band_buckets():
    tl = np.arange(TQ)[:, None]
    u = np.arange(KB)[None, :]
    prev = _t5_bucket_np(KB + tl - u)
    diag = _t5_bucket_np(tl - u)
    return np.stack([prev, diag]).astype(np.int32)


BAND_ROWS = 64


def _band_kernel(rb_ref, bucket_ref, o_ref):
    h = pl.program_id(0)
    far = rb_ref[NUM_BUCKETS - 1, h]

    def rows(r, carry):
        rs = pl.ds(pl.multiple_of(r * BAND_ROWS, BAND_ROWS), BAND_ROWS)
        bucket = bucket_ref[0, rs, :]
        acc = jnp.zeros(bucket.shape, jnp.float32)
        for b in range(NUM_BUCKETS):
            acc = jnp.where(bucket == b, rb_ref[b, h] - far, acc)
        o_ref[0, 0, rs, :] = acc * LOG2E
        return carry

    lax.fori_loop(0, TQ // BAND_ROWS, rows, 0)


def _bias_band(rel_bias):
    return pl.pallas_call(
        _band_kernel,
        grid=(A_HEADS, 2),
        in_specs=[
            pl.BlockSpec(memory_space=pltpu.SMEM),
            pl.BlockSpec((1, TQ, KB), lambda h, s: (s, 0, 0)),
        ],
        out_specs=pl.BlockSpec((1, 1, TQ, KB), lambda h, s: (h, s, 0, 0)),
        out_shape=jax.ShapeDtypeStruct((A_HEADS, 2, TQ, KB), jnp.float32),
        name="bias_band",
    )(rel_bias, jnp.asarray(_band_buckets()))


def _dot_nt(a, b):
    return lax.dot_general(a, b, (((1,), (1,)), ((), ())), preferred_element_type=jnp.float32)


def _dot_tn(a, b):
    return lax.dot_general(a, b, (((0,), (0,)), ((), ())), preferred_element_type=jnp.float32)


def _dsa_kernel(q_ref, k_ref, v_ref, iq_ref, ik_ref, iw_ref, ag_ref, band_ref, o_ref,
                iklo_scr, ikhi_scr, vext_scr, wb_scr, iq2_scr, key_scr, keyt_scr, thrn_scr,
                madd_scr, m_scr, accv_scr, accl_scr):
    qi = pl.program_id(1)
    nkb = qi + 1
    bf16 = jnp.bfloat16

    def key_rows(kb):
        return pl.ds(pl.multiple_of(kb * KB, KB), KB)

    @pl.when(qi == 0)
    def _():
        def prep(c, carry):
            rs = key_rows(c)
            ik = ik_ref[0, rs, :]
            iklo_scr[rs, :] = ik.astype(bf16)
            ikhi_scr[rs, :] = pltpu.roll(ik, IDX_DIM, axis=1).astype(bf16)
            for h in range(A_HEADS):
                vext_scr[h, rs, :A_HEAD_DIM] = v_ref[0, rs, h * A_HEAD_DIM:(h + 1) * A_HEAD_DIM]
                vext_scr[h, rs, A_HEAD_DIM:] = jnp.ones((KB, A_HEAD_DIM), bf16)
            return carry

        lax.fori_loop(0, NKB, prep, 0)

    iw = iw_ref[0][:, :IDX_HEADS] * (IDX_HEADS ** -0.5 * IDX_DIM ** -0.5)
    for j in range(IDX_HEADS):
        wb_scr[j] = jnp.broadcast_to(iw[:, j:j + 1], (TQ, HALF))
    for jp in range(IDX_HEADS // 2):
        iq2_scr[jp * TQ:(jp + 1) * TQ, :] = iq_ref[0, :, jp * LANE:(jp + 1) * LANE].astype(bf16)

    row = lax.broadcasted_iota(jnp.int32, (TQ, HALF), 0)
    col = lax.broadcasted_iota(jnp.int32, (TQ, HALF), 1)

    def score_block(kb, carry):
        rs = key_rows(kb)
        s_lo = _dot_nt(iq2_scr[...], iklo_scr[rs, :])
        s_hi = _dot_nt(iq2_scr[...], ikhi_scr[rs, :])
        for half in range(2):
            ls = slice(half * HALF, (half + 1) * HALF)
            acc = jnp.zeros((TQ, HALF), jnp.float32)
            for jp in range(IDX_HEADS // 2):
                rj = slice(jp * TQ, (jp + 1) * TQ)
                acc = (acc + jnp.maximum(s_lo[rj, ls], 0.0) * wb_scr[2 * jp]
                       + jnp.maximum(s_hi[rj, ls], 0.0) * wb_scr[2 * jp + 1])
            bits = pltpu.bitcast(acc, jnp.int32)
            key = jnp.where(bits >= 0, bits, bits ^ 0x7FFFFFFF)
            causal = (kb * KB + half * HALF + col) <= (qi * TQ + row)
            key_scr[kb, :, ls] = jnp.where(causal, key, INT_MIN)
        keyt_scr[kb] = key_scr[kb].T
        return carry

    lax.fori_loop(0, nkb, score_block, 0)

    def count_ge(cand):
        def body(kb, cnt):
            hit = (keyt_scr[kb] >= cand).astype(jnp.int32)
            return cnt + jnp.sum(hit.reshape(KB // 8, 8, TQ), axis=0)
        cnt = lax.fori_loop(0, nkb, body, jnp.zeros((8, TQ), jnp.int32))
        return jnp.sum(cnt, axis=0, keepdims=True)

    zero = jnp.zeros((1, TQ), jnp.int32)
    prefix = jnp.where(count_ge(zero) >= TOPK, zero, INT_MIN)

    def bit_pass(it, prefix):
        cand = prefix + lax.shift_left(jnp.int32(1), 30 - it)
        return jnp.where(count_ge(cand) >= TOPK, cand, prefix)

    prefix = lax.fori_loop(0, 31, bit_pass, prefix)
    thr = jnp.maximum(prefix, INT_MIN + 1)
    thrn_scr[...] = jnp.broadcast_to(thr, (KB, TQ)).T

    def madd_block(kb, carry):
        madd_scr[kb] = jnp.where(key_scr[kb] >= thrn_scr[...], 0.0, NEG_BIG)
        return carry

    lax.fori_loop(0, nkb, madd_block, 0)

    m_scr[...] = jnp.full(m_scr.shape, NEG_BIG, jnp.float32)
    accv_scr[...] = jnp.zeros_like(accv_scr)
    accl_scr[...] = jnp.zeros_like(accl_scr)
    scale2 = A_HEAD_DIM ** -0.5 * LOG2E

    def attend(kb, slot):
        rs = key_rows(kb)
        for h in range(A_HEADS):
            hs = slice(h * A_HEAD_DIM, (h + 1) * A_HEAD_DIM)
            lg = _dot_nt(q_ref[0, :, hs], k_ref[0, rs, hs]) * scale2 + madd_scr[kb]
            if slot is not None:
                lg = lg + band_ref[h, slot]
            m_old = m_scr[h]
            m_new = jnp.maximum(m_old, jnp.max(lg, axis=-1, keepdims=True))
            alpha = jnp.exp2(m_old - m_new)
            p = jnp.concatenate([jnp.exp2(lg[:, :HALF] - m_new), jnp.exp2(lg[:, HALF:] - m_new)], axis=1)
            pv = jnp.dot(p.astype(bf16), vext_scr[h, rs, :], preferred_element_type=jnp.float32)
            accv_scr[h] = accv_scr[h] * alpha + pv[:, :A_HEAD_DIM]
            accl_scr[h] = accl_scr[h] * alpha + pv[:, A_HEAD_DIM:]
            m_scr[h] = m_new

    def far_block(kb, carry):
        attend(kb, None)
        return carry

    lax.fori_loop(0, qi - 1, far_block, 0)

    @pl.when(qi >= 1)
    def _():
        attend(qi - 1, 0)

    attend(qi, 1)

    for h in range(A_HEADS):
        hs = slice(h * A_HEAD_DIM, (h + 1) * A_HEAD_DIM)
        g = ag_ref[0, :, hs]
        o_ref[0, :, hs] = (accv_scr[h] / accl_scr[h] * (g * jax.nn.sigmoid(g))).astype(o_ref.dtype)


def _dsa_attention(proj_b, proj_f, band):
    cb = lambda name: BF_OFF[name] // A_WIDTH
    once = pl.Buffered(1)
    return pl.pallas_call(
        _dsa_kernel,
        grid=(BATCH, NQ),
        in_specs=[
            pl.BlockSpec((1, TQ, A_WIDTH), lambda b, i: (b, i, cb("aq"))),
            pl.BlockSpec((1, SEQ, A_WIDTH), lambda b, i: (b, 0, cb("ak")), pipeline_mode=once),
            pl.BlockSpec((1, SEQ, A_WIDTH), lambda b, i: (b, 0, cb("av")), pipeline_mode=once),
            pl.BlockSpec((1, TQ, A_WIDTH), lambda b, i: (b, i, F32_OFF["iq"] // A_WIDTH)),
            pl.BlockSpec((1, SEQ, LANE), lambda b, i: (b, 0, F32_OFF["ik"] // LANE), pipeline_mode=once),
            pl.BlockSpec((1, TQ, LANE), lambda b, i: (b, i, F32_OFF["iw"] // LANE)),
            pl.BlockSpec((1, TQ, A_WIDTH), lambda b, i: (b, i, F32_OFF["ag"] // A_WIDTH)),
            pl.BlockSpec((A_HEADS, 2, TQ, KB), lambda b, i: (0, 0, 0, 0), pipeline_mode=once),
        ],
        out_specs=pl.BlockSpec((1, TQ, A_WIDTH), lambda b, i: (b, i, 0)),
        out_shape=jax.ShapeDtypeStruct((BATCH, SEQ, A_WIDTH), jnp.bfloat16),
        scratch_shapes=[
            pltpu.VMEM((SEQ, LANE), jnp.bfloat16),
            pltpu.VMEM((SEQ, LANE), jnp.bfloat16),
            pltpu.VMEM((A_HEADS, SEQ, 2 * A_HEAD_DIM), jnp.bfloat16),
            pltpu.VMEM((IDX_HEADS, TQ, HALF), jnp.float32),
            pltpu.VMEM((IDX_HEADS // 2 * TQ, LANE), jnp.bfloat16),
            pltpu.VMEM((NKB, TQ, KB), jnp.int32),
            pltpu.VMEM((NKB, KB, TQ), jnp.int32),
            pltpu.VMEM((TQ, KB), jnp.int32),
            pltpu.VMEM((NKB, TQ, KB), jnp.float32),
            pltpu.VMEM((A_HEADS, TQ, HALF), jnp.float32),
            pltpu.VMEM((A_HEADS, TQ, A_HEAD_DIM), jnp.float32),
            pltpu.VMEM((A_HEADS, TQ, A_HEAD_DIM), jnp.float32),
        ],
        compiler_params=pltpu.CompilerParams(
            dimension_semantics=("arbitrary", "arbitrary"), vmem_limit_bytes=VMEM_LIMIT),
        name="dsa_attention",
    )(proj_b, proj_b, proj_b, proj_f, proj_f, proj_f, proj_f, band)


GLA_CT = 512
GLA_C = 64


def _gla_kernel(bq_ref, bk_ref, bv_ref, bg_ref, ba_ref, wup_ref, balpha_ref, g_ref, o_ref, st_scr):
    @pl.when(pl.program_id(1) == 0)
    def _():
        st_scr[...] = jnp.zeros_like(st_scr)

    rr = lax.broadcasted_iota(jnp.int32, (GLA_C, GLA_C), 0)
    cc = lax.broadcasted_iota(jnp.int32, (GLA_C, GLA_C), 1)
    tri = rr >= cc
    tri_bf = tri.astype(jnp.bfloat16)
    wup = wup_ref[...].astype(jnp.bfloat16)
    balpha = balpha_ref[...]
    gain = g_ref[...]

    def chunk(c, carry):
        r0 = pl.multiple_of(c * GLA_C, GLA_C)
        rows = pl.ds(r0, GLA_C)
        ba = ba_ref[0, rows, :][:, :GATE_RANK].astype(jnp.bfloat16)
        pre = jnp.dot(ba, wup, preferred_element_type=jnp.float32) + balpha
        log_a = (jnp.minimum(pre, 0.0) - jnp.log1p(jnp.exp(-jnp.abs(pre)))) * (1.0 / GATE_TEMP)
        la_hi = log_a.astype(jnp.bfloat16)
        la_lo = (log_a - la_hi.astype(jnp.float32)).astype(jnp.bfloat16)
        bcum = (jnp.dot(tri_bf, la_hi, preferred_element_type=jnp.float32)
                + jnp.dot(tri_bf, la_lo, preferred_element_type=jnp.float32))
        for h in range(B_HEADS):
            ks = slice(h * B_DK, (h + 1) * B_DK)
            vs = slice(h * B_DV, (h + 1) * B_DV)
            b = bcum[:, ks]
            b_last = b[GLA_C - 1:GLA_C, :]
            q = bq_ref[0, rows, ks] * (B_DK ** -0.5)
            k = bk_ref[0, rows, ks]
            v = bv_ref[0, rows, vs].astype(jnp.bfloat16)
            qe = (q * jnp.exp(b)).astype(jnp.bfloat16)
            ke = (k * jnp.exp(-b)).astype(jnp.bfloat16)
            kd = (k * jnp.exp(b_last - b)).astype(jnp.bfloat16)
            attn = jnp.where(tri, _dot_nt(qe, ke), 0.0)
            st = st_scr[h]
            o = _dot_nt(qe, st.astype(jnp.bfloat16)) + jnp.dot(
                attn.astype(jnp.bfloat16), v, preferred_element_type=jnp.float32)
            st_scr[h] = st * jnp.exp(b_last) + _dot_tn(v, kd)
            on = o * lax.rsqrt(jnp.mean(o * o, axis=-1, keepdims=True) + EPS) * gain
            g = bg_ref[0, rows, vs]
            o_ref[0, rows, vs] = (on * (g * jax.nn.sigmoid(g))).astype(o_ref.dtype)
        return carry

    lax.fori_loop(0, GLA_CT // GLA_C, chunk, 0)


def _gla(proj_f, w_alpha_up, b_alpha, gla_g):
    return pl.pallas_call(
        _gla_kernel,
        grid=(BATCH, SEQ // GLA_CT),
        in_specs=[
            pl.BlockSpec((1, GLA_CT, B_KEY_WIDTH), lambda b, t: (b, t, F32_OFF["bq"] // B_KEY_WIDTH)),
            pl.BlockSpec((1, GLA_CT, B_KEY_WIDTH), lambda b, t: (b, t, F32_OFF["bk"] // B_KEY_WIDTH)),
            pl.BlockSpec((1, GLA_CT, B_WIDTH), lambda b, t: (b, t, F32_OFF["bv"] // B_WIDTH)),
            pl.BlockSpec((1, GLA_CT, B_WIDTH), lambda b, t: (b, t, F32_OFF["bg"] // B_WIDTH)),
            pl.BlockSpec((1, GLA_CT, LANE), lambda b, t: (b, t, F32_OFF["ba"] // LANE)),
            pl.BlockSpec((GATE_RANK, B_KEY_WIDTH), lambda b, t: (0, 0)),
            pl.BlockSpec((1, B_KEY_WIDTH), lambda b, t: (0, 0)),
            pl.BlockSpec((1, B_DV), lambda b, t: (0, 0)),
        ],
        out_specs=pl.BlockSpec((1, GLA_CT, B_WIDTH), lambda b, t: (b, t, 0)),
        out_shape=jax.ShapeDtypeStruct((BATCH, SEQ, B_WIDTH), jnp.bfloat16),
        scratch_shapes=[pltpu.VMEM((B_HEADS, B_DV, B_DK), jnp.float32)],
        compiler_params=pltpu.CompilerParams(
            dimension_semantics=("arbitrary", "arbitrary"), vmem_limit_bytes=VMEM_LIMIT),
        name="gla",
    )(proj_f, proj_f, proj_f, proj_f, proj_f, w_alpha_up, b_alpha.reshape(1, B_KEY_WIDTH),
      gla_g.reshape(1, B_DV))


OUT_TM = 512


def _out_kernel(a_ref, b_ref, wa_ref, wb_ref, x_ref, gate_ref, fg_ref, o_ref, *, final_norm):
    y = (jnp.dot(a_ref[...], wa_ref[...], preferred_element_type=jnp.float32)
         + jnp.dot(b_ref[...], wb_ref[...], preferred_element_type=jnp.float32))
    xn = x_ref[...] + gate_ref[0] * y
    if final_norm:
        r = xn * lax.rsqrt(jnp.mean(xn * xn, axis=-1, keepdims=True) + EPS)
        xn = r * fg_ref[...]
    o_ref[...] = xn


def _out_proj(a_out, b_out, w_out_bf, x2d, gate, final_g, layer, final_norm):
    m = x2d.shape[0]
    tiles_per_batch = SEQ // OUT_TM
    return pl.pallas_call(
        functools.partial(_out_kernel, final_norm=final_norm),
        grid=(m // OUT_TM,),
        in_specs=[
            pl.BlockSpec((OUT_TM, A_WIDTH), lambda i: (i, 0)),
            pl.BlockSpec((OUT_TM, B_WIDTH), lambda i: (i, 0)),
            pl.BlockSpec((None, A_WIDTH, D_MODEL), lambda i: (layer, 0, 0)),
            pl.BlockSpec((None, B_WIDTH, D_MODEL), lambda i: (layer, 1, 0)),
            pl.BlockSpec((OUT_TM, D_MODEL), lambda i: (i, 0)),
            pl.BlockSpec((1, 1, D_MODEL), lambda i: (i // tiles_per_batch, 0, 0)),
            pl.BlockSpec((1, D_MODEL), lambda i: (0, 0)),
        ],
        out_specs=pl.BlockSpec((OUT_TM, D_MODEL), lambda i: (i, 0)),
        out_shape=jax.ShapeDtypeStruct((m, D_MODEL), jnp.float32),
        compiler_params=pltpu.CompilerParams(
            dimension_semantics=("arbitrary",), vmem_limit_bytes=VMEM_LIMIT),
        name="out_proj",
    )(a_out, b_out, w_out_bf, w_out_bf, x2d, gate, final_g.reshape(1, D_MODEL))


def kernel(x, c, w_ada, b_ada, norm_g, w_in, w_alpha_up, b_alpha, gla_g, w_out, rel_bias, final_g):
    mod = _adaln_mod(c, w_ada, b_ada)
    band = _bias_band(rel_bias)
    w_in_p = _pack_w_in(w_in)
    w_out_bf = w_out.astype(jnp.bfloat16)
    x2d = x.reshape(BATCH * SEQ, D_MODEL)
    for l in range(DEPTH):
        shift = mod[l, :, 0:D_MODEL].reshape(BATCH, 1, D_MODEL)
        scale = mod[l, :, D_MODEL:2 * D_MODEL].reshape(BATCH, 1, D_MODEL)
        gate = mod[l, :, 2 * D_MODEL:].reshape(BATCH, 1, D_MODEL)
        proj_b, proj_f = _norm_proj(x2d, shift, scale, norm_g[l], w_in_p, l)
        proj_b = proj_b.reshape(BATCH, SEQ, NB_COLS)
        proj_f = proj_f.reshape(BATCH, SEQ, NF_COLS)
        a_out = _dsa_attention(proj_b, proj_f, band)
        b_out = _gla(proj_f, w_alpha_up[l], b_alpha[l], gla_g[l])
        x2d = _out_proj(a_out.reshape(BATCH * SEQ, A_WIDTH), b_out.reshape(BATCH * SEQ, B_WIDTH),
                        w_out_bf, x2d, gate, final_g, l, final_norm=(l == DEPTH - 1))
    return x2d.reshape(BATCH, SEQ, D_MODEL)
```

```python
import functools
import math

import numpy as np
import jax
import jax.numpy as jnp
from jax import lax
from jax.experimental import pallas as pl
from jax.experimental.pallas import tpu as pltpu

D_MODEL = 2048
BATCH = 4
SEQ = 2048
DEPTH = 4
A_WIDTH = 1024
A_HEADS = 8
A_HEAD_DIM = 128
IDX_HEADS = 16
IDX_DIM = 64
TOPK = min(256, SEQ // 4)
B_WIDTH = 1024
B_HEADS = 4
B_KEY_WIDTH = 512
B_DK = 128
B_DV = 256
GATE_RANK = 16
GATE_TEMP = 16.0
NUM_BUCKETS = 32
MAX_DISTANCE = 128
EPS = 1e-6

IN_WIDTHS = (A_WIDTH, A_WIDTH, A_WIDTH, A_WIDTH, IDX_HEADS * IDX_DIM, IDX_DIM, IDX_HEADS,
             B_KEY_WIDTH, B_KEY_WIDTH, B_WIDTH, B_WIDTH, GATE_RANK)
IN_NAMES = ("aq", "ak", "av", "ag", "iq", "ik", "iw", "bq", "bk", "bv", "bg", "ba")
IN_OFFSETS = dict(zip(IN_NAMES, np.concatenate([[0], np.cumsum(IN_WIDTHS)[:-1]]).tolist()))
IN_WIDTH_OF = dict(zip(IN_NAMES, IN_WIDTHS))
IN_COLS = sum(IN_WIDTHS)

LANE = 128
VMEM_LIMIT = 52 * 1024 * 1024

PROJ_TN = 512
BF_SEGS = (("aq", 1024), ("ak", 1024), ("av", 1024))
F32_SEGS = (("ag", 1024), ("iq", 1024), ("bv", 1024), ("bg", 1024), ("bq", 512), ("bk", 512),
            ("ik", LANE), ("iw", LANE), ("ba", LANE))
NB_COLS = sum(w for _, w in BF_SEGS)
NF_USED = sum(w for _, w in F32_SEGS)
NF_COLS = -(-NF_USED // PROJ_TN) * PROJ_TN
NP_COLS = NB_COLS + NF_COLS


def _seg_offsets(segs):
    offs, o = {}, 0
    for name, w in segs:
        offs[name] = o
        o += w
    return offs


BF_OFF = _seg_offsets(BF_SEGS)
F32_OFF = _seg_offsets(F32_SEGS)

INT_MIN = -2 ** 31
NEG_BIG = -1e30
LOG2E = math.log2(math.e)


PACK_COLS = 256


def _pack_kernel(w_ref, o_ref):
    dst = 0
    for name, width in BF_SEGS + F32_SEGS:
        src, used = IN_OFFSETS[name], IN_WIDTH_OF[name]
        o_ref[0, dst:dst + used, :] = w_ref[0, src:src + used, :].astype(jnp.bfloat16)
        if width > used:
            o_ref[0, dst + used:dst + width, :] = jnp.zeros((width - used, PACK_COLS), jnp.bfloat16)
        dst += width
    if dst < NP_COLS:
        o_ref[0, dst:, :] = jnp.zeros((NP_COLS - dst, PACK_COLS), jnp.bfloat16)


def _pack_w_in(w_in_t):
    return pl.pallas_call(
        _pack_kernel,
        grid=(DEPTH, D_MODEL // PACK_COLS),
        in_specs=[pl.BlockSpec((1, IN_COLS, PACK_COLS), lambda l, c: (l, 0, c))],
        out_specs=pl.BlockSpec((1, NP_COLS, PACK_COLS), lambda l, c: (l, 0, c)),
        out_shape=jax.ShapeDtypeStruct((DEPTH, NP_COLS, D_MODEL), jnp.bfloat16),
        compiler_params=pltpu.CompilerParams(
            dimension_semantics=("arbitrary", "arbitrary"), vmem_limit_bytes=VMEM_LIMIT),
        name="pack_w_in",
    )(w_in_t)


MOD_TN = 768


def _mod_kernel(c_ref, w_ref, b_ref, o_ref):
    c = c_ref[...]
    c_act = c * jax.nn.sigmoid(c)
    acc = jnp.dot(c_act.astype(jnp.bfloat16), w_ref[0].astype(jnp.bfloat16),
                  preferred_element_type=jnp.float32)
    o_ref[0] = acc + b_ref[0]


def _adaln_mod(c, w_ada, b_ada):
    cp = jnp.pad(c, ((0, 8 - BATCH), (0, 0)))
    out = pl.pallas_call(
        _mod_kernel,
        grid=(DEPTH, 3 * D_MODEL // MOD_TN),
        in_specs=[
            pl.BlockSpec((8, D_MODEL), lambda l, j: (0, 0)),
            pl.BlockSpec((1, D_MODEL, MOD_TN), lambda l, j: (l, 0, j)),
            pl.BlockSpec((1, 1, MOD_TN), lambda l, j: (l, 0, j)),
        ],
        out_specs=pl.BlockSpec((1, 8, MOD_TN), lambda l, j: (l, 0, j)),
        out_shape=jax.ShapeDtypeStruct((DEPTH, 8, 3 * D_MODEL), jnp.float32),
        compiler_params=pltpu.CompilerParams(
            dimension_semantics=("arbitrary", "arbitrary"), vmem_limit_bytes=VMEM_LIMIT),
        name="adaln_mod",
    )(cp, w_ada, b_ada.reshape(DEPTH, 1, 3 * D_MODEL))
    return out[:, :BATCH]


PROJ_TM = 1024
NB_TILES = NB_COLS // PROJ_TN


def _proj_kernel(x_ref, shift_ref, scale_ref, g_ref, w_ref, ob_ref, of_ref, h_scr):
    j = pl.program_id(1)

    @pl.when(j == 0)
    def _():
        x = x_ref[...]
        r = x * lax.rsqrt(jnp.mean(x * x, axis=-1, keepdims=True) + EPS)
        h = (r * g_ref[...]) * (1.0 + scale_ref[0]) + shift_ref[0]
        h_scr[...] = h.astype(jnp.bfloat16)

    @pl.when(j < NB_TILES)
    def _():
        ob_ref[...] = _dot_nt(h_scr[...], w_ref[...]).astype(jnp.bfloat16)

    @pl.when(j >= NB_TILES)
    def _():
        of_ref[...] = _dot_nt(h_scr[...], w_ref[...])


def _norm_proj(x2d, shift, scale, norm_g, w_p, layer):
    m = x2d.shape[0]
    tiles_per_batch = SEQ // PROJ_TM
    return pl.pallas_call(
        _proj_kernel,
        grid=(m // PROJ_TM, NP_COLS // PROJ_TN),
        in_specs=[
            pl.BlockSpec((PROJ_TM, D_MODEL), lambda i, j: (i, 0)),
            pl.BlockSpec((1, 1, D_MODEL), lambda i, j: (i // tiles_per_batch, 0, 0)),
            pl.BlockSpec((1, 1, D_MODEL), lambda i, j: (i // tiles_per_batch, 0, 0)),
            pl.BlockSpec((1, D_MODEL), lambda i, j: (0, 0)),
            pl.BlockSpec((None, PROJ_TN, D_MODEL), lambda i, j: (layer, j, 0)),
        ],
        out_specs=[
            pl.BlockSpec((PROJ_TM, PROJ_TN), lambda i, j: (i, jnp.minimum(j, NB_TILES - 1))),
            pl.BlockSpec((PROJ_TM, PROJ_TN), lambda i, j: (i, jnp.maximum(j - NB_TILES, 0))),
        ],
        out_shape=[
            jax.ShapeDtypeStruct((m, NB_COLS), jnp.bfloat16),
            jax.ShapeDtypeStruct((m, NF_COLS), jnp.float32),
        ],
        scratch_shapes=[pltpu.VMEM((PROJ_TM, D_MODEL), jnp.bfloat16)],
        compiler_params=pltpu.CompilerParams(
            dimension_semantics=("arbitrary", "arbitrary"), vmem_limit_bytes=VMEM_LIMIT),
        name="norm_proj",
    )(x2d, shift, scale, norm_g.reshape(1, D_MODEL), w_p)


TQ = 256
KB = 256
NQ = SEQ // TQ
NKB = SEQ // KB
HALF = KB // 2


def _t5_bucket_np(d):
    max_exact = NUM_BUCKETS // 2
    d = np.maximum(d, 0)
    df = np.maximum(d, 1).astype(np.float32)
    large = max_exact + (np.log(df / np.float32(max_exact)) / np.float32(math.log(MAX_DISTANCE / max_exact))
                         * np.float32(NUM_BUCKETS - max_exact)).astype(np.int32)
    large = np.minimum(large, NUM_BUCKETS - 1)
    return np.where(d < max_exact, d, large).astype(np.int32)


assert int(_t5_bucket_np(np.arange(KB, 2 * SEQ)).min()) == NUM_BUCKETS - 1


def _band_buckets():
    tl = np.arange(TQ)[:, None]
    u = np.arange(KB)[None, :]
    prev = _t5_bucket_np(KB + tl - u)
    diag = _t5_bucket_np(tl - u)
    return np.stack([prev, diag]).astype(np.int32)


BAND_ROWS = 64


def _band_kernel(rb_ref, bucket_ref, o_ref):
    h = pl.program_id(0)
    far = rb_ref[NUM_BUCKETS - 1, h]

    def rows(r, carry):
        rs = pl.ds(pl.multiple_of(r * BAND_ROWS, BAND_ROWS), BAND_ROWS)
        bucket = bucket_ref[0, rs, :]
        acc = jnp.zeros(bucket.shape, jnp.float32)
        for b in range(NUM_BUCKETS):
            acc = jnp.where(bucket == b, rb_ref[b, h] - far, acc)
        o_ref[0, 0, rs, :] = acc * LOG2E
        return carry

    lax.fori_loop(0, TQ // BAND_ROWS, rows, 0)


def _bias_band(rel_bias):
    return pl.pallas_call(
        _band_kernel,
        grid=(A_HEADS, 2),
        in_specs=[
            pl.BlockSpec(memory_space=pltpu.SMEM),
            pl.BlockSpec((1, TQ, KB), lambda h, s: (s, 0, 0)),
        ],
        out_specs=pl.BlockSpec((1, 1, TQ, KB), lambda h, s: (h, s, 0, 0)),
        out_shape=jax.ShapeDtypeStruct((A_HEADS, 2, TQ, KB), jnp.float32),
        name="bias_band",
    )(rel_bias, jnp.asarray(_band_buckets()))


def _dot_nt(a, b):
    return lax.dot_general(a, b, (((1,), (1,)), ((), ())), preferred_element_type=jnp.float32)


def _dot_tn(a, b):
    return lax.dot_general(a, b, (((0,), (0,)), ((), ())), preferred_element_type=jnp.float32)


def _dsa_kernel(q_ref, k_ref, v_ref, iq_ref, ik_ref, iw_ref, ag_ref, band_ref, o_ref,
                iklo_scr, ikhi_scr, vext_scr, wb_scr, iq2_scr, key_scr, keyt_scr, thrn_scr,
                madd_scr, m_scr, accv_scr, accl_scr):
    qi = pl.program_id(1)
    nkb = qi + 1
    bf16 = jnp.bfloat16

    def key_rows(kb):
        return pl.ds(pl.multiple_of(kb * KB, KB), KB)

    @pl.when(qi == 0)
    def _():
        def prep(c, carry):
            rs = key_rows(c)
            ik = ik_ref[0, rs, :]
            iklo_scr[rs, :] = ik.astype(bf16)
            ikhi_scr[rs, :] = pltpu.roll(ik, IDX_DIM, axis=1).astype(bf16)
            for h in range(A_HEADS):
                vext_scr[h, rs, :A_HEAD_DIM] = v_ref[0, rs, h * A_HEAD_DIM:(h + 1) * A_HEAD_DIM]
                vext_scr[h, rs, A_HEAD_DIM:] = jnp.ones((KB, A_HEAD_DIM), bf16)
            return carry

        lax.fori_loop(0, NKB, prep, 0)

    iw = iw_ref[0][:, :IDX_HEADS] * (IDX_HEADS ** -0.5 * IDX_DIM ** -0.5)
    for j in range(IDX_HEADS):
        wb_scr[j] = jnp.broadcast_to(iw[:, j:j + 1], (TQ, HALF))
    for jp in range(IDX_HEADS // 2):
        iq2_scr[jp * TQ:(jp + 1) * TQ, :] = iq_ref[0, :, jp * LANE:(jp + 1) * LANE].astype(bf16)

    row = lax.broadcasted_iota(jnp.int32, (TQ, HALF), 0)
    col = lax.broadcasted_iota(jnp.int32, (TQ, HALF), 1)

    def score_block(kb, carry):
        rs = key_rows(kb)
        s_lo = _dot_nt(iq2_scr[...], iklo_scr[rs, :])
        s_hi = _dot_nt(iq2_scr[...], ikhi_scr[rs, :])
        for half in range(2):
            ls = slice(half * HALF, (half + 1) * HALF)
            acc = jnp.zeros((TQ, HALF), jnp.float32)
            for jp in range(IDX_HEADS // 2):
                rj = slice(jp * TQ, (jp + 1) * TQ)
                acc = (acc + jnp.maximum(s_lo[rj, ls], 0.0) * wb_scr[2 * jp]
                       + jnp.maximum(s_hi[rj, ls], 0.0) * wb_scr[2 * jp + 1])
            bits = pltpu.bitcast(acc, jnp.int32)
            key = jnp.where(bits >= 0, bits, bits ^ 0x7FFFFFFF)
            causal = (kb * KB + half * HALF + col) <= (qi * TQ + row)
            key_scr[kb, :, ls] = jnp.where(causal, key, INT_MIN)
        keyt_scr[kb] = key_scr[kb].T
        return carry

    lax.fori_loop(0, nkb, score_block, 0)

    def count_ge(cand):
        def body(kb, cnt):
            hit = (keyt_scr[kb] >= cand).astype(jnp.int32)
            return cnt + jnp.sum(hit.reshape(KB // 8, 8, TQ), axis=0)
        cnt = lax.fori_loop(0, nkb, body, jnp.zeros((8, TQ), jnp.int32))
        return jnp.sum(cnt, axis=0, keepdims=True)

    zero = jnp.zeros((1, TQ), jnp.int32)
    prefix = jnp.where(count_ge(zero) >= TOPK, zero, INT_MIN)

    def bit_pass(it, prefix):
        cand = prefix + lax.shift_left(jnp.int32(1), 30 - it)
        return jnp.where(count_ge(cand) >= TOPK, cand, prefix)

    prefix = lax.fori_loop(0, 31, bit_pass, prefix)
    thr = jnp.maximum(prefix, INT_MIN + 1)
    thrn_scr[...] = jnp.broadcast_to(thr, (KB, TQ)).T

    def madd_block(kb, carry):
        madd_scr[kb] = jnp.where(key_scr[kb] >= thrn_scr[...], 0.0, NEG_BIG)
        return carry

    lax.fori_loop(0, nkb, madd_block, 0)

    m_scr[...] = jnp.full(m_scr.shape, NEG_BIG, jnp.float32)
    accv_scr[...] = jnp.zeros_like(accv_scr)
    accl_scr[...] = jnp.zeros_like(accl_scr)
    scale2 = A_HEAD_DIM ** -0.5 * LOG2E

    def attend(kb, slot):
        rs = key_rows(kb)
        for h in range(A_HEADS):
            hs = slice(h * A_HEAD_DIM, (h + 1) * A_HEAD_DIM)
            lg = _dot_nt(q_ref[0, :, hs], k_ref[0, rs, hs]) * scale2 + madd_scr[kb]
            if slot is not None:
                lg = lg + band_ref[h, slot]
            m_old = m_scr[h]
            m_new = jnp.maximum(m_old, jnp.max(lg, axis=-1, keepdims=True))
            alpha = jnp.exp2(m_old - m_new)
            p = jnp.concatenate([jnp.exp2(lg[:, :HALF] - m_new), jnp.exp2(lg[:, HALF:] - m_new)], axis=1)
            pv = jnp.dot(p.astype(bf16), vext_scr[h, rs, :], preferred_element_type=jnp.float32)
            accv_scr[h] = accv_scr[h] * alpha + pv[:, :A_HEAD_DIM]
            accl_scr[h] = accl_scr[h] * alpha + pv[:, A_HEAD_DIM:]
            m_scr[h] = m_new

    def far_block(kb, carry):
        attend(kb, None)
        return carry

    lax.fori_loop(0, qi - 1, far_block, 0)

    @pl.when(qi >= 1)
    def _():
        attend(qi - 1, 0)

    attend(qi, 1)

    for h in range(A_HEADS):
        hs = slice(h * A_HEAD_DIM, (h + 1) * A_HEAD_DIM)
        g = ag_ref[0, :, hs]
        o_ref[0, :, hs] = (accv_scr[h] / accl_scr[h] * (g * jax.nn.sigmoid(g))).astype(o_ref.dtype)


def _dsa_attention(proj_b, proj_f, band):
    cb = lambda name: BF_OFF[name] // A_WIDTH
    once = pl.Buffered(1)
    return pl.pallas_call(
        _dsa_kernel,
        grid=(BATCH, NQ),
        in_specs=[
            pl.BlockSpec((1, TQ, A_WIDTH), lambda b, i: (b, i, cb("aq"))),
            pl.BlockSpec((1, SEQ, A_WIDTH), lambda b, i: (b, 0, cb("ak")), pipeline_mode=once),
            pl.BlockSpec((1, SEQ, A_WIDTH), lambda b, i: (b, 0, cb("av")), pipeline_mode=once),
            pl.BlockSpec((1, TQ, A_WIDTH), lambda b, i: (b, i, F32_OFF["iq"] // A_WIDTH)),
            pl.BlockSpec((1, SEQ, LANE), lambda b, i: (b, 0, F32_OFF["ik"] // LANE), pipeline_mode=once),
            pl.BlockSpec((1, TQ, LANE), lambda b, i: (b, i, F32_OFF["iw"] // LANE)),
            pl.BlockSpec((1, TQ, A_WIDTH), lambda b, i: (b, i, F32_OFF["ag"] // A_WIDTH)),
            pl.BlockSpec((A_HEADS, 2, TQ, KB), lambda b, i: (0, 0, 0, 0), pipeline_mode=once),
        ],
        out_specs=pl.BlockSpec((1, TQ, A_WIDTH), lambda b, i: (b, i, 0)),
        out_shape=jax.ShapeDtypeStruct((BATCH, SEQ, A_WIDTH), jnp.bfloat16),
        scratch_shapes=[
            pltpu.VMEM((SEQ, LANE), jnp.bfloat16),
            pltpu.VMEM((SEQ, LANE), jnp.bfloat16),
            pltpu.VMEM((A_HEADS, SEQ, 2 * A_HEAD_DIM), jnp.bfloat16),
            pltpu.VMEM((IDX_HEADS, TQ, HALF), jnp.float32),
            pltpu.VMEM((IDX_HEADS // 2 * TQ, LANE), jnp.bfloat16),
            pltpu.VMEM((NKB, TQ, KB), jnp.int32),
            pltpu.VMEM((NKB, KB, TQ), jnp.int32),
            pltpu.VMEM((TQ, KB), jnp.int32),
            pltpu.VMEM((NKB, TQ, KB), jnp.float32),
            pltpu.VMEM((A_HEADS, TQ, HALF), jnp.float32),
            pltpu.VMEM((A_HEADS, TQ, A_HEAD_DIM), jnp.float32),
            pltpu.VMEM((A_HEADS, TQ, A_HEAD_DIM), jnp.float32),
        ],
        compiler_params=pltpu.CompilerParams(
            dimension_semantics=("arbitrary", "arbitrary"), vmem_limit_bytes=VMEM_LIMIT),
        name="dsa_attention",
    )(proj_b, proj_b, proj_b, proj_f, proj_f, proj_f, proj_f, band)


GLA_CT = 512
GLA_C = 64


def _gla_kernel(bq_ref, bk_ref, bv_ref, bg_ref, ba_ref, wup_ref, balpha_ref, g_ref, o_ref, st_scr):
    @pl.when(pl.program_id(1) == 0)
    def _():
        st_scr[...] = jnp.zeros_like(st_scr)

    rr = lax.broadcasted_iota(jnp.int32, (GLA_C, GLA_C), 0)
    cc = lax.broadcasted_iota(jnp.int32, (GLA_C, GLA_C), 1)
    tri = rr >= cc
    tri_bf = tri.astype(jnp.bfloat16)
    wup = wup_ref[...].astype(jnp.bfloat16)
    balpha = balpha_ref[...]
    gain = g_ref[...]

    def chunk(c, carry):
        r0 = pl.multiple_of(c * GLA_C, GLA_C)
        rows = pl.ds(r0, GLA_C)
        ba = ba_ref[0, rows, :][:, :GATE_RANK].astype(jnp.bfloat16)
        pre = jnp.dot(ba, wup, preferred_element_type=jnp.float32) + balpha
        log_a = (jnp.minimum(pre, 0.0) - jnp.log1p(jnp.exp(-jnp.abs(pre)))) * (1.0 / GATE_TEMP)
        la_hi = log_a.astype(jnp.bfloat16)
        la_lo = (log_a - la_hi.astype(jnp.float32)).astype(jnp.bfloat16)
        bcum = (jnp.dot(tri_bf, la_hi, preferred_element_type=jnp.float32)
                + jnp.dot(tri_bf, la_lo, preferred_element_type=jnp.float32))
        for h in range(B_HEADS):
            ks = slice(h * B_DK, (h + 1) * B_DK)
            vs = slice(h * B_DV, (h + 1) * B_DV)
            b = bcum[:, ks]
            b_last = b[GLA_C - 1:GLA_C, :]
            q = bq_ref[0, rows, ks] * (B_DK ** -0.5)
            k = bk_ref[0, rows, ks]
            v = bv_ref[0, rows, vs].astype(jnp.bfloat16)
            qe = (q * jnp.exp(b)).astype(jnp.bfloat16)
            ke = (k * jnp.exp(-b)).astype(jnp.bfloat16)
            kd = (k * jnp.exp(b_last - b)).astype(jnp.bfloat16)
            attn = jnp.where(tri, _dot_nt(qe, ke), 0.0)
            st = st_scr[h]
            o = _dot_nt(qe, st.astype(jnp.bfloat16)) + jnp.dot(
                attn.astype(jnp.bfloat16), v, preferred_element_type=jnp.float32)
            st_scr[h] = st * jnp.exp(b_last) + _dot_tn(v, kd)
            on = o * lax.rsqrt(jnp.mean(o * o, axis=-1, keepdims=True) + EPS) * gain
            g = bg_ref[0, rows, vs]
            o_ref[0, rows, vs] = (on * (g * jax.nn.sigmoid(g))).astype(o_ref.dtype)
        return carry

    lax.fori_loop(0, GLA_CT // GLA_C, chunk, 0)


def _gla(proj_f, w_alpha_up, b_alpha, gla_g):
    return pl.pallas_call(
        _gla_kernel,
        grid=(BATCH, SEQ // GLA_CT),
        in_specs=[
            pl.BlockSpec((1, GLA_CT, B_KEY_WIDTH), lambda b, t: (b, t, F32_OFF["bq"] // B_KEY_WIDTH)),
            pl.BlockSpec((1, GLA_CT, B_KEY_WIDTH), lambda b, t: (b, t, F32_OFF["bk"] // B_KEY_WIDTH)),
            pl.BlockSpec((1, GLA_CT, B_WIDTH), lambda b, t: (b, t, F32_OFF["bv"] // B_WIDTH)),
            pl.BlockSpec((1, GLA_CT, B_WIDTH), lambda b, t: (b, t, F32_OFF["bg"] // B_WIDTH)),
            pl.BlockSpec((1, GLA_CT, LANE), lambda b, t: (b, t, F32_OFF["ba"] // LANE)),
            pl.BlockSpec((GATE_RANK, B_KEY_WIDTH), lambda b, t: (0, 0)),
            pl.BlockSpec((1, B_KEY_WIDTH), lambda b, t: (0, 0)),
            pl.BlockSpec((1, B_DV), lambda b, t: (0, 0)),
        ],
        out_specs=pl.BlockSpec((1, GLA_CT, B_WIDTH), lambda b, t: (b, t, 0)),
        out_shape=jax.ShapeDtypeStruct((BATCH, SEQ, B_WIDTH), jnp.bfloat16),
        scratch_shapes=[pltpu.VMEM((B_HEADS, B_DV, B_DK), jnp.float32)],
        compiler_params=pltpu.CompilerParams(
            dimension_semantics=("arbitrary", "arbitrary"), vmem_limit_bytes=VMEM_LIMIT),
        name="gla",
    )(proj_f, proj_f, proj_f, proj_f, proj_f, w_alpha_up, b_alpha.reshape(1, B_KEY_WIDTH),
      gla_g.reshape(1, B_DV))


OUT_TM = 512


def _out_kernel(a_ref, b_ref, wa_ref, wb_ref, x_ref, gate_ref, fg_ref, o_ref, *, final_norm):
    y = (jnp.dot(a_ref[...], wa_ref[...], preferred_element_type=jnp.float32)
         + jnp.dot(b_ref[...], wb_ref[...], preferred_element_type=jnp.float32))
    xn = x_ref[...] + gate_ref[0] * y
    if final_norm:
        r = xn * lax.rsqrt(jnp.mean(xn * xn, axis=-1, keepdims=True) + EPS)
        xn = r * fg_ref[...]
    o_ref[...] = xn


def _out_proj(a_out, b_out, w_out_bf, x2d, gate, final_g, layer, final_norm):
    m = x2d.shape[0]
    tiles_per_batch = SEQ // OUT_TM
    return pl.pallas_call(
        functools.partial(_out_kernel, final_norm=final_norm),
        grid=(m // OUT_TM,),
        in_specs=[
            pl.BlockSpec((OUT_TM, A_WIDTH), lambda i: (i, 0)),
            pl.BlockSpec((OUT_TM, B_WIDTH), lambda i: (i, 0)),
            pl.BlockSpec((None, A_WIDTH, D_MODEL), lambda i: (layer, 0, 0)),
            pl.BlockSpec((None, B_WIDTH, D_MODEL), lambda i: (layer, 1, 0)),
            pl.BlockSpec((OUT_TM, D_MODEL), lambda i: (i, 0)),
            pl.BlockSpec((1, 1, D_MODEL), lambda i: (i // tiles_per_batch, 0, 0)),
            pl.BlockSpec((1, D_MODEL), lambda i: (0, 0)),
        ],
        out_specs=pl.BlockSpec((OUT_TM, D_MODEL), lambda i: (i, 0)),
        out_shape=jax.ShapeDtypeStruct((m, D_MODEL), jnp.float32),
        compiler_params=pltpu.CompilerParams(
            dimension_semantics=("arbitrary",), vmem_limit_bytes=VMEM_LIMIT),
        name="out_proj",
    )(a_out, b_out, w_out_bf, w_out_bf, x2d, gate, final_g.reshape(1, D_MODEL))


def kernel(x, c, w_ada, b_ada, norm_g, w_in, w_alpha_up, b_alpha, gla_g, w_out, rel_bias, final_g):
    mod = _adaln_mod(c, w_ada, b_ada)
    band = _bias_band(rel_bias)
    w_in_p = _pack_w_in(jnp.swapaxes(w_in, 1, 2))
    w_out_bf = w_out.astype(jnp.bfloat16)
    x2d = x.reshape(BATCH * SEQ, D_MODEL)
    for l in range(DEPTH):
        shift = mod[l, :, 0:D_MODEL].reshape(BATCH, 1, D_MODEL)
        scale = mod[l, :, D_MODEL:2 * D_MODEL].reshape(BATCH, 1, D_MODEL)
        gate = mod[l, :, 2 * D_MODEL:].reshape(BATCH, 1, D_MODEL)
        proj_b, proj_f = _norm_proj(x2d, shift, scale, norm_g[l], w_in_p, l)
        proj_b = proj_b.reshape(BATCH, SEQ, NB_COLS)
        proj_f = proj_f.reshape(BATCH, SEQ, NF_COLS)
        a_out = _dsa_attention(proj_b, proj_f, band)
        b_out = _gla(proj_f, w_alpha_up[l], b_alpha[l], gla_g[l])
        x2d = _out_proj(a_out.reshape(BATCH * SEQ, A_WIDTH), b_out.reshape(BATCH * SEQ, B_WIDTH),
                        w_out_bf, x2d, gate, final_g, l, final_norm=(l == DEPTH - 1))
    return x2d.reshape(BATCH, SEQ, D_MODEL)
```

```python
import functools
import math

import numpy as np
import jax
import jax.numpy as jnp
from jax import lax
from jax.experimental import pallas as pl
from jax.experimental.pallas import tpu as pltpu

D_MODEL = 2048
BATCH = 4
SEQ = 2048
DEPTH = 4
A_WIDTH = 1024
A_HEADS = 8
A_HEAD_DIM = 128
IDX_HEADS = 16
IDX_DIM = 64
TOPK = min(256, SEQ // 4)
B_WIDTH = 1024
B_HEADS = 4
B_KEY_WIDTH = 512
B_DK = 128
B_DV = 256
GATE_RANK = 16
GATE_TEMP = 16.0
NUM_BUCKETS = 32
MAX_DISTANCE = 128
EPS = 1e-6

IN_WIDTHS = (A_WIDTH, A_WIDTH, A_WIDTH, A_WIDTH, IDX_HEADS * IDX_DIM, IDX_DIM, IDX_HEADS,
             B_KEY_WIDTH, B_KEY_WIDTH, B_WIDTH, B_WIDTH, GATE_RANK)
IN_NAMES = ("aq", "ak", "av", "ag", "iq", "ik", "iw", "bq", "bk", "bv", "bg", "ba")
IN_OFFSETS = dict(zip(IN_NAMES, np.concatenate([[0], np.cumsum(IN_WIDTHS)[:-1]]).tolist()))
IN_WIDTH_OF = dict(zip(IN_NAMES, IN_WIDTHS))
IN_COLS = sum(IN_WIDTHS)

LANE = 128
VMEM_LIMIT = 52 * 1024 * 1024

PROJ_TN = 512
BF_SEGS = (("aq", 1024), ("ak", 1024), ("av", 1024))
F32_SEGS = (("ag", 1024), ("iq", 1024), ("bv", 1024), ("bg", 1024), ("bq", 512), ("bk", 512),
            ("ik", LANE), ("iw", LANE), ("ba", LANE))
NB_COLS = sum(w for _, w in BF_SEGS)
NF_USED = sum(w for _, w in F32_SEGS)
NF_COLS = -(-NF_USED // PROJ_TN) * PROJ_TN
NP_COLS = NB_COLS + NF_COLS


def _seg_offsets(segs):
    offs, o = {}, 0
    for name, w in segs:
        offs[name] = o
        o += w
    return offs


BF_OFF = _seg_offsets(BF_SEGS)
F32_OFF = _seg_offsets(F32_SEGS)

INT_MIN = -2 ** 31
NEG_BIG = -1e30
LOG2E = math.log2(math.e)
I16_MIN = -2 ** 15
PACK16 = 16


PACK_COLS = 256


def _pack_kernel(w_ref, o_ref):
    dst = 0
    for name, width in BF_SEGS + F32_SEGS:
        src, used = IN_OFFSETS[name], IN_WIDTH_OF[name]
        o_ref[0, dst:dst + used, :] = w_ref[0, src:src + used, :].astype(jnp.bfloat16)
        if width > used:
            o_ref[0, dst + used:dst + width, :] = jnp.zeros((width - used, PACK_COLS), jnp.bfloat16)
        dst += width
    if dst < NP_COLS:
        o_ref[0, dst:, :] = jnp.zeros((NP_COLS - dst, PACK_COLS), jnp.bfloat16)


def _pack_w_in(w_in_t):
    return pl.pallas_call(
        _pack_kernel,
        grid=(DEPTH, D_MODEL // PACK_COLS),
        in_specs=[pl.BlockSpec((1, IN_COLS, PACK_COLS), lambda l, c: (l, 0, c))],
        out_specs=pl.BlockSpec((1, NP_COLS, PACK_COLS), lambda l, c: (l, 0, c)),
        out_shape=jax.ShapeDtypeStruct((DEPTH, NP_COLS, D_MODEL), jnp.bfloat16),
        compiler_params=pltpu.CompilerParams(
            dimension_semantics=("arbitrary", "arbitrary"), vmem_limit_bytes=VMEM_LIMIT),
        name="pack_w_in",
    )(w_in_t)


MOD_TN = 768


def _mod_kernel(c_ref, w_ref, b_ref, o_ref):
    c = c_ref[...]
    c_act = c * jax.nn.sigmoid(c)
    acc = jnp.dot(c_act.astype(jnp.bfloat16), w_ref[0].astype(jnp.bfloat16),
                  preferred_element_type=jnp.float32)
    o_ref[0] = acc + b_ref[0]


def _adaln_mod(c, w_ada, b_ada):
    cp = jnp.pad(c, ((0, 8 - BATCH), (0, 0)))
    out = pl.pallas_call(
        _mod_kernel,
        grid=(DEPTH, 3 * D_MODEL // MOD_TN),
        in_specs=[
            pl.BlockSpec((8, D_MODEL), lambda l, j: (0, 0)),
            pl.BlockSpec((1, D_MODEL, MOD_TN), lambda l, j: (l, 0, j)),
            pl.BlockSpec((1, 1, MOD_TN), lambda l, j: (l, 0, j)),
        ],
        out_specs=pl.BlockSpec((1, 8, MOD_TN), lambda l, j: (l, 0, j)),
        out_shape=jax.ShapeDtypeStruct((DEPTH, 8, 3 * D_MODEL), jnp.float32),
        compiler_params=pltpu.CompilerParams(
            dimension_semantics=("arbitrary", "arbitrary"), vmem_limit_bytes=VMEM_LIMIT),
        name="adaln_mod",
    )(cp, w_ada, b_ada.reshape(DEPTH, 1, 3 * D_MODEL))
    return out[:, :BATCH]


PROJ_TM = 1024
NB_TILES = NB_COLS // PROJ_TN


def _proj_kernel(x_ref, shift_ref, scale_ref, g_ref, w_ref, ob_ref, of_ref, h_scr):
    j = pl.program_id(1)

    @pl.when(j == 0)
    def _():
        x = x_ref[...]
        r = x * lax.rsqrt(jnp.mean(x * x, axis=-1, keepdims=True) + EPS)
        h = (r * g_ref[...]) * (1.0 + scale_ref[0]) + shift_ref[0]
        h_scr[...] = h.astype(jnp.bfloat16)

    @pl.when(j < NB_TILES)
    def _():
        ob_ref[...] = _dot_nt(h_scr[...], w_ref[...]).astype(jnp.bfloat16)

    @pl.when(j >= NB_TILES)
    def _():
        of_ref[...] = _dot_nt(h_scr[...], w_ref[...])


def _norm_proj(x2d, shift, scale, norm_g, w_p, layer):
    m = x2d.shape[0]
    tiles_per_batch = SEQ // PROJ_TM
    return pl.pallas_call(
        _proj_kernel,
        grid=(m // PROJ_TM, NP_COLS // PROJ_TN),
        in_specs=[
            pl.BlockSpec((PROJ_TM, D_MODEL), lambda i, j: (i, 0)),
            pl.BlockSpec((1, 1, D_MODEL), lambda i, j: (i // tiles_per_batch, 0, 0)),
            pl.BlockSpec((1, 1, D_MODEL), lambda i, j: (i // tiles_per_batch, 0, 0)),
            pl.BlockSpec((1, D_MODEL), lambda i, j: (0, 0)),
            pl.BlockSpec((None, PROJ_TN, D_MODEL), lambda i, j: (layer, j, 0)),
        ],
        out_specs=[
            pl.BlockSpec((PROJ_TM, PROJ_TN), lambda i, j: (i, jnp.minimum(j, NB_TILES - 1))),
            pl.BlockSpec((PROJ_TM, PROJ_TN), lambda i, j: (i, jnp.maximum(j - NB_TILES, 0))),
        ],
        out_shape=[
            jax.ShapeDtypeStruct((m, NB_COLS), jnp.bfloat16),
            jax.ShapeDtypeStruct((m, NF_COLS), jnp.float32),
        ],
        scratch_shapes=[pltpu.VMEM((PROJ_TM, D_MODEL), jnp.bfloat16)],
        compiler_params=pltpu.CompilerParams(
            dimension_semantics=("arbitrary", "arbitrary"), vmem_limit_bytes=VMEM_LIMIT),
        name="norm_proj",
    )(x2d, shift, scale, norm_g.reshape(1, D_MODEL), w_p)


TQ = 256
KB = 256
NQ = SEQ // TQ
NKB = SEQ // KB
HALF = KB // 2


def _t5_bucket_np(d):
    max_exact = NUM_BUCKETS // 2
    d = np.maximum(d, 0)
    df = np.maximum(d, 1).astype(np.float32)
    large = max_exact + (np.log(df / np.float32(max_exact)) / np.float32(math.log(MAX_DISTANCE / max_exact))
                         * np.float32(NUM_BUCKETS - max_exact)).astype(np.int32)
    large = np.minimum(large, NUM_BUCKETS - 1)
    return np.where(d < max_exact, d, large).astype(np.int32)


assert int(_t5_bucket_np(np.arange(KB, 2 * SEQ)).min()) == NUM_BUCKETS - 1


def _band_buckets():
    tl = np.arange(TQ)[:, None]
    u = np.arange(KB)[None, :]
    prev = _t5_bucket_np(KB + tl - u)
    diag = _t5_bucket_np(tl - u)
    return np.stack([prev, diag]).astype(np.int32)


BAND_ROWS = 64


def _band_kernel(rb_ref, bucket_ref, o_ref):
    h = pl.program_id(0)
    far = rb_ref[NUM_BUCKETS - 1, h]

    def rows(r, carry):
        rs = pl.ds(pl.multiple_of(r * BAND_ROWS, BAND_ROWS), BAND_ROWS)
        bucket = bucket_ref[0, rs, :]
        acc = jnp.zeros(bucket.shape, jnp.float32)
        for b in range(NUM_BUCKETS):
            acc = jnp.where(bucket == b, rb_ref[b, h] - far, acc)
        o_ref[0, 0, rs, :] = acc * LOG2E
        return carry

    lax.fori_loop(0, TQ // BAND_ROWS, rows, 0)


def _bias_band(rel_bias):
    return pl.pallas_call(
        _band_kernel,
        grid=(A_HEADS, 2),
        in_specs=[
            pl.BlockSpec(memory_space=pltpu.SMEM),
            pl.BlockSpec((1, TQ, KB), lambda h, s: (s, 0, 0)),
        ],
        out_specs=pl.BlockSpec((1, 1, TQ, KB), lambda h, s: (h, s, 0, 0)),
        out_shape=jax.ShapeDtypeStruct((A_HEADS, 2, TQ, KB), jnp.float32),
        name="bias_band",
    )(rel_bias, jnp.asarray(_band_buckets()))


def _dot_nt(a, b):
    return lax.dot_general(a, b, (((1,), (1,)), ((), ())), preferred_element_type=jnp.float32)


def _dot_tn(a, b):
    return lax.dot_general(a, b, (((0,), (0,)), ((), ())), preferred_element_type=jnp.float32)


def _dsa_kernel(q_ref, k_ref, v_ref, iq_ref, ik_ref, iw_ref, ag_ref, band_ref, o_ref,
                iklo_scr, ikhi_scr, vext_scr, wb_scr, iq2_scr, key_scr, khi_scr, klo_scr, thrn_scr,
                madd_scr, m_scr, accv_scr, accl_scr):
    qi = pl.program_id(1)
    nkb = qi + 1
    bf16 = jnp.bfloat16

    def key_rows(kb):
        return pl.ds(pl.multiple_of(kb * KB, KB), KB)

    @pl.when(qi == 0)
    def _():
        def prep(c, carry):
            rs = key_rows(c)
            ik = ik_ref[0, rs, :]
            iklo_scr[rs, :] = ik.astype(bf16)
            ikhi_scr[rs, :] = pltpu.roll(ik, IDX_DIM, axis=1).astype(bf16)
            for h in range(A_HEADS):
                vext_scr[h, rs, :A_HEAD_DIM] = v_ref[0, rs, h * A_HEAD_DIM:(h + 1) * A_HEAD_DIM]
                vext_scr[h, rs, A_HEAD_DIM:] = jnp.ones((KB, A_HEAD_DIM), bf16)
            return carry

        lax.fori_loop(0, NKB, prep, 0)

    iw = iw_ref[0][:, :IDX_HEADS] * (IDX_HEADS ** -0.5 * IDX_DIM ** -0.5)
    for j in range(IDX_HEADS):
        wb_scr[j] = jnp.broadcast_to(iw[:, j:j + 1], (TQ, HALF))
    for jp in range(IDX_HEADS // 2):
        iq2_scr[jp * TQ:(jp + 1) * TQ, :] = iq_ref[0, :, jp * LANE:(jp + 1) * LANE].astype(bf16)

    row = lax.broadcasted_iota(jnp.int32, (TQ, HALF), 0)
    col = lax.broadcasted_iota(jnp.int32, (TQ, HALF), 1)

    def score_block(kb, carry):
        rs = key_rows(kb)
        s_lo = _dot_nt(iq2_scr[...], iklo_scr[rs, :])
        s_hi = _dot_nt(iq2_scr[...], ikhi_scr[rs, :])
        for half in range(2):
            ls = slice(half * HALF, (half + 1) * HALF)
            acc = jnp.zeros((TQ, HALF), jnp.float32)
            for jp in range(IDX_HEADS // 2):
                rj = slice(jp * TQ, (jp + 1) * TQ)
                acc = (acc + jnp.maximum(s_lo[rj, ls], 0.0) * wb_scr[2 * jp]
                       + jnp.maximum(s_hi[rj, ls], 0.0) * wb_scr[2 * jp + 1])
            bits = pltpu.bitcast(acc, jnp.int32)
            key = jnp.where(bits >= 0, bits, bits ^ 0x7FFFFFFF)
            causal = (kb * KB + half * HALF + col) <= (qi * TQ + row)
            key_scr[kb, :, ls] = jnp.where(causal, key, INT_MIN)
        keyt = key_scr[kb].T
        khi_scr[kb] = lax.shift_right_arithmetic(keyt, 16).astype(jnp.int16)
        klo_scr[kb] = ((keyt & 0xFFFF) + I16_MIN).astype(jnp.int16)
        return carry

    lax.fori_loop(0, nkb, score_block, 0)

    def count(ref, cand, strictly=False):
        c16 = jnp.broadcast_to(cand.astype(jnp.int16), (PACK16, TQ))

        def body(kb, cnt):
            v = ref[kb]
            for r in range(KB // PACK16):
                blk = v[r * PACK16:(r + 1) * PACK16]
                hit = (blk > c16) if strictly else (blk >= c16)
                cnt = cnt + jnp.where(hit, jnp.int16(1), jnp.int16(0))
            return cnt

        cnt = lax.fori_loop(0, nkb, body, jnp.zeros((PACK16, TQ), jnp.int16))
        return jnp.sum(cnt.astype(jnp.int32), axis=0, keepdims=True)

    def bisect16(ref, need):
        zero = jnp.zeros((1, TQ), jnp.int32)
        prefix = jnp.where(count(ref, zero) >= need, zero, I16_MIN)

        def bit_pass(it, prefix):
            cand = prefix + lax.shift_left(jnp.int32(1), 14 - it)
            return jnp.where(count(ref, cand) >= need, cand, prefix)

        return lax.fori_loop(0, 15, bit_pass, prefix)

    topk = jnp.full((1, TQ), TOPK, jnp.int32)
    p_hi = bisect16(khi_scr, topk)
    above = count(khi_scr, p_hi, strictly=True)
    p_hi16 = jnp.broadcast_to(p_hi.astype(jnp.int16), (PACK16, TQ))

    def narrow(kb, carry):
        hi, lo = khi_scr[kb], klo_scr[kb]
        for r in range(KB // PACK16):
            rs = slice(r * PACK16, (r + 1) * PACK16)
            klo_scr[kb, rs, :] = jnp.where(hi[rs] == p_hi16, lo[rs], jnp.int16(I16_MIN))
        return carry

    lax.fori_loop(0, nkb, narrow, 0)
    p_lo = bisect16(klo_scr, topk - above)
    prefix = p_hi * 65536 + (p_lo - I16_MIN)
    thr = jnp.maximum(prefix, INT_MIN + 1)
    thrn_scr[...] = jnp.broadcast_to(thr, (KB, TQ)).T

    def madd_block(kb, carry):
        madd_scr[kb] = jnp.where(key_scr[kb] >= thrn_scr[...], 0.0, NEG_BIG)
        return carry

    lax.fori_loop(0, nkb, madd_block, 0)

    m_scr[...] = jnp.full(m_scr.shape, NEG_BIG, jnp.float32)
    accv_scr[...] = jnp.zeros_like(accv_scr)
    accl_scr[...] = jnp.zeros_like(accl_scr)
    scale2 = A_HEAD_DIM ** -0.5 * LOG2E

    def attend(kb, slot):
        rs = key_rows(kb)
        for h in range(A_HEADS):
            hs = slice(h * A_HEAD_DIM, (h + 1) * A_HEAD_DIM)
            lg = _dot_nt(q_ref[0, :, hs], k_ref[0, rs, hs]) * scale2 + madd_scr[kb]
            if slot is not None:
                lg = lg + band_ref[h, slot]
            m_old = m_scr[h]
            m_new = jnp.maximum(m_old, jnp.max(lg, axis=-1, keepdims=True))
            alpha = jnp.exp2(m_old - m_new)
            p = jnp.concatenate([jnp.exp2(lg[:, :HALF] - m_new), jnp.exp2(lg[:, HALF:] - m_new)], axis=1)
            pv = jnp.dot(p.astype(bf16), vext_scr[h, rs, :], preferred_element_type=jnp.float32)
            accv_scr[h] = accv_scr[h] * alpha + pv[:, :A_HEAD_DIM]
            accl_scr[h] = accl_scr[h] * alpha + pv[:, A_HEAD_DIM:]
            m_scr[h] = m_new

    def far_block(kb, carry):
        attend(kb, None)
        return carry

    lax.fori_loop(0, qi - 1, far_block, 0)

    @pl.when(qi >= 1)
    def _():
        attend(qi - 1, 0)

    attend(qi, 1)

    for h in range(A_HEADS):
        hs = slice(h * A_HEAD_DIM, (h + 1) * A_HEAD_DIM)
        g = ag_ref[0, :, hs]
        o_ref[0, :, hs] = (accv_scr[h] / accl_scr[h] * (g * jax.nn.sigmoid(g))).astype(o_ref.dtype)


def _dsa_attention(proj_b, proj_f, band):
    cb = lambda name: BF_OFF[name] // A_WIDTH
    once = pl.Buffered(1)
    return pl.pallas_call(
        _dsa_kernel,
        grid=(BATCH, NQ),
        in_specs=[
            pl.BlockSpec((1, TQ, A_WIDTH), lambda b, i: (b, i, cb("aq"))),
            pl.BlockSpec((1, SEQ, A_WIDTH), lambda b, i: (b, 0, cb("ak")), pipeline_mode=once),
            pl.BlockSpec((1, SEQ, A_WIDTH), lambda b, i: (b, 0, cb("av")), pipeline_mode=once),
            pl.BlockSpec((1, TQ, A_WIDTH), lambda b, i: (b, i, F32_OFF["iq"] // A_WIDTH)),
            pl.BlockSpec((1, SEQ, LANE), lambda b, i: (b, 0, F32_OFF["ik"] // LANE), pipeline_mode=once),
            pl.BlockSpec((1, TQ, LANE), lambda b, i: (b, i, F32_OFF["iw"] // LANE)),
            pl.BlockSpec((1, TQ, A_WIDTH), lambda b, i: (b, i, F32_OFF["ag"] // A_WIDTH)),
            pl.BlockSpec((A_HEADS, 2, TQ, KB), lambda b, i: (0, 0, 0, 0), pipeline_mode=once),
        ],
        out_specs=pl.BlockSpec((1, TQ, A_WIDTH), lambda b, i: (b, i, 0)),
        out_shape=jax.ShapeDtypeStruct((BATCH, SEQ, A_WIDTH), jnp.bfloat16),
        scratch_shapes=[
            pltpu.VMEM((SEQ, LANE), jnp.bfloat16),
            pltpu.VMEM((SEQ, LANE), jnp.bfloat16),
            pltpu.VMEM((A_HEADS, SEQ, 2 * A_HEAD_DIM), jnp.bfloat16),
            pltpu.VMEM((IDX_HEADS, TQ, HALF), jnp.float32),
            pltpu.VMEM((IDX_HEADS // 2 * TQ, LANE), jnp.bfloat16),
            pltpu.VMEM((NKB, TQ, KB), jnp.int32),
            pltpu.VMEM((NKB, KB, TQ), jnp.int16),
            pltpu.VMEM((NKB, KB, TQ), jnp.int16),
            pltpu.VMEM((TQ, KB), jnp.int32),
            pltpu.VMEM((NKB, TQ, KB), jnp.float32),
            pltpu.VMEM((A_HEADS, TQ, HALF), jnp.float32),
            pltpu.VMEM((A_HEADS, TQ, A_HEAD_DIM), jnp.float32),
            pltpu.VMEM((A_HEADS, TQ, A_HEAD_DIM), jnp.float32),
        ],
        compiler_params=pltpu.CompilerParams(
            dimension_semantics=("arbitrary", "arbitrary"), vmem_limit_bytes=VMEM_LIMIT),
        name="dsa_attention",
    )(proj_b, proj_b, proj_b, proj_f, proj_f, proj_f, proj_f, band)


GLA_CT = 512
GLA_C = 64


def _gla_kernel(bq_ref, bk_ref, bv_ref, bg_ref, ba_ref, wup_ref, balpha_ref, g_ref, o_ref, st_scr):
    @pl.when(pl.program_id(1) == 0)
    def _():
        st_scr[...] = jnp.zeros_like(st_scr)

    rr = lax.broadcasted_iota(jnp.int32, (GLA_C, GLA_C), 0)
    cc = lax.broadcasted_iota(jnp.int32, (GLA_C, GLA_C), 1)
    tri = rr >= cc
    tri_bf = tri.astype(jnp.bfloat16)
    wup = wup_ref[...].astype(jnp.bfloat16)
    balpha = balpha_ref[...]
    gain = g_ref[...]

    def chunk(c, carry):
        r0 = pl.multiple_of(c * GLA_C, GLA_C)
        rows = pl.ds(r0, GLA_C)
        ba = ba_ref[0, rows, :][:, :GATE_RANK].astype(jnp.bfloat16)
        pre = jnp.dot(ba, wup, preferred_element_type=jnp.float32) + balpha
        log_a = (jnp.minimum(pre, 0.0) - jnp.log1p(jnp.exp(-jnp.abs(pre)))) * (1.0 / GATE_TEMP)
        la_hi = log_a.astype(jnp.bfloat16)
        la_lo = (log_a - la_hi.astype(jnp.float32)).astype(jnp.bfloat16)
        bcum = (jnp.dot(tri_bf, la_hi, preferred_element_type=jnp.float32)
                + jnp.dot(tri_bf, la_lo, preferred_element_type=jnp.float32))
        for h in range(B_HEADS):
            ks = slice(h * B_DK, (h + 1) * B_DK)
            vs = slice(h * B_DV, (h + 1) * B_DV)
            b = bcum[:, ks]
            b_last = b[GLA_C - 1:GLA_C, :]
            q = bq_ref[0, rows, ks] * (B_DK ** -0.5)
            k = bk_ref[0, rows, ks]
            v = bv_ref[0, rows, vs].astype(jnp.bfloat16)
            qe = (q * jnp.exp(b)).astype(jnp.bfloat16)
            ke = (k * jnp.exp(-b)).astype(jnp.bfloat16)
            kd = (k * jnp.exp(b_last - b)).astype(jnp.bfloat16)
            attn = jnp.where(tri, _dot_nt(qe, ke), 0.0)
            st = st_scr[h]
            o = _dot_nt(qe, st.astype(jnp.bfloat16)) + jnp.dot(
                attn.astype(jnp.bfloat16), v, preferred_element_type=jnp.float32)
            st_scr[h] = st * jnp.exp(b_last) + _dot_tn(v, kd)
            on = o * lax.rsqrt(jnp.mean(o * o, axis=-1, keepdims=True) + EPS) * gain
            g = bg_ref[0, rows, vs]
            o_ref[0, rows, vs] = (on * (g * jax.nn.sigmoid(g))).astype(o_ref.dtype)
        return carry

    lax.fori_loop(0, GLA_CT // GLA_C, chunk, 0)


def _gla(proj_f, w_alpha_up, b_alpha, gla_g):
    return pl.pallas_call(
        _gla_kernel,
        grid=(BATCH, SEQ // GLA_CT),
        in_specs=[
            pl.BlockSpec((1, GLA_CT, B_KEY_WIDTH), lambda b, t: (b, t, F32_OFF["bq"] // B_KEY_WIDTH)),
            pl.BlockSpec((1, GLA_CT, B_KEY_WIDTH), lambda b, t: (b, t, F32_OFF["bk"] // B_KEY_WIDTH)),
            pl.BlockSpec((1, GLA_CT, B_WIDTH), lambda b, t: (b, t, F32_OFF["bv"] // B_WIDTH)),
            pl.BlockSpec((1, GLA_CT, B_WIDTH), lambda b, t: (b, t, F32_OFF["bg"] // B_WIDTH)),
            pl.BlockSpec((1, GLA_CT, LANE), lambda b, t: (b, t, F32_OFF["ba"] // LANE)),
            pl.BlockSpec((GATE_RANK, B_KEY_WIDTH), lambda b, t: (0, 0)),
            pl.BlockSpec((1, B_KEY_WIDTH), lambda b, t: (0, 0)),
            pl.BlockSpec((1, B_DV), lambda b, t: (0, 0)),
        ],
        out_specs=pl.BlockSpec((1, GLA_CT, B_WIDTH), lambda b, t: (b, t, 0)),
        out_shape=jax.ShapeDtypeStruct((BATCH, SEQ, B_WIDTH), jnp.bfloat16),
        scratch_shapes=[pltpu.VMEM((B_HEADS, B_DV, B_DK), jnp.float32)],
        compiler_params=pltpu.CompilerParams(
            dimension_semantics=("arbitrary", "arbitrary"), vmem_limit_bytes=VMEM_LIMIT),
        name="gla",
    )(proj_f, proj_f, proj_f, proj_f, proj_f, w_alpha_up, b_alpha.reshape(1, B_KEY_WIDTH),
      gla_g.reshape(1, B_DV))


OUT_TM = 512


def _out_kernel(a_ref, b_ref, wa_ref, wb_ref, x_ref, gate_ref, fg_ref, o_ref, *, final_norm):
    y = (jnp.dot(a_ref[...], wa_ref[...], preferred_element_type=jnp.float32)
         + jnp.dot(b_ref[...], wb_ref[...], preferred_element_type=jnp.float32))
    xn = x_ref[...] + gate_ref[0] * y
    if final_norm:
        r = xn * lax.rsqrt(jnp.mean(xn * xn, axis=-1, keepdims=True) + EPS)
        xn = r * fg_ref[...]
    o_ref[...] = xn


def _out_proj(a_out, b_out, w_out_bf, x2d, gate, final_g, layer, final_norm):
    m = x2d.shape[0]
    tiles_per_batch = SEQ // OUT_TM
    return pl.pallas_call(
        functools.partial(_out_kernel, final_norm=final_norm),
        grid=(m // OUT_TM,),
        in_specs=[
            pl.BlockSpec((OUT_TM, A_WIDTH), lambda i: (i, 0)),
            pl.BlockSpec((OUT_TM, B_WIDTH), lambda i: (i, 0)),
            pl.BlockSpec((None, A_WIDTH, D_MODEL), lambda i: (layer, 0, 0)),
            pl.BlockSpec((None, B_WIDTH, D_MODEL), lambda i: (layer, 1, 0)),
            pl.BlockSpec((OUT_TM, D_MODEL), lambda i: (i, 0)),
            pl.BlockSpec((1, 1, D_MODEL), lambda i: (i // tiles_per_batch, 0, 0)),
            pl.BlockSpec((1, D_MODEL), lambda i: (0, 0)),
        ],
        out_specs=pl.BlockSpec((OUT_TM, D_MODEL), lambda i: (i, 0)),
        out_shape=jax.ShapeDtypeStruct((m, D_MODEL), jnp.float32),
        compiler_params=pltpu.CompilerParams(
            dimension_semantics=("arbitrary",), vmem_limit_bytes=VMEM_LIMIT),
        name="out_proj",
    )(a_out, b_out, w_out_bf, w_out_bf, x2d, gate, final_g.reshape(1, D_MODEL))


def kernel(x, c, w_ada, b_ada, norm_g, w_in, w_alpha_up, b_alpha, gla_g, w_out, rel_bias, final_g):
    mod = _adaln_mod(c, w_ada, b_ada)
    band = _bias_band(rel_bias)
    w_in_p = _pack_w_in(jnp.swapaxes(w_in, 1, 2))
    w_out_bf = w_out.astype(jnp.bfloat16)
    x2d = x.reshape(BATCH * SEQ, D_MODEL)
    for l in range(DEPTH):
        shift = mod[l, :, 0:D_MODEL].reshape(BATCH, 1, D_MODEL)
        scale = mod[l, :, D_MODEL:2 * D_MODEL].reshape(BATCH, 1, D_MODEL)
        gate = mod[l, :, 2 * D_MODEL:].reshape(BATCH, 1, D_MODEL)
        proj_b, proj_f = _norm_proj(x2d, shift, scale, norm_g[l], w_in_p, l)
        proj_b = proj_b.reshape(BATCH, SEQ, NB_COLS)
        proj_f = proj_f.reshape(BATCH, SEQ, NF_COLS)
        a_out = _dsa_attention(proj_b, proj_f, band)
        b_out = _gla(proj_f, w_alpha_up[l], b_alpha[l], gla_g[l])
        x2d = _out_proj(a_out.reshape(BATCH * SEQ, A_WIDTH), b_out.reshape(BATCH * SEQ, B_WIDTH),
                        w_out_bf, x2d, gate, final_g, l, final_norm=(l == DEPTH - 1))
    return x2d.reshape(BATCH, SEQ, D_MODEL)
```

```python
import functools
import math

import numpy as np
import jax
import jax.numpy as jnp
from jax import lax
from jax.experimental import pallas as pl
from jax.experimental.pallas import tpu as pltpu

D_MODEL = 2048
BATCH = 4
SEQ = 2048
DEPTH = 4
A_WIDTH = 1024
A_HEADS = 8
A_HEAD_DIM = 128
IDX_HEADS = 16
IDX_DIM = 64
TOPK = min(256, SEQ // 4)
B_WIDTH = 1024
B_HEADS = 4
B_KEY_WIDTH = 512
B_DK = 128
B_DV = 256
GATE_RANK = 16
GATE_TEMP = 16.0
NUM_BUCKETS = 32
MAX_DISTANCE = 128
EPS = 1e-6

IN_WIDTHS = (A_WIDTH, A_WIDTH, A_WIDTH, A_WIDTH, IDX_HEADS * IDX_DIM, IDX_DIM, IDX_HEADS,
             B_KEY_WIDTH, B_KEY_WIDTH, B_WIDTH, B_WIDTH, GATE_RANK)
IN_NAMES = ("aq", "ak", "av", "ag", "iq", "ik", "iw", "bq", "bk", "bv", "bg", "ba")
IN_OFFSETS = dict(zip(IN_NAMES, np.concatenate([[0], np.cumsum(IN_WIDTHS)[:-1]]).tolist()))
IN_WIDTH_OF = dict(zip(IN_NAMES, IN_WIDTHS))
IN_COLS = sum(IN_WIDTHS)

LANE = 128
VMEM_LIMIT = 52 * 1024 * 1024

PROJ_TN = 512
BF_SEGS = (("aq", 1024), ("ak", 1024), ("av", 1024))
F32_SEGS = (("ag", 1024), ("iq", 1024), ("bv", 1024), ("bg", 1024), ("bq", 512), ("bk", 512),
            ("ik", LANE), ("iw", LANE), ("ba", LANE))
NB_COLS = sum(w for _, w in BF_SEGS)
NF_USED = sum(w for _, w in F32_SEGS)
NF_COLS = -(-NF_USED // PROJ_TN) * PROJ_TN
NP_COLS = NB_COLS + NF_COLS


def _seg_offsets(segs):
    offs, o = {}, 0
    for name, w in segs:
        offs[name] = o
        o += w
    return offs


BF_OFF = _seg_offsets(BF_SEGS)
F32_OFF = _seg_offsets(F32_SEGS)

INT_MIN = -2 ** 31
NEG_BIG = -1e30
LOG2E = math.log2(math.e)
I16_MIN = -2 ** 15
PACK16 = 16


PACK_COLS = 256


def _pack_kernel(w_ref, o_ref):
    dst = 0
    for name, width in BF_SEGS + F32_SEGS:
        src, used = IN_OFFSETS[name], IN_WIDTH_OF[name]
        o_ref[0, dst:dst + used, :] = w_ref[0, src:src + used, :].astype(jnp.bfloat16)
        if width > used:
            o_ref[0, dst + used:dst + width, :] = jnp.zeros((width - used, PACK_COLS), jnp.bfloat16)
        dst += width
    if dst < NP_COLS:
        o_ref[0, dst:, :] = jnp.zeros((NP_COLS - dst, PACK_COLS), jnp.bfloat16)


def _pack_w_in(w_in_t):
    return pl.pallas_call(
        _pack_kernel,
        grid=(DEPTH, D_MODEL // PACK_COLS),
        in_specs=[pl.BlockSpec((1, IN_COLS, PACK_COLS), lambda l, c: (l, 0, c))],
        out_specs=pl.BlockSpec((1, NP_COLS, PACK_COLS), lambda l, c: (l, 0, c)),
        out_shape=jax.ShapeDtypeStruct((DEPTH, NP_COLS, D_MODEL), jnp.bfloat16),
        compiler_params=pltpu.CompilerParams(
            dimension_semantics=("arbitrary", "arbitrary"), vmem_limit_bytes=VMEM_LIMIT),
        name="pack_w_in",
    )(w_in_t)


MOD_TN = 768


def _mod_kernel(c_ref, w_ref, b_ref, o_ref):
    c = c_ref[...]
    c_act = c * jax.nn.sigmoid(c)
    acc = jnp.dot(c_act.astype(jnp.bfloat16), w_ref[0].astype(jnp.bfloat16),
                  preferred_element_type=jnp.float32)
    o_ref[0] = acc + b_ref[0]


def _adaln_mod(c, w_ada, b_ada):
    cp = jnp.pad(c, ((0, 8 - BATCH), (0, 0)))
    out = pl.pallas_call(
        _mod_kernel,
        grid=(DEPTH, 3 * D_MODEL // MOD_TN),
        in_specs=[
            pl.BlockSpec((8, D_MODEL), lambda l, j: (0, 0)),
            pl.BlockSpec((1, D_MODEL, MOD_TN), lambda l, j: (l, 0, j)),
            pl.BlockSpec((1, 1, MOD_TN), lambda l, j: (l, 0, j)),
        ],
        out_specs=pl.BlockSpec((1, 8, MOD_TN), lambda l, j: (l, 0, j)),
        out_shape=jax.ShapeDtypeStruct((DEPTH, 8, 3 * D_MODEL), jnp.float32),
        compiler_params=pltpu.CompilerParams(
            dimension_semantics=("arbitrary", "arbitrary"), vmem_limit_bytes=VMEM_LIMIT),
        name="adaln_mod",
    )(cp, w_ada, b_ada.reshape(DEPTH, 1, 3 * D_MODEL))
    return out[:, :BATCH]


PROJ_TM = 1024
NB_TILES = NB_COLS // PROJ_TN


def _proj_kernel(x_ref, shift_ref, scale_ref, g_ref, w_ref, ob_ref, of_ref, h_scr):
    j = pl.program_id(1)

    @pl.when(j == 0)
    def _():
        x = x_ref[...]
        r = x * lax.rsqrt(jnp.mean(x * x, axis=-1, keepdims=True) + EPS)
        h = (r * g_ref[...]) * (1.0 + scale_ref[0]) + shift_ref[0]
        h_scr[...] = h.astype(jnp.bfloat16)

    @pl.when(j < NB_TILES)
    def _():
        ob_ref[...] = _dot_nt(h_scr[...], w_ref[...]).astype(jnp.bfloat16)

    @pl.when(j >= NB_TILES)
    def _():
        of_ref[...] = _dot_nt(h_scr[...], w_ref[...])


def _norm_proj(x2d, shift, scale, norm_g, w_p, layer):
    m = x2d.shape[0]
    tiles_per_batch = SEQ // PROJ_TM
    return pl.pallas_call(
        _proj_kernel,
        grid=(m // PROJ_TM, NP_COLS // PROJ_TN),
        in_specs=[
            pl.BlockSpec((PROJ_TM, D_MODEL), lambda i, j: (i, 0)),
            pl.BlockSpec((1, 1, D_MODEL), lambda i, j: (i // tiles_per_batch, 0, 0)),
            pl.BlockSpec((1, 1, D_MODEL), lambda i, j: (i // tiles_per_batch, 0, 0)),
            pl.BlockSpec((1, D_MODEL), lambda i, j: (0, 0)),
            pl.BlockSpec((None, PROJ_TN, D_MODEL), lambda i, j: (layer, j, 0)),
        ],
        out_specs=[
            pl.BlockSpec((PROJ_TM, PROJ_TN), lambda i, j: (i, jnp.minimum(j, NB_TILES - 1))),
            pl.BlockSpec((PROJ_TM, PROJ_TN), lambda i, j: (i, jnp.maximum(j - NB_TILES, 0))),
        ],
        out_shape=[
            jax.ShapeDtypeStruct((m, NB_COLS), jnp.bfloat16),
            jax.ShapeDtypeStruct((m, NF_COLS), jnp.float32),
        ],
        scratch_shapes=[pltpu.VMEM((PROJ_TM, D_MODEL), jnp.bfloat16)],
        compiler_params=pltpu.CompilerParams(
            dimension_semantics=("arbitrary", "arbitrary"), vmem_limit_bytes=VMEM_LIMIT),
        name="norm_proj",
    )(x2d, shift, scale, norm_g.reshape(1, D_MODEL), w_p)


TQ = 256
KB = 256
NQ = SEQ // TQ
NKB = SEQ // KB
HALF = KB // 2


def _t5_bucket_np(d):
    max_exact = NUM_BUCKETS // 2
    d = np.maximum(d, 0)
    df = np.maximum(d, 1).astype(np.float32)
    large = max_exact + (np.log(df / np.float32(max_exact)) / np.float32(math.log(MAX_DISTANCE / max_exact))
                         * np.float32(NUM_BUCKETS - max_exact)).astype(np.int32)
    large = np.minimum(large, NUM_BUCKETS - 1)
    return np.where(d < max_exact, d, large).astype(np.int32)


assert int(_t5_bucket_np(np.arange(KB, 2 * SEQ)).min()) == NUM_BUCKETS - 1


def _band_buckets():
    tl = np.arange(TQ)[:, None]
    u = np.arange(KB)[None, :]
    prev = _t5_bucket_np(KB + tl - u)
    diag = _t5_bucket_np(tl - u)
    return np.stack([prev, diag]).astype(np.int32)


BAND_ROWS = 64


def _band_kernel(rb_ref, bucket_ref, o_ref):
    h = pl.program_id(0)
    far = rb_ref[NUM_BUCKETS - 1, h]

    def rows(r, carry):
        rs = pl.ds(pl.multiple_of(r * BAND_ROWS, BAND_ROWS), BAND_ROWS)
        bucket = bucket_ref[0, rs, :]
        acc = jnp.zeros(bucket.shape, jnp.float32)
        for b in range(NUM_BUCKETS):
            acc = jnp.where(bucket == b, rb_ref[b, h] - far, acc)
        o_ref[0, 0, rs, :] = acc * LOG2E
        return carry

    lax.fori_loop(0, TQ // BAND_ROWS, rows, 0)


def _bias_band(rel_bias):
    return pl.pallas_call(
        _band_kernel,
        grid=(A_HEADS, 2),
        in_specs=[
            pl.BlockSpec(memory_space=pltpu.SMEM),
            pl.BlockSpec((1, TQ, KB), lambda h, s: (s, 0, 0)),
        ],
        out_specs=pl.BlockSpec((1, 1, TQ, KB), lambda h, s: (h, s, 0, 0)),
        out_shape=jax.ShapeDtypeStruct((A_HEADS, 2, TQ, KB), jnp.float32),
        name="bias_band",
    )(rel_bias, jnp.asarray(_band_buckets()))


def _dot_nt(a, b):
    return lax.dot_general(a, b, (((1,), (1,)), ((), ())), preferred_element_type=jnp.float32)


def _dot_tn(a, b):
    return lax.dot_general(a, b, (((0,), (0,)), ((), ())), preferred_element_type=jnp.float32)


def _dsa_kernel(q_ref, k_ref, v_ref, iq_ref, ik_ref, iw_ref, ag_ref, band_ref, o_ref,
                iklo_scr, ikhi_scr, vext_scr, wb_scr, iq2_scr, key_scr, khi_scr, klo_scr, thrn_scr,
                madd_scr, m_scr, accv_scr, accl_scr):
    qi = pl.program_id(1)
    nkb = qi + 1
    bf16 = jnp.bfloat16

    def key_rows(kb):
        return pl.ds(pl.multiple_of(kb * KB, KB), KB)

    @pl.when(qi == 0)
    def _():
        def prep(c, carry):
            rs = key_rows(c)
            ik = ik_ref[0, rs, :]
            iklo_scr[rs, :] = ik.astype(bf16)
            ikhi_scr[rs, :] = pltpu.roll(ik, IDX_DIM, axis=1).astype(bf16)
            for h in range(A_HEADS):
                vext_scr[h, rs, :A_HEAD_DIM] = v_ref[0, rs, h * A_HEAD_DIM:(h + 1) * A_HEAD_DIM]
                vext_scr[h, rs, A_HEAD_DIM:] = jnp.ones((KB, A_HEAD_DIM), bf16)
            return carry

        lax.fori_loop(0, NKB, prep, 0)

    iw = iw_ref[0][:, :IDX_HEADS] * (IDX_HEADS ** -0.5 * IDX_DIM ** -0.5)
    for j in range(IDX_HEADS):
        wb_scr[j] = jnp.broadcast_to(iw[:, j:j + 1], (TQ, HALF))
    for jp in range(IDX_HEADS // 2):
        iq2_scr[jp * TQ:(jp + 1) * TQ, :] = iq_ref[0, :, jp * LANE:(jp + 1) * LANE].astype(bf16)

    row = lax.broadcasted_iota(jnp.int32, (TQ, HALF), 0)
    col = lax.broadcasted_iota(jnp.int32, (TQ, HALF), 1)

    def score_block(kb, carry):
        rs = key_rows(kb)
        s_lo = _dot_nt(iq2_scr[...], iklo_scr[rs, :])
        s_hi = _dot_nt(iq2_scr[...], ikhi_scr[rs, :])
        for half in range(2):
            ls = slice(half * HALF, (half + 1) * HALF)
            acc = jnp.zeros((TQ, HALF), jnp.float32)
            for jp in range(IDX_HEADS // 2):
                rj = slice(jp * TQ, (jp + 1) * TQ)
                acc = (acc + jnp.maximum(s_lo[rj, ls], 0.0) * wb_scr[2 * jp]
                       + jnp.maximum(s_hi[rj, ls], 0.0) * wb_scr[2 * jp + 1])
            bits = pltpu.bitcast(acc, jnp.int32)
            key = jnp.where(bits >= 0, bits, bits ^ 0x7FFFFFFF)
            causal = (kb * KB + half * HALF + col) <= (qi * TQ + row)
            key_scr[kb, :, ls] = jnp.where(causal, key, INT_MIN)
        keyt = key_scr[kb].T
        khi_scr[kb] = lax.shift_right_arithmetic(keyt, 16).astype(jnp.int16)
        klo_scr[kb] = ((keyt & 0xFFFF) + I16_MIN).astype(jnp.int16)
        return carry

    lax.fori_loop(0, nkb, score_block, 0)

    def count(ref, cand, strictly=False):
        c16 = jnp.broadcast_to(cand.astype(jnp.int16), (PACK16, TQ))

        def body(kb, cnt):
            v = ref[kb]
            for r in range(KB // PACK16):
                blk = v[r * PACK16:(r + 1) * PACK16]
                hit = (blk > c16) if strictly else (blk >= c16)
                cnt = cnt + jnp.where(hit, jnp.int16(1), jnp.int16(0))
            return cnt

        cnt = lax.fori_loop(0, nkb, body, jnp.zeros((PACK16, TQ), jnp.int16))
        return jnp.sum(cnt.astype(jnp.int32), axis=0, keepdims=True)

    def bisect16(ref, need):
        zero = jnp.zeros((1, TQ), jnp.int32)
        prefix = jnp.where(count(ref, zero) >= need, zero, I16_MIN)

        def bit_pass(it, prefix):
            cand = prefix + lax.shift_left(jnp.int32(1), 14 - it)
            return jnp.where(count(ref, cand) >= need, cand, prefix)

        return lax.fori_loop(0, 15, bit_pass, prefix)

    topk = jnp.full((1, TQ), TOPK, jnp.int32)
    p_hi = bisect16(khi_scr, topk)
    above = count(khi_scr, p_hi, strictly=True)
    p_hi16 = jnp.broadcast_to(p_hi.astype(jnp.int16), (PACK16, TQ))

    def narrow(kb, carry):
        hi, lo = khi_scr[kb], klo_scr[kb]
        for r in range(KB // PACK16):
            rs = slice(r * PACK16, (r + 1) * PACK16)
            klo_scr[kb, rs, :] = jnp.where(hi[rs] == p_hi16, lo[rs], jnp.int16(I16_MIN))
        return carry

    lax.fori_loop(0, nkb, narrow, 0)
    p_lo = bisect16(klo_scr, topk - above)
    prefix = p_hi * 65536 + (p_lo - I16_MIN)
    thr = jnp.maximum(prefix, INT_MIN + 1)
    thrn_scr[...] = jnp.broadcast_to(thr, (KB, TQ)).T

    def madd_block(kb, carry):
        madd_scr[kb] = jnp.where(key_scr[kb] >= thrn_scr[...], 0.0, NEG_BIG)
        return carry

    lax.fori_loop(0, nkb, madd_block, 0)

    m_scr[...] = jnp.full(m_scr.shape, NEG_BIG, jnp.float32)
    accv_scr[...] = jnp.zeros_like(accv_scr)
    accl_scr[...] = jnp.zeros_like(accl_scr)
    scale2 = A_HEAD_DIM ** -0.5 * LOG2E

    def attend(kb, slot):
        rs = key_rows(kb)
        for h in range(A_HEADS):
            hs = slice(h * A_HEAD_DIM, (h + 1) * A_HEAD_DIM)
            lg = _dot_nt(q_ref[0, :, hs], k_ref[0, rs, hs]) * scale2 + madd_scr[kb]
            if slot is not None:
                lg = lg + band_ref[h, slot]
            m_old = m_scr[h]
            m_new = jnp.maximum(m_old, jnp.max(lg, axis=-1, keepdims=True))
            alpha = jnp.exp2(m_old - m_new)
            p = jnp.concatenate([jnp.exp2(lg[:, :HALF] - m_new), jnp.exp2(lg[:, HALF:] - m_new)], axis=1)
            pv = jnp.dot(p.astype(bf16), vext_scr[h, rs, :], preferred_element_type=jnp.float32)
            accv_scr[h] = accv_scr[h] * alpha + pv[:, :A_HEAD_DIM]
            accl_scr[h] = accl_scr[h] * alpha + pv[:, A_HEAD_DIM:]
            m_scr[h] = m_new

    def far_block(kb, carry):
        attend(kb, None)
        return carry

    lax.fori_loop(0, qi - 1, far_block, 0)

    @pl.when(qi >= 1)
    def _():
        attend(qi - 1, 0)

    attend(qi, 1)

    for h in range(A_HEADS):
        hs = slice(h * A_HEAD_DIM, (h + 1) * A_HEAD_DIM)
        g = ag_ref[0, :, hs]
        o_ref[0, :, hs] = (accv_scr[h] / accl_scr[h] * (g * jax.nn.sigmoid(g))).astype(o_ref.dtype)


def _dsa_attention(proj_b, proj_f, band):
    cb = lambda name: BF_OFF[name] // A_WIDTH
    once = pl.Buffered(1)
    return pl.pallas_call(
        _dsa_kernel,
        grid=(BATCH, NQ),
        in_specs=[
            pl.BlockSpec((1, TQ, A_WIDTH), lambda b, i: (b, i, cb("aq"))),
            pl.BlockSpec((1, SEQ, A_WIDTH), lambda b, i: (b, 0, cb("ak")), pipeline_mode=once),
            pl.BlockSpec((1, SEQ, A_WIDTH), lambda b, i: (b, 0, cb("av")), pipeline_mode=once),
            pl.BlockSpec((1, TQ, A_WIDTH), lambda b, i: (b, i, F32_OFF["iq"] // A_WIDTH)),
            pl.BlockSpec((1, SEQ, LANE), lambda b, i: (b, 0, F32_OFF["ik"] // LANE), pipeline_mode=once),
            pl.BlockSpec((1, TQ, LANE), lambda b, i: (b, i, F32_OFF["iw"] // LANE)),
            pl.BlockSpec((1, TQ, A_WIDTH), lambda b, i: (b, i, F32_OFF["ag"] // A_WIDTH)),
            pl.BlockSpec((A_HEADS, 2, TQ, KB), lambda b, i: (0, 0, 0, 0), pipeline_mode=once),
        ],
        out_specs=pl.BlockSpec((1, TQ, A_WIDTH), lambda b, i: (b, i, 0)),
        out_shape=jax.ShapeDtypeStruct((BATCH, SEQ, A_WIDTH), jnp.bfloat16),
        scratch_shapes=[
            pltpu.VMEM((SEQ, LANE), jnp.bfloat16),
            pltpu.VMEM((SEQ, LANE), jnp.bfloat16),
            pltpu.VMEM((A_HEADS, SEQ, 2 * A_HEAD_DIM), jnp.bfloat16),
            pltpu.VMEM((IDX_HEADS, TQ, HALF), jnp.float32),
            pltpu.VMEM((IDX_HEADS // 2 * TQ, LANE), jnp.bfloat16),
            pltpu.VMEM((NKB, TQ, KB), jnp.int32),
            pltpu.VMEM((NKB, KB, TQ), jnp.int16),
            pltpu.VMEM((NKB, KB, TQ), jnp.int16),
            pltpu.VMEM((TQ, KB), jnp.int32),
            pltpu.VMEM((NKB, TQ, KB), jnp.float32),
            pltpu.VMEM((A_HEADS, TQ, HALF), jnp.float32),
            pltpu.VMEM((A_HEADS, TQ, A_HEAD_DIM), jnp.float32),
            pltpu.VMEM((A_HEADS, TQ, A_HEAD_DIM), jnp.float32),
        ],
        compiler_params=pltpu.CompilerParams(
            dimension_semantics=("arbitrary", "arbitrary"), vmem_limit_bytes=VMEM_LIMIT),
        name="dsa_attention",
    )(proj_b, proj_b, proj_b, proj_f, proj_f, proj_f, proj_f, band)


GLA_CT = 512
GLA_C = 64
GLA_SC = 256
GLA_NCS = GLA_SC // GLA_C


def _gla_kernel(bq_ref, bk_ref, bv_ref, bg_ref, ba_ref, wup_ref, balpha_ref, g_ref, o_ref, st_scr):
    @pl.when(pl.program_id(1) == 0)
    def _():
        st_scr[...] = jnp.zeros_like(st_scr)

    bf16 = jnp.bfloat16
    rr = lax.broadcasted_iota(jnp.int32, (GLA_SC, GLA_SC), 0)
    cc = lax.broadcasted_iota(jnp.int32, (GLA_SC, GLA_SC), 1)
    tri = (rr >= cc) & (rr // GLA_C == cc // GLA_C)
    tri_bf = tri.astype(bf16)
    er = lax.broadcasted_iota(jnp.int32, (GLA_SC, GLA_NCS * B_DK), 0)
    ec = lax.broadcasted_iota(jnp.int32, (GLA_SC, GLA_NCS * B_DK), 1)
    own_block = (er // GLA_C) == (ec // B_DK)
    wup = wup_ref[...].astype(bf16)
    balpha = balpha_ref[...]
    gain = g_ref[...]
    heads = range(B_HEADS)

    def expand(a):
        return jnp.where(own_block, jnp.concatenate([a] * GLA_NCS, axis=1), jnp.zeros((), a.dtype))

    def body(sc, carry):
        rows = pl.ds(pl.multiple_of(sc * GLA_SC, GLA_SC), GLA_SC)
        ba = ba_ref[0, rows, :][:, :GATE_RANK].astype(bf16)
        pre = jnp.dot(ba, wup, preferred_element_type=jnp.float32) + balpha
        log_a = (jnp.minimum(pre, 0.0) - jnp.log1p(jnp.exp(-jnp.abs(pre)))) * (1.0 / GATE_TEMP)
        la_hi = log_a.astype(bf16)
        la_lo = (log_a - la_hi.astype(jnp.float32)).astype(bf16)
        bcum = (jnp.dot(tri_bf, la_hi, preferred_element_type=jnp.float32)
                + jnp.dot(tri_bf, la_lo, preferred_element_type=jnp.float32))
        ks = [slice(h * B_DK, (h + 1) * B_DK) for h in heads]
        vs = [slice(h * B_DV, (h + 1) * B_DV) for h in heads]
        qe, ke, kd, dec, v = [], [], [], [], []
        for h in heads:
            b = bcum[:, ks[h]]
            b_end = b.reshape(GLA_NCS, GLA_C, B_DK)[:, GLA_C - 1:GLA_C, :]
            b_end_rows = jnp.broadcast_to(b_end, (GLA_NCS, GLA_C, B_DK)).reshape(GLA_SC, B_DK)
            q = bq_ref[0, rows, ks[h]] * (B_DK ** -0.5)
            k = bk_ref[0, rows, ks[h]]
            qe.append((q * jnp.exp(b)).astype(bf16))
            ke.append((k * jnp.exp(-b)).astype(bf16))
            kd.append((k * jnp.exp(b_end_rows - b)).astype(bf16))
            dec.append(jnp.exp(b_end))
            v.append(bv_ref[0, rows, vs[h]].astype(bf16))
        attn = [jnp.where(tri, _dot_nt(qe[h], ke[h]), 0.0).astype(bf16) for h in heads]
        o_intra = [jnp.dot(attn[h], v[h], preferred_element_type=jnp.float32) for h in heads]
        upd = [_dot_tn(v[h], expand(kd[h])) for h in heads]
        o_inter = []
        for h in heads:
            st = st_scr[h]
            states = []
            for c in range(GLA_NCS):
                states.append(st.astype(bf16))
                st = st * dec[h][c] + upd[h][:, c * B_DK:(c + 1) * B_DK]
            st_scr[h] = st
            o_inter.append(_dot_nt(expand(qe[h]), jnp.concatenate(states, axis=1)))
        for h in heads:
            o = o_intra[h] + o_inter[h]
            on = o * lax.rsqrt(jnp.mean(o * o, axis=-1, keepdims=True) + EPS) * gain
            g = bg_ref[0, rows, vs[h]]
            o_ref[0, rows, vs[h]] = (on * (g * jax.nn.sigmoid(g))).astype(o_ref.dtype)
        return carry

    lax.fori_loop(0, GLA_CT // GLA_SC, body, 0)


def _gla(proj_f, w_alpha_up, b_alpha, gla_g):
    return pl.pallas_call(
        _gla_kernel,
        grid=(BATCH, SEQ // GLA_CT),
        in_specs=[
            pl.BlockSpec((1, GLA_CT, B_KEY_WIDTH), lambda b, t: (b, t, F32_OFF["bq"] // B_KEY_WIDTH)),
            pl.BlockSpec((1, GLA_CT, B_KEY_WIDTH), lambda b, t: (b, t, F32_OFF["bk"] // B_KEY_WIDTH)),
            pl.BlockSpec((1, GLA_CT, B_WIDTH), lambda b, t: (b, t, F32_OFF["bv"] // B_WIDTH)),
            pl.BlockSpec((1, GLA_CT, B_WIDTH), lambda b, t: (b, t, F32_OFF["bg"] // B_WIDTH)),
            pl.BlockSpec((1, GLA_CT, LANE), lambda b, t: (b, t, F32_OFF["ba"] // LANE)),
            pl.BlockSpec((GATE_RANK, B_KEY_WIDTH), lambda b, t: (0, 0)),
            pl.BlockSpec((1, B_KEY_WIDTH), lambda b, t: (0, 0)),
            pl.BlockSpec((1, B_DV), lambda b, t: (0, 0)),
        ],
        out_specs=pl.BlockSpec((1, GLA_CT, B_WIDTH), lambda b, t: (b, t, 0)),
        out_shape=jax.ShapeDtypeStruct((BATCH, SEQ, B_WIDTH), jnp.bfloat16),
        scratch_shapes=[pltpu.VMEM((B_HEADS, B_DV, B_DK), jnp.float32)],
        compiler_params=pltpu.CompilerParams(
            dimension_semantics=("arbitrary", "arbitrary"), vmem_limit_bytes=VMEM_LIMIT),
        name="gla",
    )(proj_f, proj_f, proj_f, proj_f, proj_f, w_alpha_up, b_alpha.reshape(1, B_KEY_WIDTH),
      gla_g.reshape(1, B_DV))


OUT_TM = 512


def _out_kernel(a_ref, b_ref, wa_ref, wb_ref, x_ref, gate_ref, fg_ref, o_ref, *, final_norm):
    y = (jnp.dot(a_ref[...], wa_ref[...], preferred_element_type=jnp.float32)
         + jnp.dot(b_ref[...], wb_ref[...], preferred_element_type=jnp.float32))
    xn = x_ref[...] + gate_ref[0] * y
    if final_norm:
        r = xn * lax.rsqrt(jnp.mean(xn * xn, axis=-1, keepdims=True) + EPS)
        xn = r * fg_ref[...]
    o_ref[...] = xn


def _out_proj(a_out, b_out, w_out_bf, x2d, gate, final_g, layer, final_norm):
    m = x2d.shape[0]
    tiles_per_batch = SEQ // OUT_TM
    return pl.pallas_call(
        functools.partial(_out_kernel, final_norm=final_norm),
        grid=(m // OUT_TM,),
        in_specs=[
            pl.BlockSpec((OUT_TM, A_WIDTH), lambda i: (i, 0)),
            pl.BlockSpec((OUT_TM, B_WIDTH), lambda i: (i, 0)),
            pl.BlockSpec((None, A_WIDTH, D_MODEL), lambda i: (layer, 0, 0)),
            pl.BlockSpec((None, B_WIDTH, D_MODEL), lambda i: (layer, 1, 0)),
            pl.BlockSpec((OUT_TM, D_MODEL), lambda i: (i, 0)),
            pl.BlockSpec((1, 1, D_MODEL), lambda i: (i // tiles_per_batch, 0, 0)),
            pl.BlockSpec((1, D_MODEL), lambda i: (0, 0)),
        ],
        out_specs=pl.BlockSpec((OUT_TM, D_MODEL), lambda i: (i, 0)),
        out_shape=jax.ShapeDtypeStruct((m, D_MODEL), jnp.float32),
        compiler_params=pltpu.CompilerParams(
            dimension_semantics=("arbitrary",), vmem_limit_bytes=VMEM_LIMIT),
        name="out_proj",
    )(a_out, b_out, w_out_bf, w_out_bf, x2d, gate, final_g.reshape(1, D_MODEL))


def kernel(x, c, w_ada, b_ada, norm_g, w_in, w_alpha_up, b_alpha, gla_g, w_out, rel_bias, final_g):
    mod = _adaln_mod(c, w_ada, b_ada)
    band = _bias_band(rel_bias)
    w_in_p = _pack_w_in(jnp.swapaxes(w_in, 1, 2))
    w_out_bf = w_out.astype(jnp.bfloat16)
    x2d = x.reshape(BATCH * SEQ, D_MODEL)
    for l in range(DEPTH):
        shift = mod[l, :, 0:D_MODEL].reshape(BATCH, 1, D_MODEL)
        scale = mod[l, :, D_MODEL:2 * D_MODEL].reshape(BATCH, 1, D_MODEL)
        gate = mod[l, :, 2 * D_MODEL:].reshape(BATCH, 1, D_MODEL)
        proj_b, proj_f = _norm_proj(x2d, shift, scale, norm_g[l], w_in_p, l)
        proj_b = proj_b.reshape(BATCH, SEQ, NB_COLS)
        proj_f = proj_f.reshape(BATCH, SEQ, NF_COLS)
        a_out = _dsa_attention(proj_b, proj_f, band)
        b_out = _gla(proj_f, w_alpha_up[l], b_alpha[l], gla_g[l])
        x2d = _out_proj(a_out.reshape(BATCH * SEQ, A_WIDTH), b_out.reshape(BATCH * SEQ, B_WIDTH),
                        w_out_bf, x2d, gate, final_g, l, final_norm=(l == DEPTH - 1))
    return x2d.reshape(BATCH, SEQ, D_MODEL)
```

```python
import functools
import math

import numpy as np
import jax
import jax.numpy as jnp
from jax import lax
from jax.experimental import pallas as pl
from jax.experimental.pallas import tpu as pltpu

D_MODEL = 2048
BATCH = 4
SEQ = 2048
DEPTH = 4
A_WIDTH = 1024
A_HEADS = 8
A_HEAD_DIM = 128
IDX_HEADS = 16
IDX_DIM = 64
TOPK = min(256, SEQ // 4)
B_WIDTH = 1024
B_HEADS = 4
B_KEY_WIDTH = 512
B_DK = 128
B_DV = 256
GATE_RANK = 16
GATE_TEMP = 16.0
NUM_BUCKETS = 32
MAX_DISTANCE = 128
EPS = 1e-6

IN_WIDTHS = (A_WIDTH, A_WIDTH, A_WIDTH, A_WIDTH, IDX_HEADS * IDX_DIM, IDX_DIM, IDX_HEADS,
             B_KEY_WIDTH, B_KEY_WIDTH, B_WIDTH, B_WIDTH, GATE_RANK)
IN_NAMES = ("aq", "ak", "av", "ag", "iq", "ik", "iw", "bq", "bk", "bv", "bg", "ba")
IN_OFFSETS = dict(zip(IN_NAMES, np.concatenate([[0], np.cumsum(IN_WIDTHS)[:-1]]).tolist()))
IN_WIDTH_OF = dict(zip(IN_NAMES, IN_WIDTHS))
IN_COLS = sum(IN_WIDTHS)

LANE = 128
VMEM_LIMIT = 52 * 1024 * 1024

PROJ_TN = 512
BF_SEGS = (("aq", 1024), ("ak", 1024), ("av", 1024))
F32_SEGS = (("ag", 1024), ("iq", 1024), ("bv", 1024), ("bg", 1024), ("bq", 512), ("bk", 512),
            ("ik", LANE), ("iw", LANE), ("ba", LANE))
NB_COLS = sum(w for _, w in BF_SEGS)
NF_USED = sum(w for _, w in F32_SEGS)
NF_COLS = -(-NF_USED // PROJ_TN) * PROJ_TN
NP_COLS = NB_COLS + NF_COLS


def _seg_offsets(segs):
    offs, o = {}, 0
    for name, w in segs:
        offs[name] = o
        o += w
    return offs


BF_OFF = _seg_offsets(BF_SEGS)
F32_OFF = _seg_offsets(F32_SEGS)

INT_MIN = -2 ** 31
NEG_BIG = -1e30
LOG2E = math.log2(math.e)
I16_MIN = -2 ** 15
PACK16 = 16


PACK_COLS = 256


def _pack_kernel(w_ref, o_ref):
    dst = 0
    for name, width in BF_SEGS + F32_SEGS:
        src, used = IN_OFFSETS[name], IN_WIDTH_OF[name]
        o_ref[0, dst:dst + used, :] = w_ref[0, src:src + used, :].astype(jnp.bfloat16)
        if width > used:
            o_ref[0, dst + used:dst + width, :] = jnp.zeros((width - used, PACK_COLS), jnp.bfloat16)
        dst += width
    if dst < NP_COLS:
        o_ref[0, dst:, :] = jnp.zeros((NP_COLS - dst, PACK_COLS), jnp.bfloat16)


def _pack_w_in(w_in_t):
    return pl.pallas_call(
        _pack_kernel,
        grid=(DEPTH, D_MODEL // PACK_COLS),
        in_specs=[pl.BlockSpec((1, IN_COLS, PACK_COLS), lambda l, c: (l, 0, c))],
        out_specs=pl.BlockSpec((1, NP_COLS, PACK_COLS), lambda l, c: (l, 0, c)),
        out_shape=jax.ShapeDtypeStruct((DEPTH, NP_COLS, D_MODEL), jnp.bfloat16),
        compiler_params=pltpu.CompilerParams(
            dimension_semantics=("arbitrary", "arbitrary"), vmem_limit_bytes=VMEM_LIMIT),
        name="pack_w_in",
    )(w_in_t)


MOD_TN = 768


def _mod_kernel(c_ref, w_ref, b_ref, o_ref):
    c = c_ref[...]
    c_act = c * jax.nn.sigmoid(c)
    acc = jnp.dot(c_act.astype(jnp.bfloat16), w_ref[0].astype(jnp.bfloat16),
                  preferred_element_type=jnp.float32)
    o_ref[0] = acc + b_ref[0]


def _adaln_mod(c, w_ada, b_ada):
    cp = jnp.pad(c, ((0, 8 - BATCH), (0, 0)))
    out = pl.pallas_call(
        _mod_kernel,
        grid=(DEPTH, 3 * D_MODEL // MOD_TN),
        in_specs=[
            pl.BlockSpec((8, D_MODEL), lambda l, j: (0, 0)),
            pl.BlockSpec((1, D_MODEL, MOD_TN), lambda l, j: (l, 0, j)),
            pl.BlockSpec((1, 1, MOD_TN), lambda l, j: (l, 0, j)),
        ],
        out_specs=pl.BlockSpec((1, 8, MOD_TN), lambda l, j: (l, 0, j)),
        out_shape=jax.ShapeDtypeStruct((DEPTH, 8, 3 * D_MODEL), jnp.float32),
        compiler_params=pltpu.CompilerParams(
            dimension_semantics=("arbitrary", "arbitrary"), vmem_limit_bytes=VMEM_LIMIT),
        name="adaln_mod",
    )(cp, w_ada, b_ada.reshape(DEPTH, 1, 3 * D_MODEL))
    return out[:, :BATCH]


PROJ_TM = 1024
NB_TILES = NB_COLS // PROJ_TN


def _proj_kernel(x_ref, shift_ref, scale_ref, g_ref, w_ref, ob_ref, of_ref, h_scr):
    j = pl.program_id(1)

    @pl.when(j == 0)
    def _():
        x = x_ref[...]
        r = x * lax.rsqrt(jnp.mean(x * x, axis=-1, keepdims=True) + EPS)
        h = (r * g_ref[...]) * (1.0 + scale_ref[0]) + shift_ref[0]
        h_scr[...] = h.astype(jnp.bfloat16)

    @pl.when(j < NB_TILES)
    def _():
        ob_ref[...] = _dot_nt(h_scr[...], w_ref[...]).astype(jnp.bfloat16)

    @pl.when(j >= NB_TILES)
    def _():
        of_ref[...] = _dot_nt(h_scr[...], w_ref[...])


def _norm_proj(x2d, shift, scale, norm_g, w_p, layer):
    m = x2d.shape[0]
    tiles_per_batch = SEQ // PROJ_TM
    return pl.pallas_call(
        _proj_kernel,
        grid=(m // PROJ_TM, NP_COLS // PROJ_TN),
        in_specs=[
            pl.BlockSpec((PROJ_TM, D_MODEL), lambda i, j: (i, 0)),
            pl.BlockSpec((1, 1, D_MODEL), lambda i, j: (i // tiles_per_batch, 0, 0)),
            pl.BlockSpec((1, 1, D_MODEL), lambda i, j: (i // tiles_per_batch, 0, 0)),
            pl.BlockSpec((1, D_MODEL), lambda i, j: (0, 0)),
            pl.BlockSpec((None, PROJ_TN, D_MODEL), lambda i, j: (layer, j, 0)),
        ],
        out_specs=[
            pl.BlockSpec((PROJ_TM, PROJ_TN), lambda i, j: (i, jnp.minimum(j, NB_TILES - 1))),
            pl.BlockSpec((PROJ_TM, PROJ_TN), lambda i, j: (i, jnp.maximum(j - NB_TILES, 0))),
        ],
        out_shape=[
            jax.ShapeDtypeStruct((m, NB_COLS), jnp.bfloat16),
            jax.ShapeDtypeStruct((m, NF_COLS), jnp.float32),
        ],
        scratch_shapes=[pltpu.VMEM((PROJ_TM, D_MODEL), jnp.bfloat16)],
        compiler_params=pltpu.CompilerParams(
            dimension_semantics=("arbitrary", "arbitrary"), vmem_limit_bytes=VMEM_LIMIT),
        name="norm_proj",
    )(x2d, shift, scale, norm_g.reshape(1, D_MODEL), w_p)


TQ = 256
KB = 256
NQ = SEQ // TQ
NKB = SEQ // KB
HALF = KB // 2


def _t5_bucket_np(d):
    max_exact = NUM_BUCKETS // 2
    d = np.maximum(d, 0)
    df = np.maximum(d, 1).astype(np.float32)
    large = max_exact + (np.log(df / np.float32(max_exact)) / np.float32(math.log(MAX_DISTANCE / max_exact))
                         * np.float32(NUM_BUCKETS - max_exact)).astype(np.int32)
    large = np.minimum(large, NUM_BUCKETS - 1)
    return np.where(d < max_exact, d, large).astype(np.int32)


assert int(_t5_bucket_np(np.arange(KB, 2 * SEQ)).min()) == NUM_BUCKETS - 1


def _band_buckets():
    tl = np.arange(TQ)[:, None]
    u = np.arange(KB)[None, :]
    prev = _t5_bucket_np(KB + tl - u)
    diag = _t5_bucket_np(tl - u)
    return np.stack([prev, diag]).astype(np.int32)


BAND_ROWS = 64


def _band_kernel(rb_ref, bucket_ref, o_ref):
    h = pl.program_id(0)
    far = rb_ref[NUM_BUCKETS - 1, h]

    def rows(r, carry):
        rs = pl.ds(pl.multiple_of(r * BAND_ROWS, BAND_ROWS), BAND_ROWS)
        bucket = bucket_ref[0, rs, :]
        acc = jnp.zeros(bucket.shape, jnp.float32)
        for b in range(NUM_BUCKETS):
            acc = jnp.where(bucket == b, rb_ref[b, h] - far, acc)
        o_ref[0, 0, rs, :] = acc * LOG2E
        return carry

    lax.fori_loop(0, TQ // BAND_ROWS, rows, 0)


def _bias_band(rel_bias):
    return pl.pallas_call(
        _band_kernel,
        grid=(A_HEADS, 2),
        in_specs=[
            pl.BlockSpec(memory_space=pltpu.SMEM),
            pl.BlockSpec((1, TQ, KB), lambda h, s: (s, 0, 0)),
        ],
        out_specs=pl.BlockSpec((1, 1, TQ, KB), lambda h, s: (h, s, 0, 0)),
        out_shape=jax.ShapeDtypeStruct((A_HEADS, 2, TQ, KB), jnp.float32),
        name="bias_band",
    )(rel_bias, jnp.asarray(_band_buckets()))


def _dot_nt(a, b):
    return lax.dot_general(a, b, (((1,), (1,)), ((), ())), preferred_element_type=jnp.float32)


def _dot_tn(a, b):
    return lax.dot_general(a, b, (((0,), (0,)), ((), ())), preferred_element_type=jnp.float32)


def _dsa_kernel(q_ref, k_ref, v_ref, iq_ref, ik_ref, iw_ref, ag_ref, band_ref, o_ref,
                iklo_scr, ikhi_scr, vext_scr, wb_scr, iq2_scr, key_scr, khi_scr, klo_scr, thrn_scr,
                cutn_scr, madd_scr, m_scr, accv_scr, accl_scr):
    qi = pl.program_id(1)
    nkb = qi + 1
    bf16 = jnp.bfloat16

    def key_rows(kb):
        return pl.ds(pl.multiple_of(kb * KB, KB), KB)

    @pl.when(qi == 0)
    def _():
        def prep(c, carry):
            rs = key_rows(c)
            ik = ik_ref[0, rs, :]
            iklo_scr[rs, :] = ik.astype(bf16)
            ikhi_scr[rs, :] = pltpu.roll(ik, IDX_DIM, axis=1).astype(bf16)
            for h in range(A_HEADS):
                vext_scr[h, rs, :A_HEAD_DIM] = v_ref[0, rs, h * A_HEAD_DIM:(h + 1) * A_HEAD_DIM]
                vext_scr[h, rs, A_HEAD_DIM:] = jnp.ones((KB, A_HEAD_DIM), bf16)
            return carry

        lax.fori_loop(0, NKB, prep, 0)

    iw = iw_ref[0][:, :IDX_HEADS] * (IDX_HEADS ** -0.5 * IDX_DIM ** -0.5)
    for j in range(IDX_HEADS):
        wb_scr[j] = jnp.broadcast_to(iw[:, j:j + 1], (TQ, HALF))
    for jp in range(IDX_HEADS // 2):
        iq2_scr[jp * TQ:(jp + 1) * TQ, :] = iq_ref[0, :, jp * LANE:(jp + 1) * LANE].astype(bf16)

    row = lax.broadcasted_iota(jnp.int32, (TQ, HALF), 0)
    col = lax.broadcasted_iota(jnp.int32, (TQ, HALF), 1)

    def score_block(kb, carry):
        rs = key_rows(kb)
        s_lo = _dot_nt(iq2_scr[...], iklo_scr[rs, :])
        s_hi = _dot_nt(iq2_scr[...], ikhi_scr[rs, :])
        for half in range(2):
            ls = slice(half * HALF, (half + 1) * HALF)
            acc = jnp.zeros((TQ, HALF), jnp.float32)
            for jp in range(IDX_HEADS // 2):
                rj = slice(jp * TQ, (jp + 1) * TQ)
                acc = (acc + jnp.maximum(s_lo[rj, ls], 0.0) * wb_scr[2 * jp]
                       + jnp.maximum(s_hi[rj, ls], 0.0) * wb_scr[2 * jp + 1])
            bits = pltpu.bitcast(acc + 0.0, jnp.int32)
            key = jnp.where(bits >= 0, bits, bits ^ 0x7FFFFFFF)
            causal = (kb * KB + half * HALF + col) <= (qi * TQ + row)
            key_scr[kb, :, ls] = jnp.where(causal, key, INT_MIN)
        keyt = key_scr[kb].T
        khi_scr[kb] = lax.shift_right_arithmetic(keyt, 16).astype(jnp.int16)
        klo_scr[kb] = ((keyt & 0xFFFF) + I16_MIN).astype(jnp.int16)
        return carry

    lax.fori_loop(0, nkb, score_block, 0)

    def count(ref, cand, strictly=False):
        c16 = jnp.broadcast_to(cand.astype(jnp.int16), (PACK16, TQ))

        def body(kb, cnt):
            v = ref[kb]
            for r in range(KB // PACK16):
                blk = v[r * PACK16:(r + 1) * PACK16]
                hit = (blk > c16) if strictly else (blk >= c16)
                cnt = cnt + jnp.where(hit, jnp.int16(1), jnp.int16(0))
            return cnt

        cnt = lax.fori_loop(0, nkb, body, jnp.zeros((PACK16, TQ), jnp.int16))
        return jnp.sum(cnt.astype(jnp.int32), axis=0, keepdims=True)

    def bisect16(ref, need):
        zero = jnp.zeros((1, TQ), jnp.int32)
        prefix = jnp.where(count(ref, zero) >= need, zero, I16_MIN)

        def bit_pass(it, prefix):
            cand = prefix + lax.shift_left(jnp.int32(1), 14 - it)
            return jnp.where(count(ref, cand) >= need, cand, prefix)

        return lax.fori_loop(0, 15, bit_pass, prefix)

    topk = jnp.full((1, TQ), TOPK, jnp.int32)
    p_hi = bisect16(khi_scr, topk)
    above = count(khi_scr, p_hi, strictly=True)
    p_hi16 = jnp.broadcast_to(p_hi.astype(jnp.int16), (PACK16, TQ))

    def narrow(kb, carry):
        hi, lo = khi_scr[kb], klo_scr[kb]
        for r in range(KB // PACK16):
            rs = slice(r * PACK16, (r + 1) * PACK16)
            klo_scr[kb, rs, :] = jnp.where(hi[rs] == p_hi16, lo[rs], jnp.int16(I16_MIN))
        return carry

    lax.fori_loop(0, nkb, narrow, 0)
    p_lo = bisect16(klo_scr, topk - above)
    prefix = p_hi * 65536 + (p_lo - I16_MIN)
    thr = jnp.maximum(prefix, INT_MIN + 1)
    thrn_scr[...] = jnp.broadcast_to(thr, (KB, TQ)).T

    p_lo16 = jnp.broadcast_to(p_lo.astype(jnp.int16), (PACK16, TQ))
    sub16 = lax.broadcasted_iota(jnp.int32, (PACK16, TQ), 0)

    def tie_blocks(kb, fn, carry):
        hi, lo = khi_scr[kb], klo_scr[kb]
        for r in range(KB // PACK16):
            rs = slice(r * PACK16, (r + 1) * PACK16)
            carry = fn(kb, r, rs, (hi[rs] == p_hi16) & (lo[rs] == p_lo16), carry)
        return carry

    def count_tied(kb, cnt):
        return tie_blocks(kb, lambda kb, r, rs, tied, c: c + jnp.where(tied, jnp.int16(1), jnp.int16(0)), cnt)

    tied = lax.fori_loop(0, nkb, count_tied, jnp.zeros((PACK16, TQ), jnp.int16))
    tied = jnp.sum(tied.astype(jnp.int32), axis=0, keepdims=True)
    room = topk - above - count(klo_scr, p_lo, strictly=True)
    trim = (tied > room) & (prefix != INT_MIN)

    def madd_block(kb, carry):
        madd_scr[kb] = jnp.where(key_scr[kb] >= thrn_scr[...], 0.0, NEG_BIG)
        return carry

    lax.fori_loop(0, nkb, madd_block, 0)

    @pl.when(jnp.max(trim.astype(jnp.int32)) > 0)
    def _():
        def mark(kb, carry):
            def put(kb, r, rs, tied, c):
                neg_pos = (-(kb * KB + r * PACK16) - sub16).astype(jnp.int16)
                khi_scr[kb, rs, :] = jnp.where(tied, neg_pos, jnp.int16(I16_MIN))
                return c
            return tie_blocks(kb, put, carry)

        lax.fori_loop(0, nkb, mark, 0)
        cut = jnp.where(trim, -bisect16(khi_scr, room), SEQ)
        cutn_scr[...] = jnp.broadcast_to(cut, (KB, TQ)).T

        def madd_ties(kb, carry):
            key = key_scr[kb]
            pos = kb * KB + lax.broadcasted_iota(jnp.int32, (TQ, KB), 1)
            keep_tie = jnp.where(pos <= cutn_scr[...], 0.0, NEG_BIG)
            madd_scr[kb] = jnp.where(key > thrn_scr[...], 0.0,
                                     jnp.where(key == thrn_scr[...], keep_tie, NEG_BIG))
            return carry

        lax.fori_loop(0, nkb, madd_ties, 0)

    m_scr[...] = jnp.full(m_scr.shape, NEG_BIG, jnp.float32)
    accv_scr[...] = jnp.zeros_like(accv_scr)
    accl_scr[...] = jnp.zeros_like(accl_scr)
    scale2 = A_HEAD_DIM ** -0.5 * LOG2E

    def attend(kb, slot):
        rs = key_rows(kb)
        for h in range(A_HEADS):
            hs = slice(h * A_HEAD_DIM, (h + 1) * A_HEAD_DIM)
            lg = _dot_nt(q_ref[0, :, hs], k_ref[0, rs, hs]) * scale2 + madd_scr[kb]
            if slot is not None:
                lg = lg + band_ref[h, slot]
            m_old = m_scr[h]
            m_new = jnp.maximum(m_old, jnp.max(lg, axis=-1, keepdims=True))
            alpha = jnp.exp2(m_old - m_new)
            p = jnp.concatenate([jnp.exp2(lg[:, :HALF] - m_new), jnp.exp2(lg[:, HALF:] - m_new)], axis=1)
            pv = jnp.dot(p.astype(bf16), vext_scr[h, rs, :], preferred_element_type=jnp.float32)
            accv_scr[h] = accv_scr[h] * alpha + pv[:, :A_HEAD_DIM]
            accl_scr[h] = accl_scr[h] * alpha + pv[:, A_HEAD_DIM:]
            m_scr[h] = m_new

    def far_block(kb, carry):
        attend(kb, None)
        return carry

    lax.fori_loop(0, qi - 1, far_block, 0)

    @pl.when(qi >= 1)
    def _():
        attend(qi - 1, 0)

    attend(qi, 1)

    for h in range(A_HEADS):
        hs = slice(h * A_HEAD_DIM, (h + 1) * A_HEAD_DIM)
        g = ag_ref[0, :, hs]
        o_ref[0, :, hs] = (accv_scr[h] / accl_scr[h] * (g * jax.nn.sigmoid(g))).astype(o_ref.dtype)


def _dsa_attention(proj_b, proj_f, band):
    cb = lambda name: BF_OFF[name] // A_WIDTH
    once = pl.Buffered(1)
    return pl.pallas_call(
        _dsa_kernel,
        grid=(BATCH, NQ),
        in_specs=[
            pl.BlockSpec((1, TQ, A_WIDTH), lambda b, i: (b, i, cb("aq"))),
            pl.BlockSpec((1, SEQ, A_WIDTH), lambda b, i: (b, 0, cb("ak")), pipeline_mode=once),
            pl.BlockSpec((1, SEQ, A_WIDTH), lambda b, i: (b, 0, cb("av")), pipeline_mode=once),
            pl.BlockSpec((1, TQ, A_WIDTH), lambda b, i: (b, i, F32_OFF["iq"] // A_WIDTH)),
            pl.BlockSpec((1, SEQ, LANE), lambda b, i: (b, 0, F32_OFF["ik"] // LANE), pipeline_mode=once),
            pl.BlockSpec((1, TQ, LANE), lambda b, i: (b, i, F32_OFF["iw"] // LANE)),
            pl.BlockSpec((1, TQ, A_WIDTH), lambda b, i: (b, i, F32_OFF["ag"] // A_WIDTH)),
            pl.BlockSpec((A_HEADS, 2, TQ, KB), lambda b, i: (0, 0, 0, 0), pipeline_mode=once),
        ],
        out_specs=pl.BlockSpec((1, TQ, A_WIDTH), lambda b, i: (b, i, 0)),
        out_shape=jax.ShapeDtypeStruct((BATCH, SEQ, A_WIDTH), jnp.bfloat16),
        scratch_shapes=[
            pltpu.VMEM((SEQ, LANE), jnp.bfloat16),
            pltpu.VMEM((SEQ, LANE), jnp.bfloat16),
            pltpu.VMEM((A_HEADS, SEQ, 2 * A_HEAD_DIM), jnp.bfloat16),
            pltpu.VMEM((IDX_HEADS, TQ, HALF), jnp.float32),
            pltpu.VMEM((IDX_HEADS // 2 * TQ, LANE), jnp.bfloat16),
            pltpu.VMEM((NKB, TQ, KB), jnp.int32),
            pltpu.VMEM((NKB, KB, TQ), jnp.int16),
            pltpu.VMEM((NKB, KB, TQ), jnp.int16),
            pltpu.VMEM((TQ, KB), jnp.int32),
            pltpu.VMEM((TQ, KB), jnp.int32),
            pltpu.VMEM((NKB, TQ, KB), jnp.float32),
            pltpu.VMEM((A_HEADS, TQ, HALF), jnp.float32),
            pltpu.VMEM((A_HEADS, TQ, A_HEAD_DIM), jnp.float32),
            pltpu.VMEM((A_HEADS, TQ, A_HEAD_DIM), jnp.float32),
        ],
        compiler_params=pltpu.CompilerParams(
            dimension_semantics=("arbitrary", "arbitrary"), vmem_limit_bytes=VMEM_LIMIT),
        name="dsa_attention",
    )(proj_b, proj_b, proj_b, proj_f, proj_f, proj_f, proj_f, band)


GLA_CT = 512
GLA_C = 64
GLA_SC = 256
GLA_NCS = GLA_SC // GLA_C


def _gla_kernel(bq_ref, bk_ref, bv_ref, bg_ref, ba_ref, wup_ref, balpha_ref, g_ref, o_ref, st_scr):
    @pl.when(pl.program_id(1) == 0)
    def _():
        st_scr[...] = jnp.zeros_like(st_scr)

    bf16 = jnp.bfloat16
    rr = lax.broadcasted_iota(jnp.int32, (GLA_SC, GLA_SC), 0)
    cc = lax.broadcasted_iota(jnp.int32, (GLA_SC, GLA_SC), 1)
    tri = (rr >= cc) & (rr // GLA_C == cc // GLA_C)
    tri_bf = tri.astype(bf16)
    er = lax.broadcasted_iota(jnp.int32, (GLA_SC, GLA_NCS * B_DK), 0)
    ec = lax.broadcasted_iota(jnp.int32, (GLA_SC, GLA_NCS * B_DK), 1)
    own_block = (er // GLA_C) == (ec // B_DK)
    wup = wup_ref[...].astype(bf16)
    balpha = balpha_ref[...]
    gain = g_ref[...]
    heads = range(B_HEADS)

    def expand(a):
        return jnp.where(own_block, jnp.concatenate([a] * GLA_NCS, axis=1), jnp.zeros((), a.dtype))

    def body(sc, carry):
        rows = pl.ds(pl.multiple_of(sc * GLA_SC, GLA_SC), GLA_SC)
        ba = ba_ref[0, rows, :][:, :GATE_RANK].astype(bf16)
        pre = jnp.dot(ba, wup, preferred_element_type=jnp.float32) + balpha
        log_a = (jnp.minimum(pre, 0.0) - jnp.log1p(jnp.exp(-jnp.abs(pre)))) * (1.0 / GATE_TEMP)
        la_hi = log_a.astype(bf16)
        la_lo = (log_a - la_hi.astype(jnp.float32)).astype(bf16)
        bcum = (jnp.dot(tri_bf, la_hi, preferred_element_type=jnp.float32)
                + jnp.dot(tri_bf, la_lo, preferred_element_type=jnp.float32))
        ks = [slice(h * B_DK, (h + 1) * B_DK) for h in heads]
        vs = [slice(h * B_DV, (h + 1) * B_DV) for h in heads]
        qe, ke, kd, dec, v = [], [], [], [], []
        for h in heads:
            b = bcum[:, ks[h]]
            b_end = b.reshape(GLA_NCS, GLA_C, B_DK)[:, GLA_C - 1:GLA_C, :]
            b_end_rows = jnp.broadcast_to(b_end, (GLA_NCS, GLA_C, B_DK)).reshape(GLA_SC, B_DK)
            q = bq_ref[0, rows, ks[h]] * (B_DK ** -0.5)
            k = bk_ref[0, rows, ks[h]]
            qe.append((q * jnp.exp(b)).astype(bf16))
            ke.append((k * jnp.exp(-b)).astype(bf16))
            kd.append((k * jnp.exp(b_end_rows - b)).astype(bf16))
            dec.append(jnp.exp(b_end))
            v.append(bv_ref[0, rows, vs[h]].astype(bf16))
        attn = [jnp.where(tri, _dot_nt(qe[h], ke[h]), 0.0).astype(bf16) for h in heads]
        o_intra = [jnp.dot(attn[h], v[h], preferred_element_type=jnp.float32) for h in heads]
        upd = [_dot_tn(v[h], expand(kd[h])) for h in heads]
        o_inter = []
        for h in heads:
            st = st_scr[h]
            states = []
            for c in range(GLA_NCS):
                states.append(st.astype(bf16))
                st = st * dec[h][c] + upd[h][:, c * B_DK:(c + 1) * B_DK]
            st_scr[h] = st
            o_inter.append(_dot_nt(expand(qe[h]), jnp.concatenate(states, axis=1)))
        for h in heads:
            o = o_intra[h] + o_inter[h]
            on = o * lax.rsqrt(jnp.mean(o * o, axis=-1, keepdims=True) + EPS) * gain
            g = bg_ref[0, rows, vs[h]]
            o_ref[0, rows, vs[h]] = (on * (g * jax.nn.sigmoid(g))).astype(o_ref.dtype)
        return carry

    lax.fori_loop(0, GLA_CT // GLA_SC, body, 0)


def _gla(proj_f, w_alpha_up, b_alpha, gla_g):
    return pl.pallas_call(
        _gla_kernel,
        grid=(BATCH, SEQ // GLA_CT),
        in_specs=[
            pl.BlockSpec((1, GLA_CT, B_KEY_WIDTH), lambda b, t: (b, t, F32_OFF["bq"] // B_KEY_WIDTH)),
            pl.BlockSpec((1, GLA_CT, B_KEY_WIDTH), lambda b, t: (b, t, F32_OFF["bk"] // B_KEY_WIDTH)),
            pl.BlockSpec((1, GLA_CT, B_WIDTH), lambda b, t: (b, t, F32_OFF["bv"] // B_WIDTH)),
            pl.BlockSpec((1, GLA_CT, B_WIDTH), lambda b, t: (b, t, F32_OFF["bg"] // B_WIDTH)),
            pl.BlockSpec((1, GLA_CT, LANE), lambda b, t: (b, t, F32_OFF["ba"] // LANE)),
            pl.BlockSpec((GATE_RANK, B_KEY_WIDTH), lambda b, t: (0, 0)),
            pl.BlockSpec((1, B_KEY_WIDTH), lambda b, t: (0, 0)),
            pl.BlockSpec((1, B_DV), lambda b, t: (0, 0)),
        ],
        out_specs=pl.BlockSpec((1, GLA_CT, B_WIDTH), lambda b, t: (b, t, 0)),
        out_shape=jax.ShapeDtypeStruct((BATCH, SEQ, B_WIDTH), jnp.bfloat16),
        scratch_shapes=[pltpu.VMEM((B_HEADS, B_DV, B_DK), jnp.float32)],
        compiler_params=pltpu.CompilerParams(
            dimension_semantics=("arbitrary", "arbitrary"), vmem_limit_bytes=VMEM_LIMIT),
        name="gla",
    )(proj_f, proj_f, proj_f, proj_f, proj_f, w_alpha_up, b_alpha.reshape(1, B_KEY_WIDTH),
      gla_g.reshape(1, B_DV))


OUT_TM = 512


def _out_kernel(a_ref, b_ref, wa_ref, wb_ref, x_ref, gate_ref, fg_ref, o_ref, *, final_norm):
    y = (jnp.dot(a_ref[...], wa_ref[...], preferred_element_type=jnp.float32)
         + jnp.dot(b_ref[...], wb_ref[...], preferred_element_type=jnp.float32))
    xn = x_ref[...] + gate_ref[0] * y
    if final_norm:
        r = xn * lax.rsqrt(jnp.mean(xn * xn, axis=-1, keepdims=True) + EPS)
        xn = r * fg_ref[...]
    o_ref[...] = xn


def _out_proj(a_out, b_out, w_out_bf, x2d, gate, final_g, layer, final_norm):
    m = x2d.shape[0]
    tiles_per_batch = SEQ // OUT_TM
    return pl.pallas_call(
        functools.partial(_out_kernel, final_norm=final_norm),
        grid=(m // OUT_TM,),
        in_specs=[
            pl.BlockSpec((OUT_TM, A_WIDTH), lambda i: (i, 0)),
            pl.BlockSpec((OUT_TM, B_WIDTH), lambda i: (i, 0)),
            pl.BlockSpec((None, A_WIDTH, D_MODEL), lambda i: (layer, 0, 0)),
            pl.BlockSpec((None, B_WIDTH, D_MODEL), lambda i: (layer, 1, 0)),
            pl.BlockSpec((OUT_TM, D_MODEL), lambda i: (i, 0)),
            pl.BlockSpec((1, 1, D_MODEL), lambda i: (i // tiles_per_batch, 0, 0)),
            pl.BlockSpec((1, D_MODEL), lambda i: (0, 0)),
        ],
        out_specs=pl.BlockSpec((OUT_TM, D_MODEL), lambda i: (i, 0)),
        out_shape=jax.ShapeDtypeStruct((m, D_MODEL), jnp.float32),
        compiler_params=pltpu.CompilerParams(
            dimension_semantics=("arbitrary",), vmem_limit_bytes=VMEM_LIMIT),
        name="out_proj",
    )(a_out, b_out, w_out_bf, w_out_bf, x2d, gate, final_g.reshape(1, D_MODEL))


def kernel(x, c, w_ada, b_ada, norm_g, w_in, w_alpha_up, b_alpha, gla_g, w_out, rel_bias, final_g):
    mod = _adaln_mod(c, w_ada, b_ada)
    band = _bias_band(rel_bias)
    w_in_p = _pack_w_in(jnp.swapaxes(w_in, 1, 2))
    w_out_bf = w_out.astype(jnp.bfloat16)
    x2d = x.reshape(BATCH * SEQ, D_MODEL)
    for l in range(DEPTH):
        shift = mod[l, :, 0:D_MODEL].reshape(BATCH, 1, D_MODEL)
        scale = mod[l, :, D_MODEL:2 * D_MODEL].reshape(BATCH, 1, D_MODEL)
        gate = mod[l, :, 2 * D_MODEL:].reshape(BATCH, 1, D_MODEL)
        proj_b, proj_f = _norm_proj(x2d, shift, scale, norm_g[l], w_in_p, l)
        proj_b = proj_b.reshape(BATCH, SEQ, NB_COLS)
        proj_f = proj_f.reshape(BATCH, SEQ, NF_COLS)
        a_out = _dsa_attention(proj_b, proj_f, band)
        b_out = _gla(proj_f, w_alpha_up[l], b_alpha[l], gla_g[l])
        x2d = _out_proj(a_out.reshape(BATCH * SEQ, A_WIDTH), b_out.reshape(BATCH * SEQ, B_WIDTH),
                        w_out_bf, x2d, gate, final_g, l, final_norm=(l == DEPTH - 1))
    return x2d.reshape(BATCH, SEQ, D_MODEL)
```

```python
import functools
import math

import numpy as np
import jax
import jax.numpy as jnp
from jax import lax
from jax.experimental import pallas as pl
from jax.experimental.pallas import tpu as pltpu

D_MODEL = 2048
BATCH = 4
SEQ = 2048
DEPTH = 4
A_WIDTH = 1024
A_HEADS = 8
A_HEAD_DIM = 128
IDX_HEADS = 16
IDX_DIM = 64
TOPK = min(256, SEQ // 4)
B_WIDTH = 1024
B_HEADS = 4
B_KEY_WIDTH = 512
B_DK = 128
B_DV = 256
GATE_RANK = 16
GATE_TEMP = 16.0
NUM_BUCKETS = 32
MAX_DISTANCE = 128
EPS = 1e-6

IN_WIDTHS = (A_WIDTH, A_WIDTH, A_WIDTH, A_WIDTH, IDX_HEADS * IDX_DIM, IDX_DIM, IDX_HEADS,
             B_KEY_WIDTH, B_KEY_WIDTH, B_WIDTH, B_WIDTH, GATE_RANK)
IN_NAMES = ("aq", "ak", "av", "ag", "iq", "ik", "iw", "bq", "bk", "bv", "bg", "ba")
IN_OFFSETS = dict(zip(IN_NAMES, np.concatenate([[0], np.cumsum(IN_WIDTHS)[:-1]]).tolist()))
IN_WIDTH_OF = dict(zip(IN_NAMES, IN_WIDTHS))
IN_COLS = sum(IN_WIDTHS)

LANE = 128
VMEM_LIMIT = 52 * 1024 * 1024

PROJ_TN = 512
BF_SEGS = (("aq", 1024), ("ak", 1024), ("av", 1024))
F32_SEGS = (("ag", 1024), ("iq", 1024), ("bv", 1024), ("bg", 1024), ("bq", 512), ("bk", 512),
            ("ik", LANE), ("iw", LANE), ("ba", LANE))
NB_COLS = sum(w for _, w in BF_SEGS)
NF_USED = sum(w for _, w in F32_SEGS)
NF_COLS = -(-NF_USED // PROJ_TN) * PROJ_TN
NP_COLS = NB_COLS + NF_COLS


def _seg_offsets(segs):
    offs, o = {}, 0
    for name, w in segs:
        offs[name] = o
        o += w
    return offs


BF_OFF = _seg_offsets(BF_SEGS)
F32_OFF = _seg_offsets(F32_SEGS)

INT_MIN = -2 ** 31
NEG_BIG = -1e30
LOG2E = math.log2(math.e)
AQ_SCALE = A_HEAD_DIM ** -0.5 * LOG2E
I16_MIN = -2 ** 15
PACK16 = 16


PACK_COLS = 256


def _pack_kernel(w_ref, o_ref):
    dst = 0
    for name, width in BF_SEGS + F32_SEGS:
        src, used = IN_OFFSETS[name], IN_WIDTH_OF[name]
        o_ref[0, dst:dst + used, :] = w_ref[0, src:src + used, :].astype(jnp.bfloat16)
        if width > used:
            o_ref[0, dst + used:dst + width, :] = jnp.zeros((width - used, PACK_COLS), jnp.bfloat16)
        dst += width
    if dst < NP_COLS:
        o_ref[0, dst:, :] = jnp.zeros((NP_COLS - dst, PACK_COLS), jnp.bfloat16)


def _pack_w_in(w_in_t):
    return pl.pallas_call(
        _pack_kernel,
        grid=(DEPTH, D_MODEL // PACK_COLS),
        in_specs=[pl.BlockSpec((1, IN_COLS, PACK_COLS), lambda l, c: (l, 0, c))],
        out_specs=pl.BlockSpec((1, NP_COLS, PACK_COLS), lambda l, c: (l, 0, c)),
        out_shape=jax.ShapeDtypeStruct((DEPTH, NP_COLS, D_MODEL), jnp.bfloat16),
        compiler_params=pltpu.CompilerParams(
            dimension_semantics=("arbitrary", "arbitrary"), vmem_limit_bytes=VMEM_LIMIT),
        name="pack_w_in",
    )(w_in_t)


MOD_TN = 768


def _mod_kernel(c_ref, w_ref, b_ref, o_ref):
    c = c_ref[...]
    c_act = c * jax.nn.sigmoid(c)
    acc = jnp.dot(c_act.astype(jnp.bfloat16), w_ref[0].astype(jnp.bfloat16),
                  preferred_element_type=jnp.float32)
    o_ref[0] = acc + b_ref[0]


def _adaln_mod(c, w_ada, b_ada):
    cp = jnp.pad(c, ((0, 8 - BATCH), (0, 0)))
    out = pl.pallas_call(
        _mod_kernel,
        grid=(DEPTH, 3 * D_MODEL // MOD_TN),
        in_specs=[
            pl.BlockSpec((8, D_MODEL), lambda l, j: (0, 0)),
            pl.BlockSpec((1, D_MODEL, MOD_TN), lambda l, j: (l, 0, j)),
            pl.BlockSpec((1, 1, MOD_TN), lambda l, j: (l, 0, j)),
        ],
        out_specs=pl.BlockSpec((1, 8, MOD_TN), lambda l, j: (l, 0, j)),
        out_shape=jax.ShapeDtypeStruct((DEPTH, 8, 3 * D_MODEL), jnp.float32),
        compiler_params=pltpu.CompilerParams(
            dimension_semantics=("arbitrary", "arbitrary"), vmem_limit_bytes=VMEM_LIMIT),
        name="adaln_mod",
    )(cp, w_ada, b_ada.reshape(DEPTH, 1, 3 * D_MODEL))
    return out[:, :BATCH]


PROJ_TM = 1024
NB_TILES = NB_COLS // PROJ_TN


def _proj_kernel(x_ref, shift_ref, scale_ref, g_ref, w_ref, ob_ref, of_ref, h_scr):
    j = pl.program_id(1)

    @pl.when(j == 0)
    def _():
        x = x_ref[...]
        r = x * lax.rsqrt(jnp.mean(x * x, axis=-1, keepdims=True) + EPS)
        h = (r * g_ref[...]) * (1.0 + scale_ref[0]) + shift_ref[0]
        h_scr[...] = h.astype(jnp.bfloat16)

    @pl.when(j < NB_TILES)
    def _():
        out_scale = jnp.where(j < A_WIDTH // PROJ_TN, AQ_SCALE, 1.0)
        ob_ref[...] = (_dot_nt(h_scr[...], w_ref[...]) * out_scale).astype(jnp.bfloat16)

    @pl.when(j >= NB_TILES)
    def _():
        of_ref[...] = _dot_nt(h_scr[...], w_ref[...])


def _norm_proj(x2d, shift, scale, norm_g, w_p, layer):
    m = x2d.shape[0]
    tiles_per_batch = SEQ // PROJ_TM
    return pl.pallas_call(
        _proj_kernel,
        grid=(m // PROJ_TM, NP_COLS // PROJ_TN),
        in_specs=[
            pl.BlockSpec((PROJ_TM, D_MODEL), lambda i, j: (i, 0)),
            pl.BlockSpec((1, 1, D_MODEL), lambda i, j: (i // tiles_per_batch, 0, 0)),
            pl.BlockSpec((1, 1, D_MODEL), lambda i, j: (i // tiles_per_batch, 0, 0)),
            pl.BlockSpec((1, D_MODEL), lambda i, j: (0, 0)),
            pl.BlockSpec((None, PROJ_TN, D_MODEL), lambda i, j: (layer, j, 0)),
        ],
        out_specs=[
            pl.BlockSpec((PROJ_TM, PROJ_TN), lambda i, j: (i, jnp.minimum(j, NB_TILES - 1))),
            pl.BlockSpec((PROJ_TM, PROJ_TN), lambda i, j: (i, jnp.maximum(j - NB_TILES, 0))),
        ],
        out_shape=[
            jax.ShapeDtypeStruct((m, NB_COLS), jnp.bfloat16),
            jax.ShapeDtypeStruct((m, NF_COLS), jnp.float32),
        ],
        scratch_shapes=[pltpu.VMEM((PROJ_TM, D_MODEL), jnp.bfloat16)],
        compiler_params=pltpu.CompilerParams(
            dimension_semantics=("arbitrary", "arbitrary"), vmem_limit_bytes=VMEM_LIMIT),
        name="norm_proj",
    )(x2d, shift, scale, norm_g.reshape(1, D_MODEL), w_p)


TQ = 256
KB = 256
NQ = SEQ // TQ
NKB = SEQ // KB
HALF = KB // 2


def _t5_bucket_np(d):
    max_exact = NUM_BUCKETS // 2
    d = np.maximum(d, 0)
    df = np.maximum(d, 1).astype(np.float32)
    large = max_exact + (np.log(df / np.float32(max_exact)) / np.float32(math.log(MAX_DISTANCE / max_exact))
                         * np.float32(NUM_BUCKETS - max_exact)).astype(np.int32)
    large = np.minimum(large, NUM_BUCKETS - 1)
    return np.where(d < max_exact, d, large).astype(np.int32)


assert int(_t5_bucket_np(np.arange(KB, 2 * SEQ)).min()) == NUM_BUCKETS - 1


def _band_buckets():
    tl = np.arange(TQ)[:, None]
    u = np.arange(KB)[None, :]
    prev = _t5_bucket_np(KB + tl - u)
    diag = _t5_bucket_np(tl - u)
    return np.stack([prev, diag]).astype(np.int32)


BAND_ROWS = 64


def _band_kernel(rb_ref, bucket_ref, o_ref):
    h = pl.program_id(0)
    far = rb_ref[NUM_BUCKETS - 1, h]

    def rows(r, carry):
        rs = pl.ds(pl.multiple_of(r * BAND_ROWS, BAND_ROWS), BAND_ROWS)
        bucket = bucket_ref[0, rs, :]
        acc = jnp.zeros(bucket.shape, jnp.float32)
        for b in range(NUM_BUCKETS):
            acc = jnp.where(bucket == b, rb_ref[b, h] - far, acc)
        o_ref[0, 0, rs, :] = acc * LOG2E
        return carry

    lax.fori_loop(0, TQ // BAND_ROWS, rows, 0)


def _bias_band(rel_bias):
    return pl.pallas_call(
        _band_kernel,
        grid=(A_HEADS, 2),
        in_specs=[
            pl.BlockSpec(memory_space=pltpu.SMEM),
            pl.BlockSpec((1, TQ, KB), lambda h, s: (s, 0, 0)),
        ],
        out_specs=pl.BlockSpec((1, 1, TQ, KB), lambda h, s: (h, s, 0, 0)),
        out_shape=jax.ShapeDtypeStruct((A_HEADS, 2, TQ, KB), jnp.float32),
        name="bias_band",
    )(rel_bias, jnp.asarray(_band_buckets()))


def _dot_nt(a, b):
    return lax.dot_general(a, b, (((1,), (1,)), ((), ())), preferred_element_type=jnp.float32)


def _dot_tn(a, b):
    return lax.dot_general(a, b, (((0,), (0,)), ((), ())), preferred_element_type=jnp.float32)


def _dsa_kernel(q_ref, k_ref, v_ref, iq_ref, ik_ref, iw_ref, ag_ref, band_ref, o_ref,
                iklo_scr, ikhi_scr, vext_scr, wb_scr, iq2_scr, key_scr, khi_scr, klo_scr, thrn_scr,
                cutn_scr, madd_scr, m_scr, accv_scr, accl_scr):
    qi = pl.program_id(1)
    nkb = qi + 1
    bf16 = jnp.bfloat16

    def key_rows(kb):
        return pl.ds(pl.multiple_of(kb * KB, KB), KB)

    @pl.when(qi == 0)
    def _():
        def prep(c, carry):
            rs = key_rows(c)
            ik = ik_ref[0, rs, :]
            iklo_scr[rs, :] = ik.astype(bf16)
            ikhi_scr[rs, :] = pltpu.roll(ik, IDX_DIM, axis=1).astype(bf16)
            for h in range(A_HEADS):
                vext_scr[h, rs, :A_HEAD_DIM] = v_ref[0, rs, h * A_HEAD_DIM:(h + 1) * A_HEAD_DIM]
                vext_scr[h, rs, A_HEAD_DIM:] = jnp.ones((KB, A_HEAD_DIM), bf16)
            return carry

        lax.fori_loop(0, NKB, prep, 0)

    iw = iw_ref[0][:, :IDX_HEADS] * (IDX_HEADS ** -0.5 * IDX_DIM ** -0.5)
    for j in range(IDX_HEADS):
        wb_scr[j] = jnp.broadcast_to(iw[:, j:j + 1], (TQ, HALF))
    for jp in range(IDX_HEADS // 2):
        iq2_scr[jp * TQ:(jp + 1) * TQ, :] = iq_ref[0, :, jp * LANE:(jp + 1) * LANE].astype(bf16)

    row = lax.broadcasted_iota(jnp.int32, (TQ, HALF), 0)
    col = lax.broadcasted_iota(jnp.int32, (TQ, HALF), 1)

    def score_block(kb, carry):
        rs = key_rows(kb)
        s_lo = _dot_nt(iq2_scr[...], iklo_scr[rs, :])
        s_hi = _dot_nt(iq2_scr[...], ikhi_scr[rs, :])
        for half in range(2):
            ls = slice(half * HALF, (half + 1) * HALF)
            acc = jnp.zeros((TQ, HALF), jnp.float32)
            for jp in range(IDX_HEADS // 2):
                rj = slice(jp * TQ, (jp + 1) * TQ)
                acc = (acc + jnp.maximum(s_lo[rj, ls], 0.0) * wb_scr[2 * jp]
                       + jnp.maximum(s_hi[rj, ls], 0.0) * wb_scr[2 * jp + 1])
            bits = pltpu.bitcast(acc + 0.0, jnp.int32)
            key = jnp.where(bits >= 0, bits, bits ^ 0x7FFFFFFF)
            causal = (kb * KB + half * HALF + col) <= (qi * TQ + row)
            key_scr[kb, :, ls] = jnp.where(causal, key, INT_MIN)
        keyt = key_scr[kb].T
        khi_scr[kb] = lax.shift_right_arithmetic(keyt, 16).astype(jnp.int16)
        klo_scr[kb] = ((keyt & 0xFFFF) + I16_MIN).astype(jnp.int16)
        return carry

    lax.fori_loop(0, nkb, score_block, 0)

    def count(ref, cand, strictly=False):
        c16 = jnp.broadcast_to(cand.astype(jnp.int16), (PACK16, TQ))

        def body(kb, cnt):
            v = ref[kb]
            for r in range(KB // PACK16):
                blk = v[r * PACK16:(r + 1) * PACK16]
                hit = (blk > c16) if strictly else (blk >= c16)
                cnt = cnt + jnp.where(hit, jnp.int16(1), jnp.int16(0))
            return cnt

        cnt = lax.fori_loop(0, nkb, body, jnp.zeros((PACK16, TQ), jnp.int16))
        return jnp.sum(cnt.astype(jnp.int32), axis=0, keepdims=True)

    def bisect16(ref, need):
        zero = jnp.zeros((1, TQ), jnp.int32)
        prefix = jnp.where(count(ref, zero) >= need, zero, I16_MIN)

        def bit_pass(it, prefix):
            cand = prefix + lax.shift_left(jnp.int32(1), 14 - it)
            return jnp.where(count(ref, cand) >= need, cand, prefix)

        return lax.fori_loop(0, 15, bit_pass, prefix)

    topk = jnp.full((1, TQ), TOPK, jnp.int32)
    p_hi = bisect16(khi_scr, topk)
    above = count(khi_scr, p_hi, strictly=True)
    p_hi16 = jnp.broadcast_to(p_hi.astype(jnp.int16), (PACK16, TQ))

    def narrow(kb, carry):
        hi, lo = khi_scr[kb], klo_scr[kb]
        for r in range(KB // PACK16):
            rs = slice(r * PACK16, (r + 1) * PACK16)
            klo_scr[kb, rs, :] = jnp.where(hi[rs] == p_hi16, lo[rs], jnp.int16(I16_MIN))
        return carry

    lax.fori_loop(0, nkb, narrow, 0)
    p_lo = bisect16(klo_scr, topk - above)
    prefix = p_hi * 65536 + (p_lo - I16_MIN)
    thr = jnp.maximum(prefix, INT_MIN + 1)
    thrn_scr[...] = jnp.broadcast_to(thr, (KB, TQ)).T

    p_lo16 = jnp.broadcast_to(p_lo.astype(jnp.int16), (PACK16, TQ))
    sub16 = lax.broadcasted_iota(jnp.int32, (PACK16, TQ), 0)

    def tie_blocks(kb, fn, carry):
        hi, lo = khi_scr[kb], klo_scr[kb]
        for r in range(KB // PACK16):
            rs = slice(r * PACK16, (r + 1) * PACK16)
            carry = fn(kb, r, rs, (hi[rs] == p_hi16) & (lo[rs] == p_lo16), carry)
        return carry

    def count_tied_and_above(kb, cnts):
        n_tied, n_above = cnts
        hi, lo = khi_scr[kb], klo_scr[kb]
        one, nil = jnp.int16(1), jnp.int16(0)
        for r in range(KB // PACK16):
            rs = slice(r * PACK16, (r + 1) * PACK16)
            n_tied = n_tied + jnp.where((hi[rs] == p_hi16) & (lo[rs] == p_lo16), one, nil)
            n_above = n_above + jnp.where(lo[rs] > p_lo16, one, nil)
        return n_tied, n_above

    zeros16 = jnp.zeros((PACK16, TQ), jnp.int16)
    tied, above_lo = lax.fori_loop(0, nkb, count_tied_and_above, (zeros16, zeros16))
    tied = jnp.sum(tied.astype(jnp.int32), axis=0, keepdims=True)
    above_lo = jnp.sum(above_lo.astype(jnp.int32), axis=0, keepdims=True)
    room = topk - above - above_lo
    trim = (tied > room) & (prefix != INT_MIN)

    def madd_block(kb, carry):
        madd_scr[kb] = jnp.where(key_scr[kb] >= thrn_scr[...], 0.0, NEG_BIG)
        return carry

    lax.fori_loop(0, nkb, madd_block, 0)

    @pl.when(jnp.max(trim.astype(jnp.int32)) > 0)
    def _():
        def mark(kb, carry):
            def put(kb, r, rs, tied, c):
                neg_pos = (-(kb * KB + r * PACK16) - sub16).astype(jnp.int16)
                khi_scr[kb, rs, :] = jnp.where(tied, neg_pos, jnp.int16(I16_MIN))
                return c
            return tie_blocks(kb, put, carry)

        lax.fori_loop(0, nkb, mark, 0)
        cut = jnp.where(trim, -bisect16(khi_scr, room), SEQ)
        cutn_scr[...] = jnp.broadcast_to(cut, (KB, TQ)).T

        def madd_ties(kb, carry):
            key = key_scr[kb]
            pos = kb * KB + lax.broadcasted_iota(jnp.int32, (TQ, KB), 1)
            keep_tie = jnp.where(pos <= cutn_scr[...], 0.0, NEG_BIG)
            madd_scr[kb] = jnp.where(key > thrn_scr[...], 0.0,
                                     jnp.where(key == thrn_scr[...], keep_tie, NEG_BIG))
            return carry

        lax.fori_loop(0, nkb, madd_ties, 0)

    m_scr[...] = jnp.full(m_scr.shape, NEG_BIG, jnp.float32)
    accv_scr[...] = jnp.zeros_like(accv_scr)
    accl_scr[...] = jnp.zeros_like(accl_scr)

    def attend(kb, slot):
        rs = key_rows(kb)
        for h in range(A_HEADS):
            hs = slice(h * A_HEAD_DIM, (h + 1) * A_HEAD_DIM)
            lg = _dot_nt(q_ref[0, :, hs], k_ref[0, rs, hs]) + madd_scr[kb]
            if slot is not None:
                lg = lg + band_ref[h, slot]
            m_old = m_scr[h]
            m_new = jnp.maximum(m_old, jnp.max(lg, axis=-1, keepdims=True))
            alpha = jnp.exp2(m_old - m_new)
            p = jnp.concatenate([jnp.exp2(lg[:, :HALF] - m_new), jnp.exp2(lg[:, HALF:] - m_new)], axis=1)
            pv = jnp.dot(p.astype(bf16), vext_scr[h, rs, :], preferred_element_type=jnp.float32)
            accv_scr[h] = accv_scr[h] * alpha + pv[:, :A_HEAD_DIM]
            accl_scr[h] = accl_scr[h] * alpha + pv[:, A_HEAD_DIM:]
            m_scr[h] = m_new

    def far_block(kb, carry):
        attend(kb, None)
        return carry

    lax.fori_loop(0, qi - 1, far_block, 0)

    @pl.when(qi >= 1)
    def _():
        attend(qi - 1, 0)

    attend(qi, 1)

    for h in range(A_HEADS):
        hs = slice(h * A_HEAD_DIM, (h + 1) * A_HEAD_DIM)
        g = ag_ref[0, :, hs]
        o_ref[0, :, hs] = (accv_scr[h] / accl_scr[h] * (g * jax.nn.sigmoid(g))).astype(o_ref.dtype)


def _dsa_attention(proj_b, proj_f, band):
    cb = lambda name: BF_OFF[name] // A_WIDTH
    once = pl.Buffered(1)
    return pl.pallas_call(
        _dsa_kernel,
        grid=(BATCH, NQ),
        in_specs=[
            pl.BlockSpec((1, TQ, A_WIDTH), lambda b, i: (b, i, cb("aq"))),
            pl.BlockSpec((1, SEQ, A_WIDTH), lambda b, i: (b, 0, cb("ak")), pipeline_mode=once),
            pl.BlockSpec((1, SEQ, A_WIDTH), lambda b, i: (b, 0, cb("av")), pipeline_mode=once),
            pl.BlockSpec((1, TQ, A_WIDTH), lambda b, i: (b, i, F32_OFF["iq"] // A_WIDTH)),
            pl.BlockSpec((1, SEQ, LANE), lambda b, i: (b, 0, F32_OFF["ik"] // LANE), pipeline_mode=once),
            pl.BlockSpec((1, TQ, LANE), lambda b, i: (b, i, F32_OFF["iw"] // LANE)),
            pl.BlockSpec((1, TQ, A_WIDTH), lambda b, i: (b, i, F32_OFF["ag"] // A_WIDTH)),
            pl.BlockSpec((A_HEADS, 2, TQ, KB), lambda b, i: (0, 0, 0, 0), pipeline_mode=once),
        ],
        out_specs=pl.BlockSpec((1, TQ, A_WIDTH), lambda b, i: (b, i, 0)),
        out_shape=jax.ShapeDtypeStruct((BATCH, SEQ, A_WIDTH), jnp.bfloat16),
        scratch_shapes=[
            pltpu.VMEM((SEQ, LANE), jnp.bfloat16),
            pltpu.VMEM((SEQ, LANE), jnp.bfloat16),
            pltpu.VMEM((A_HEADS, SEQ, 2 * A_HEAD_DIM), jnp.bfloat16),
            pltpu.VMEM((IDX_HEADS, TQ, HALF), jnp.float32),
            pltpu.VMEM((IDX_HEADS // 2 * TQ, LANE), jnp.bfloat16),
            pltpu.VMEM((NKB, TQ, KB), jnp.int32),
            pltpu.VMEM((NKB, KB, TQ), jnp.int16),
            pltpu.VMEM((NKB, KB, TQ), jnp.int16),
            pltpu.VMEM((TQ, KB), jnp.int32),
            pltpu.VMEM((TQ, KB), jnp.int32),
            pltpu.VMEM((NKB, TQ, KB), jnp.float32),
            pltpu.VMEM((A_HEADS, TQ, HALF), jnp.float32),
            pltpu.VMEM((A_HEADS, TQ, A_HEAD_DIM), jnp.float32),
            pltpu.VMEM((A_HEADS, TQ, A_HEAD_DIM), jnp.float32),
        ],
        compiler_params=pltpu.CompilerParams(
            dimension_semantics=("arbitrary", "arbitrary"), vmem_limit_bytes=VMEM_LIMIT),
        name="dsa_attention",
    )(proj_b, proj_b, proj_b, proj_f, proj_f, proj_f, proj_f, band)


GLA_CT = 512
GLA_C = 64
GLA_SC = 256
GLA_NCS = GLA_SC // GLA_C


def _gla_kernel(bq_ref, bk_ref, bv_ref, bg_ref, ba_ref, wup_ref, balpha_ref, g_ref, o_ref, st_scr):
    @pl.when(pl.program_id(1) == 0)
    def _():
        st_scr[...] = jnp.zeros_like(st_scr)

    bf16 = jnp.bfloat16
    rr = lax.broadcasted_iota(jnp.int32, (GLA_SC, GLA_SC), 0)
    cc = lax.broadcasted_iota(jnp.int32, (GLA_SC, GLA_SC), 1)
    tri = (rr >= cc) & (rr // GLA_C == cc // GLA_C)
    tri_bf = tri.astype(bf16)
    er = lax.broadcasted_iota(jnp.int32, (GLA_SC, GLA_NCS * B_DK), 0)
    ec = lax.broadcasted_iota(jnp.int32, (GLA_SC, GLA_NCS * B_DK), 1)
    own_block = (er // GLA_C) == (ec // B_DK)
    wup = wup_ref[...].astype(bf16)
    balpha = balpha_ref[...]
    gain = g_ref[...]
    heads = range(B_HEADS)

    def expand(a):
        return jnp.where(own_block, jnp.concatenate([a] * GLA_NCS, axis=1), jnp.zeros((), a.dtype))

    def body(sc, carry):
        rows = pl.ds(pl.multiple_of(sc * GLA_SC, GLA_SC), GLA_SC)
        ba = ba_ref[0, rows, :][:, :GATE_RANK].astype(bf16)
        pre = jnp.dot(ba, wup, preferred_element_type=jnp.float32) + balpha
        log_a = (jnp.minimum(pre, 0.0) - jnp.log1p(jnp.exp(-jnp.abs(pre)))) * (1.0 / GATE_TEMP)
        la_hi = log_a.astype(bf16)
        la_lo = (log_a - la_hi.astype(jnp.float32)).astype(bf16)
        bcum = (jnp.dot(tri_bf, la_hi, preferred_element_type=jnp.float32)
                + jnp.dot(tri_bf, la_lo, preferred_element_type=jnp.float32))
        ks = [slice(h * B_DK, (h + 1) * B_DK) for h in heads]
        vs = [slice(h * B_DV, (h + 1) * B_DV) for h in heads]
        qe, ke, kd, dec, v = [], [], [], [], []
        for h in heads:
            b = bcum[:, ks[h]]
            b_end = b.reshape(GLA_NCS, GLA_C, B_DK)[:, GLA_C - 1:GLA_C, :]
            b_end_rows = jnp.broadcast_to(b_end, (GLA_NCS, GLA_C, B_DK)).reshape(GLA_SC, B_DK)
            q = bq_ref[0, rows, ks[h]] * (B_DK ** -0.5)
            k = bk_ref[0, rows, ks[h]]
            qe.append((q * jnp.exp(b)).astype(bf16))
            ke.append((k * jnp.exp(-b)).astype(bf16))
            kd.append((k * jnp.exp(b_end_rows - b)).astype(bf16))
            dec.append(jnp.exp(b_end))
            v.append(bv_ref[0, rows, vs[h]].astype(bf16))
        attn = [jnp.where(tri, _dot_nt(qe[h], ke[h]), 0.0).astype(bf16) for h in heads]
        o_intra = [jnp.dot(attn[h], v[h], preferred_element_type=jnp.float32) for h in heads]
        upd = [_dot_tn(v[h], expand(kd[h])) for h in heads]
        o_inter = []
        for h in heads:
            st = st_scr[h]
            states = []
            for c in range(GLA_NCS):
                states.append(st.astype(bf16))
                st = st * dec[h][c] + upd[h][:, c * B_DK:(c + 1) * B_DK]
            st_scr[h] = st
            o_inter.append(_dot_nt(expand(qe[h]), jnp.concatenate(states, axis=1)))
        for h in heads:
            o = o_intra[h] + o_inter[h]
            on = o * lax.rsqrt(jnp.mean(o * o, axis=-1, keepdims=True) + EPS) * gain
            g = bg_ref[0, rows, vs[h]]
            o_ref[0, rows, vs[h]] = (on * (g * jax.nn.sigmoid(g))).astype(o_ref.dtype)
        return carry

    lax.fori_loop(0, GLA_CT // GLA_SC, body, 0)


def _gla(proj_f, w_alpha_up, b_alpha, gla_g):
    return pl.pallas_call(
        _gla_kernel,
        grid=(BATCH, SEQ // GLA_CT),
        in_specs=[
            pl.BlockSpec((1, GLA_CT, B_KEY_WIDTH), lambda b, t: (b, t, F32_OFF["bq"] // B_KEY_WIDTH)),
            pl.BlockSpec((1, GLA_CT, B_KEY_WIDTH), lambda b, t: (b, t, F32_OFF["bk"] // B_KEY_WIDTH)),
            pl.BlockSpec((1, GLA_CT, B_WIDTH), lambda b, t: (b, t, F32_OFF["bv"] // B_WIDTH)),
            pl.BlockSpec((1, GLA_CT, B_WIDTH), lambda b, t: (b, t, F32_OFF["bg"] // B_WIDTH)),
            pl.BlockSpec((1, GLA_CT, LANE), lambda b, t: (b, t, F32_OFF["ba"] // LANE)),
            pl.BlockSpec((GATE_RANK, B_KEY_WIDTH), lambda b, t: (0, 0)),
            pl.BlockSpec((1, B_KEY_WIDTH), lambda b, t: (0, 0)),
            pl.BlockSpec((1, B_DV), lambda b, t: (0, 0)),
        ],
        out_specs=pl.BlockSpec((1, GLA_CT, B_WIDTH), lambda b, t: (b, t, 0)),
        out_shape=jax.ShapeDtypeStruct((BATCH, SEQ, B_WIDTH), jnp.bfloat16),
        scratch_shapes=[pltpu.VMEM((B_HEADS, B_DV, B_DK), jnp.float32)],
        compiler_params=pltpu.CompilerParams(
            dimension_semantics=("arbitrary", "arbitrary"), vmem_limit_bytes=VMEM_LIMIT),
        name="gla",
    )(proj_f, proj_f, proj_f, proj_f, proj_f, w_alpha_up, b_alpha.reshape(1, B_KEY_WIDTH),
      gla_g.reshape(1, B_DV))


OUT_TM = 512


def _out_kernel(a_ref, b_ref, wa_ref, wb_ref, x_ref, gate_ref, fg_ref, o_ref, *, final_norm):
    y = (jnp.dot(a_ref[...], wa_ref[...], preferred_element_type=jnp.float32)
         + jnp.dot(b_ref[...], wb_ref[...], preferred_element_type=jnp.float32))
    xn = x_ref[...] + gate_ref[0] * y
    if final_norm:
        r = xn * lax.rsqrt(jnp.mean(xn * xn, axis=-1, keepdims=True) + EPS)
        xn = r * fg_ref[...]
    o_ref[...] = xn


def _out_proj(a_out, b_out, w_out_bf, x2d, gate, final_g, layer, final_norm):
    m = x2d.shape[0]
    tiles_per_batch = SEQ // OUT_TM
    return pl.pallas_call(
        functools.partial(_out_kernel, final_norm=final_norm),
        grid=(m // OUT_TM,),
        in_specs=[
            pl.BlockSpec((OUT_TM, A_WIDTH), lambda i: (i, 0)),
            pl.BlockSpec((OUT_TM, B_WIDTH), lambda i: (i, 0)),
            pl.BlockSpec((None, A_WIDTH, D_MODEL), lambda i: (layer, 0, 0)),
            pl.BlockSpec((None, B_WIDTH, D_MODEL), lambda i: (layer, 1, 0)),
            pl.BlockSpec((OUT_TM, D_MODEL), lambda i: (i, 0)),
            pl.BlockSpec((1, 1, D_MODEL), lambda i: (i // tiles_per_batch, 0, 0)),
            pl.BlockSpec((1, D_MODEL), lambda i: (0, 0)),
        ],
        out_specs=pl.BlockSpec((OUT_TM, D_MODEL), lambda i: (i, 0)),
        out_shape=jax.ShapeDtypeStruct((m, D_MODEL), jnp.float32),
        compiler_params=pltpu.CompilerParams(
            dimension_semantics=("arbitrary",), vmem_limit_bytes=VMEM_LIMIT),
        name="out_proj",
    )(a_out, b_out, w_out_bf, w_out_bf, x2d, gate, final_g.reshape(1, D_MODEL))


def kernel(x, c, w_ada, b_ada, norm_g, w_in, w_alpha_up, b_alpha, gla_g, w_out, rel_bias, final_g):
    mod = _adaln_mod(c, w_ada, b_ada)
    band = _bias_band(rel_bias)
    w_in_p = _pack_w_in(jnp.swapaxes(w_in, 1, 2))
    w_out_bf = w_out.astype(jnp.bfloat16)
    x2d = x.reshape(BATCH * SEQ, D_MODEL)
    for l in range(DEPTH):
        shift = mod[l, :, 0:D_MODEL].reshape(BATCH, 1, D_MODEL)
        scale = mod[l, :, D_MODEL:2 * D_MODEL].reshape(BATCH, 1, D_MODEL)
        gate = mod[l, :, 2 * D_MODEL:].reshape(BATCH, 1, D_MODEL)
        proj_b, proj_f = _norm_proj(x2d, shift, scale, norm_g[l], w_in_p, l)
        proj_b = proj_b.reshape(BATCH, SEQ, NB_COLS)
        proj_f = proj_f.reshape(BATCH, SEQ, NF_COLS)
        a_out = _dsa_attention(proj_b, proj_f, band)
        b_out = _gla(proj_f, w_alpha_up[l], b_alpha[l], gla_g[l])
        x2d = _out_proj(a_out.reshape(BATCH * SEQ, A_WIDTH), b_out.reshape(BATCH * SEQ, B_WIDTH),
                        w_out_bf, x2d, gate, final_g, l, final_norm=(l == DEPTH - 1))
    return x2d.reshape(BATCH, SEQ, D_MODEL)
```

```python
import functools
import math

import numpy as np
import jax
import jax.numpy as jnp
from jax import lax
from jax.experimental import pallas as pl
from jax.experimental.pallas import tpu as pltpu

D_MODEL = 2048
BATCH = 4
SEQ = 2048
DEPTH = 4
A_WIDTH = 1024
A_HEADS = 8
A_HEAD_DIM = 128
IDX_HEADS = 16
IDX_DIM = 64
TOPK = min(256, SEQ // 4)
B_WIDTH = 1024
B_HEADS = 4
B_KEY_WIDTH = 512
B_DK = 128
B_DV = 256
GATE_RANK = 16
GATE_TEMP = 16.0
NUM_BUCKETS = 32
MAX_DISTANCE = 128
EPS = 1e-6

IN_WIDTHS = (A_WIDTH, A_WIDTH, A_WIDTH, A_WIDTH, IDX_HEADS * IDX_DIM, IDX_DIM, IDX_HEADS,
             B_KEY_WIDTH, B_KEY_WIDTH, B_WIDTH, B_WIDTH, GATE_RANK)
IN_NAMES = ("aq", "ak", "av", "ag", "iq", "ik", "iw", "bq", "bk", "bv", "bg", "ba")
IN_OFFSETS = dict(zip(IN_NAMES, np.concatenate([[0], np.cumsum(IN_WIDTHS)[:-1]]).tolist()))
IN_WIDTH_OF = dict(zip(IN_NAMES, IN_WIDTHS))
IN_COLS = sum(IN_WIDTHS)

LANE = 128
VMEM_LIMIT = 52 * 1024 * 1024

PROJ_TN = 512
BF_SEGS = (("aq", 1024), ("ak", 1024), ("av", 1024))
F32_SEGS = (("ag", 1024), ("iq", 1024), ("bv", 1024), ("bg", 1024), ("bq", 512), ("bk", 512),
            ("ik", LANE), ("iw", LANE), ("ba", LANE))
NB_COLS = sum(w for _, w in BF_SEGS)
NF_USED = sum(w for _, w in F32_SEGS)
NF_COLS = -(-NF_USED // PROJ_TN) * PROJ_TN
NP_COLS = NB_COLS + NF_COLS


def _seg_offsets(segs):
    offs, o = {}, 0
    for name, w in segs:
        offs[name] = o
        o += w
    return offs


BF_OFF = _seg_offsets(BF_SEGS)
F32_OFF = _seg_offsets(F32_SEGS)

INT_MIN = -2 ** 31
NEG_BIG = -1e30
LOG2E = math.log2(math.e)
AQ_SCALE = A_HEAD_DIM ** -0.5 * LOG2E
I16_MIN = -2 ** 15
PACK16 = 16


PACK_COLS = 256


def _pack_kernel(w_ref, o_ref):
    dst = 0
    for name, width in BF_SEGS + F32_SEGS:
        src, used = IN_OFFSETS[name], IN_WIDTH_OF[name]
        o_ref[0, dst:dst + used, :] = w_ref[0, src:src + used, :].astype(jnp.bfloat16)
        if width > used:
            o_ref[0, dst + used:dst + width, :] = jnp.zeros((width - used, PACK_COLS), jnp.bfloat16)
        dst += width
    if dst < NP_COLS:
        o_ref[0, dst:, :] = jnp.zeros((NP_COLS - dst, PACK_COLS), jnp.bfloat16)


def _pack_w_in(w_in_t):
    return pl.pallas_call(
        _pack_kernel,
        grid=(DEPTH, D_MODEL // PACK_COLS),
        in_specs=[pl.BlockSpec((1, IN_COLS, PACK_COLS), lambda l, c: (l, 0, c))],
        out_specs=pl.BlockSpec((1, NP_COLS, PACK_COLS), lambda l, c: (l, 0, c)),
        out_shape=jax.ShapeDtypeStruct((DEPTH, NP_COLS, D_MODEL), jnp.bfloat16),
        compiler_params=pltpu.CompilerParams(
            dimension_semantics=("arbitrary", "arbitrary"), vmem_limit_bytes=VMEM_LIMIT),
        name="pack_w_in",
    )(w_in_t)


MOD_TN = 768


def _mod_kernel(c_ref, w_ref, b_ref, o_ref):
    c = c_ref[...]
    c_act = c * jax.nn.sigmoid(c)
    acc = jnp.dot(c_act.astype(jnp.bfloat16), w_ref[0].astype(jnp.bfloat16),
                  preferred_element_type=jnp.float32)
    o_ref[0] = acc + b_ref[0]


def _adaln_mod(c, w_ada, b_ada):
    cp = jnp.pad(c, ((0, 8 - BATCH), (0, 0)))
    out = pl.pallas_call(
        _mod_kernel,
        grid=(DEPTH, 3 * D_MODEL // MOD_TN),
        in_specs=[
            pl.BlockSpec((8, D_MODEL), lambda l, j: (0, 0)),
            pl.BlockSpec((1, D_MODEL, MOD_TN), lambda l, j: (l, 0, j)),
            pl.BlockSpec((1, 1, MOD_TN), lambda l, j: (l, 0, j)),
        ],
        out_specs=pl.BlockSpec((1, 8, MOD_TN), lambda l, j: (l, 0, j)),
        out_shape=jax.ShapeDtypeStruct((DEPTH, 8, 3 * D_MODEL), jnp.float32),
        compiler_params=pltpu.CompilerParams(
            dimension_semantics=("arbitrary", "arbitrary"), vmem_limit_bytes=VMEM_LIMIT),
        name="adaln_mod",
    )(cp, w_ada, b_ada.reshape(DEPTH, 1, 3 * D_MODEL))
    return out[:, :BATCH]


PROJ_TM = 1024
NB_TILES = NB_COLS // PROJ_TN
N_COL_TILES = NP_COLS // PROJ_TN
AHEAD_ROWS = 64
assert PROJ_TM // AHEAD_ROWS < N_COL_TILES


def _proj_kernel(x_ref, shift_ref, scale_ref, g_ref, w_ref, ob_ref, of_ref, ha_scr, hb_scr):
    i, j = pl.program_id(0), pl.program_id(1)

    def normed(x):
        r = x * lax.rsqrt(jnp.mean(x * x, axis=-1, keepdims=True) + EPS)
        return ((r * g_ref[...]) * (1.0 + scale_ref[0]) + shift_ref[0]).astype(jnp.bfloat16)

    @pl.when((i == 0) & (j == 0))
    def _():
        ha_scr[...] = normed(x_ref[...])

    def step(h_cur, h_next):
        def norm_ahead():
            c = jnp.clip(j - (i == 0).astype(jnp.int32), 0, PROJ_TM // AHEAD_ROWS - 1)
            rows = pl.ds(pl.multiple_of(c * AHEAD_ROWS, AHEAD_ROWS), AHEAD_ROWS)
            h_next[rows, :] = normed(x_ref[rows, :])

        @pl.when(j < NB_TILES)
        def _():
            norm_ahead()
            out_scale = jnp.where(j < A_WIDTH // PROJ_TN, AQ_SCALE, 1.0)
            ob_ref[...] = (_dot_nt(h_cur[...], w_ref[...]) * out_scale).astype(jnp.bfloat16)

        @pl.when(j >= NB_TILES)
        def _():
            norm_ahead()
            of_ref[...] = _dot_nt(h_cur[...], w_ref[...])

    @pl.when(i % 2 == 0)
    def _():
        step(ha_scr, hb_scr)

    @pl.when(i % 2 == 1)
    def _():
        step(hb_scr, ha_scr)


def _norm_proj(x2d, shift, scale, norm_g, w_p, layer):
    m = x2d.shape[0]
    n_row_tiles = m // PROJ_TM
    tiles_per_batch = SEQ // PROJ_TM

    def ahead(i, j):
        return jnp.where((i == 0) & (j == 0), 0, jnp.minimum(i + 1, n_row_tiles - 1))

    return pl.pallas_call(
        _proj_kernel,
        grid=(n_row_tiles, N_COL_TILES),
        in_specs=[
            pl.BlockSpec((PROJ_TM, D_MODEL), lambda i, j: (ahead(i, j), 0)),
            pl.BlockSpec((1, 1, D_MODEL), lambda i, j: (ahead(i, j) // tiles_per_batch, 0, 0)),
            pl.BlockSpec((1, 1, D_MODEL), lambda i, j: (ahead(i, j) // tiles_per_batch, 0, 0)),
            pl.BlockSpec((1, D_MODEL), lambda i, j: (0, 0)),
            pl.BlockSpec((None, PROJ_TN, D_MODEL), lambda i, j: (layer, j, 0)),
        ],
        out_specs=[
            pl.BlockSpec((PROJ_TM, PROJ_TN), lambda i, j: (i, jnp.minimum(j, NB_TILES - 1))),
            pl.BlockSpec((PROJ_TM, PROJ_TN), lambda i, j: (i, jnp.maximum(j - NB_TILES, 0))),
        ],
        out_shape=[
            jax.ShapeDtypeStruct((m, NB_COLS), jnp.bfloat16),
            jax.ShapeDtypeStruct((m, NF_COLS), jnp.float32),
        ],
        scratch_shapes=[pltpu.VMEM((PROJ_TM, D_MODEL), jnp.bfloat16),
                        pltpu.VMEM((PROJ_TM, D_MODEL), jnp.bfloat16)],
        compiler_params=pltpu.CompilerParams(
            dimension_semantics=("arbitrary", "arbitrary"), vmem_limit_bytes=VMEM_LIMIT),
        name="norm_proj",
    )(x2d, shift, scale, norm_g.reshape(1, D_MODEL), w_p)


TQ = 256
KB = 256
NQ = SEQ // TQ
NKB = SEQ // KB
HALF = KB // 2


def _t5_bucket_np(d):
    max_exact = NUM_BUCKETS // 2
    d = np.maximum(d, 0)
    df = np.maximum(d, 1).astype(np.float32)
    large = max_exact + (np.log(df / np.float32(max_exact)) / np.float32(math.log(MAX_DISTANCE / max_exact))
                         * np.float32(NUM_BUCKETS - max_exact)).astype(np.int32)
    large = np.minimum(large, NUM_BUCKETS - 1)
    return np.where(d < max_exact, d, large).astype(np.int32)


assert int(_t5_bucket_np(np.arange(KB, 2 * SEQ)).min()) == NUM_BUCKETS - 1


def _band_buckets():
    tl = np.arange(TQ)[:, None]
    u = np.arange(KB)[None, :]
    prev = _t5_bucket_np(KB + tl - u)
    diag = _t5_bucket_np(tl - u)
    return np.stack([prev, diag]).astype(np.int32)


BAND_ROWS = 64


def _band_kernel(rb_ref, bucket_ref, o_ref):
    h = pl.program_id(0)
    far = rb_ref[NUM_BUCKETS - 1, h]

    def rows(r, carry):
        rs = pl.ds(pl.multiple_of(r * BAND_ROWS, BAND_ROWS), BAND_ROWS)
        bucket = bucket_ref[0, rs, :]
        acc = jnp.zeros(bucket.shape, jnp.float32)
        for b in range(NUM_BUCKETS):
            acc = jnp.where(bucket == b, rb_ref[b, h] - far, acc)
        o_ref[0, 0, rs, :] = acc * LOG2E
        return carry

    lax.fori_loop(0, TQ // BAND_ROWS, rows, 0)


def _bias_band(rel_bias):
    return pl.pallas_call(
        _band_kernel,
        grid=(A_HEADS, 2),
        in_specs=[
            pl.BlockSpec(memory_space=pltpu.SMEM),
            pl.BlockSpec((1, TQ, KB), lambda h, s: (s, 0, 0)),
        ],
        out_specs=pl.BlockSpec((1, 1, TQ, KB), lambda h, s: (h, s, 0, 0)),
        out_shape=jax.ShapeDtypeStruct((A_HEADS, 2, TQ, KB), jnp.float32),
        name="bias_band",
    )(rel_bias, jnp.asarray(_band_buckets()))


def _dot_nt(a, b):
    return lax.dot_general(a, b, (((1,), (1,)), ((), ())), preferred_element_type=jnp.float32)


def _dot_tn(a, b):
    return lax.dot_general(a, b, (((0,), (0,)), ((), ())), preferred_element_type=jnp.float32)


def _dsa_kernel(q_ref, k_ref, v_ref, iq_ref, ik_ref, iw_ref, ag_ref, band_ref, o_ref,
                iklo_scr, ikhi_scr, vext_scr, wb_scr, iq2_scr, key_scr, khi_scr, klo_scr, thrn_scr,
                cutn_scr, madd_scr, m_scr, accv_scr, accl_scr):
    qi = pl.program_id(1)
    nkb = qi + 1
    bf16 = jnp.bfloat16

    def key_rows(kb):
        return pl.ds(pl.multiple_of(kb * KB, KB), KB)

    @pl.when(qi == 0)
    def _():
        def prep(c, carry):
            rs = key_rows(c)
            ik = ik_ref[0, rs, :]
            iklo_scr[rs, :] = ik.astype(bf16)
            ikhi_scr[rs, :] = pltpu.roll(ik, IDX_DIM, axis=1).astype(bf16)
            for h in range(A_HEADS):
                vext_scr[h, rs, :A_HEAD_DIM] = v_ref[0, rs, h * A_HEAD_DIM:(h + 1) * A_HEAD_DIM]
                vext_scr[h, rs, A_HEAD_DIM:] = jnp.ones((KB, A_HEAD_DIM), bf16)
            return carry

        lax.fori_loop(0, NKB, prep, 0)

    iw = iw_ref[0][:, :IDX_HEADS] * (IDX_HEADS ** -0.5 * IDX_DIM ** -0.5)
    for j in range(IDX_HEADS):
        wb_scr[j] = jnp.broadcast_to(iw[:, j:j + 1], (TQ, HALF))
    for jp in range(IDX_HEADS // 2):
        iq2_scr[jp * TQ:(jp + 1) * TQ, :] = iq_ref[0, :, jp * LANE:(jp + 1) * LANE].astype(bf16)

    row = lax.broadcasted_iota(jnp.int32, (TQ, HALF), 0)
    col = lax.broadcasted_iota(jnp.int32, (TQ, HALF), 1)

    def score_block(kb, carry):
        rs = key_rows(kb)
        s_lo = _dot_nt(iq2_scr[...], iklo_scr[rs, :])
        s_hi = _dot_nt(iq2_scr[...], ikhi_scr[rs, :])
        for half in range(2):
            ls = slice(half * HALF, (half + 1) * HALF)
            acc = jnp.zeros((TQ, HALF), jnp.float32)
            for jp in range(IDX_HEADS // 2):
                rj = slice(jp * TQ, (jp + 1) * TQ)
                acc = (acc + jnp.maximum(s_lo[rj, ls], 0.0) * wb_scr[2 * jp]
                       + jnp.maximum(s_hi[rj, ls], 0.0) * wb_scr[2 * jp + 1])
            bits = pltpu.bitcast(acc + 0.0, jnp.int32)
            key = jnp.where(bits >= 0, bits, bits ^ 0x7FFFFFFF)
            causal = (kb * KB + half * HALF + col) <= (qi * TQ + row)
            key_scr[kb, :, ls] = jnp.where(causal, key, INT_MIN)
        keyt = key_scr[kb].T
        khi_scr[kb] = lax.shift_right_arithmetic(keyt, 16).astype(jnp.int16)
        klo_scr[kb] = ((keyt & 0xFFFF) + I16_MIN).astype(jnp.int16)
        return carry

    lax.fori_loop(0, nkb, score_block, 0)

    def count(ref, cand, strictly=False):
        c16 = jnp.broadcast_to(cand.astype(jnp.int16), (PACK16, TQ))

        def body(kb, cnt):
            v = ref[kb]
            for r in range(KB // PACK16):
                blk = v[r * PACK16:(r + 1) * PACK16]
                hit = (blk > c16) if strictly else (blk >= c16)
                cnt = cnt + jnp.where(hit, jnp.int16(1), jnp.int16(0))
            return cnt

        cnt = lax.fori_loop(0, nkb, body, jnp.zeros((PACK16, TQ), jnp.int16))
        return jnp.sum(cnt.astype(jnp.int32), axis=0, keepdims=True)

    def bisect16(ref, need):
        zero = jnp.zeros((1, TQ), jnp.int32)
        prefix = jnp.where(count(ref, zero) >= need, zero, I16_MIN)

        def bit_pass(it, prefix):
            cand = prefix + lax.shift_left(jnp.int32(1), 14 - it)
            return jnp.where(count(ref, cand) >= need, cand, prefix)

        return lax.fori_loop(0, 15, bit_pass, prefix)

    topk = jnp.full((1, TQ), TOPK, jnp.int32)
    p_hi = bisect16(khi_scr, topk)
    above = count(khi_scr, p_hi, strictly=True)
    p_hi16 = jnp.broadcast_to(p_hi.astype(jnp.int16), (PACK16, TQ))

    def narrow(kb, carry):
        hi, lo = khi_scr[kb], klo_scr[kb]
        for r in range(KB // PACK16):
            rs = slice(r * PACK16, (r + 1) * PACK16)
            klo_scr[kb, rs, :] = jnp.where(hi[rs] == p_hi16, lo[rs], jnp.int16(I16_MIN))
        return carry

    lax.fori_loop(0, nkb, narrow, 0)
    p_lo = bisect16(klo_scr, topk - above)
    prefix = p_hi * 65536 + (p_lo - I16_MIN)
    thr = jnp.maximum(prefix, INT_MIN + 1)
    thrn_scr[...] = jnp.broadcast_to(thr, (KB, TQ)).T

    p_lo16 = jnp.broadcast_to(p_lo.astype(jnp.int16), (PACK16, TQ))
    sub16 = lax.broadcasted_iota(jnp.int32, (PACK16, TQ), 0)

    def tie_blocks(kb, fn, carry):
        hi, lo = khi_scr[kb], klo_scr[kb]
        for r in range(KB // PACK16):
            rs = slice(r * PACK16, (r + 1) * PACK16)
            carry = fn(kb, r, rs, (hi[rs] == p_hi16) & (lo[rs] == p_lo16), carry)
        return carry

    def count_tied_and_above(kb, cnts):
        n_tied, n_above = cnts
        hi, lo = khi_scr[kb], klo_scr[kb]
        one, nil = jnp.int16(1), jnp.int16(0)
        for r in range(KB // PACK16):
            rs = slice(r * PACK16, (r + 1) * PACK16)
            n_tied = n_tied + jnp.where((hi[rs] == p_hi16) & (lo[rs] == p_lo16), one, nil)
            n_above = n_above + jnp.where(lo[rs] > p_lo16, one, nil)
        return n_tied, n_above

    zeros16 = jnp.zeros((PACK16, TQ), jnp.int16)
    tied, above_lo = lax.fori_loop(0, nkb, count_tied_and_above, (zeros16, zeros16))
    tied = jnp.sum(tied.astype(jnp.int32), axis=0, keepdims=True)
    above_lo = jnp.sum(above_lo.astype(jnp.int32), axis=0, keepdims=True)
    room = topk - above - above_lo
    trim = (tied > room) & (prefix != INT_MIN)

    def madd_block(kb, carry):
        madd_scr[kb] = jnp.where(key_scr[kb] >= thrn_scr[...], 0.0, NEG_BIG)
        return carry

    lax.fori_loop(0, nkb, madd_block, 0)

    @pl.when(jnp.max(trim.astype(jnp.int32)) > 0)
    def _():
        def mark(kb, carry):
            def put(kb, r, rs, tied, c):
                neg_pos = (-(kb * KB + r * PACK16) - sub16).astype(jnp.int16)
                khi_scr[kb, rs, :] = jnp.where(tied, neg_pos, jnp.int16(I16_MIN))
                return c
            return tie_blocks(kb, put, carry)

        lax.fori_loop(0, nkb, mark, 0)
        cut = jnp.where(trim, -bisect16(khi_scr, room), SEQ)
        cutn_scr[...] = jnp.broadcast_to(cut, (KB, TQ)).T

        def madd_ties(kb, carry):
            key = key_scr[kb]
            pos = kb * KB + lax.broadcasted_iota(jnp.int32, (TQ, KB), 1)
            keep_tie = jnp.where(pos <= cutn_scr[...], 0.0, NEG_BIG)
            madd_scr[kb] = jnp.where(key > thrn_scr[...], 0.0,
                                     jnp.where(key == thrn_scr[...], keep_tie, NEG_BIG))
            return carry

        lax.fori_loop(0, nkb, madd_ties, 0)

    m_scr[...] = jnp.full(m_scr.shape, NEG_BIG, jnp.float32)
    accv_scr[...] = jnp.zeros_like(accv_scr)
    accl_scr[...] = jnp.zeros_like(accl_scr)

    def attend(kb, slot):
        rs = key_rows(kb)
        for h in range(A_HEADS):
            hs = slice(h * A_HEAD_DIM, (h + 1) * A_HEAD_DIM)
            lg = _dot_nt(q_ref[0, :, hs], k_ref[0, rs, hs]) + madd_scr[kb]
            if slot is not None:
                lg = lg + band_ref[h, slot]
            m_old = m_scr[h]
            m_new = jnp.maximum(m_old, jnp.max(lg, axis=-1, keepdims=True))
            alpha = jnp.exp2(m_old - m_new)
            p = jnp.concatenate([jnp.exp2(lg[:, :HALF] - m_new), jnp.exp2(lg[:, HALF:] - m_new)], axis=1)
            pv = jnp.dot(p.astype(bf16), vext_scr[h, rs, :], preferred_element_type=jnp.float32)
            accv_scr[h] = accv_scr[h] * alpha + pv[:, :A_HEAD_DIM]
            accl_scr[h] = accl_scr[h] * alpha + pv[:, A_HEAD_DIM:]
            m_scr[h] = m_new

    def far_block(kb, carry):
        attend(kb, None)
        return carry

    lax.fori_loop(0, qi - 1, far_block, 0)

    @pl.when(qi >= 1)
    def _():
        attend(qi - 1, 0)

    attend(qi, 1)

    for h in range(A_HEADS):
        hs = slice(h * A_HEAD_DIM, (h + 1) * A_HEAD_DIM)
        g = ag_ref[0, :, hs]
        o_ref[0, :, hs] = (accv_scr[h] / accl_scr[h] * (g * jax.nn.sigmoid(g))).astype(o_ref.dtype)


def _dsa_attention(proj_b, proj_f, band):
    cb = lambda name: BF_OFF[name] // A_WIDTH
    once = pl.Buffered(1)
    return pl.pallas_call(
        _dsa_kernel,
        grid=(BATCH, NQ),
        in_specs=[
            pl.BlockSpec((1, TQ, A_WIDTH), lambda b, i: (b, i, cb("aq"))),
            pl.BlockSpec((1, SEQ, A_WIDTH), lambda b, i: (b, 0, cb("ak")), pipeline_mode=once),
            pl.BlockSpec((1, SEQ, A_WIDTH), lambda b, i: (b, 0, cb("av")), pipeline_mode=once),
            pl.BlockSpec((1, TQ, A_WIDTH), lambda b, i: (b, i, F32_OFF["iq"] // A_WIDTH)),
            pl.BlockSpec((1, SEQ, LANE), lambda b, i: (b, 0, F32_OFF["ik"] // LANE), pipeline_mode=once),
            pl.BlockSpec((1, TQ, LANE), lambda b, i: (b, i, F32_OFF["iw"] // LANE)),
            pl.BlockSpec((1, TQ, A_WIDTH), lambda b, i: (b, i, F32_OFF["ag"] // A_WIDTH)),
            pl.BlockSpec((A_HEADS, 2, TQ, KB), lambda b, i: (0, 0, 0, 0), pipeline_mode=once),
        ],
        out_specs=pl.BlockSpec((1, TQ, A_WIDTH), lambda b, i: (b, i, 0)),
        out_shape=jax.ShapeDtypeStruct((BATCH, SEQ, A_WIDTH), jnp.bfloat16),
        scratch_shapes=[
            pltpu.VMEM((SEQ, LANE), jnp.bfloat16),
            pltpu.VMEM((SEQ, LANE), jnp.bfloat16),
            pltpu.VMEM((A_HEADS, SEQ, 2 * A_HEAD_DIM), jnp.bfloat16),
            pltpu.VMEM((IDX_HEADS, TQ, HALF), jnp.float32),
            pltpu.VMEM((IDX_HEADS // 2 * TQ, LANE), jnp.bfloat16),
            pltpu.VMEM((NKB, TQ, KB), jnp.int32),
            pltpu.VMEM((NKB, KB, TQ), jnp.int16),
            pltpu.VMEM((NKB, KB, TQ), jnp.int16),
            pltpu.VMEM((TQ, KB), jnp.int32),
            pltpu.VMEM((TQ, KB), jnp.int32),
            pltpu.VMEM((NKB, TQ, KB), jnp.float32),
            pltpu.VMEM((A_HEADS, TQ, HALF), jnp.float32),
            pltpu.VMEM((A_HEADS, TQ, A_HEAD_DIM), jnp.float32),
            pltpu.VMEM((A_HEADS, TQ, A_HEAD_DIM), jnp.float32),
        ],
        compiler_params=pltpu.CompilerParams(
            dimension_semantics=("arbitrary", "arbitrary"), vmem_limit_bytes=VMEM_LIMIT),
        name="dsa_attention",
    )(proj_b, proj_b, proj_b, proj_f, proj_f, proj_f, proj_f, band)


GLA_CT = 512
GLA_C = 64
GLA_SC = 256
GLA_NCS = GLA_SC // GLA_C


def _gla_kernel(bq_ref, bk_ref, bv_ref, bg_ref, ba_ref, wup_ref, balpha_ref, g_ref, o_ref, st_scr):
    @pl.when(pl.program_id(1) == 0)
    def _():
        st_scr[...] = jnp.zeros_like(st_scr)

    bf16 = jnp.bfloat16
    rr = lax.broadcasted_iota(jnp.int32, (GLA_SC, GLA_SC), 0)
    cc = lax.broadcasted_iota(jnp.int32, (GLA_SC, GLA_SC), 1)
    tri = (rr >= cc) & (rr // GLA_C == cc // GLA_C)
    tri_bf = tri.astype(bf16)
    er = lax.broadcasted_iota(jnp.int32, (GLA_SC, GLA_NCS * B_DK), 0)
    ec = lax.broadcasted_iota(jnp.int32, (GLA_SC, GLA_NCS * B_DK), 1)
    own_block = (er // GLA_C) == (ec // B_DK)
    wup = wup_ref[...].astype(bf16)
    balpha = balpha_ref[...]
    gain = g_ref[...]
    heads = range(B_HEADS)

    def expand(a):
        return jnp.where(own_block, jnp.concatenate([a] * GLA_NCS, axis=1), jnp.zeros((), a.dtype))

    def body(sc, carry):
        rows = pl.ds(pl.multiple_of(sc * GLA_SC, GLA_SC), GLA_SC)
        ba = ba_ref[0, rows, :][:, :GATE_RANK].astype(bf16)
        pre = jnp.dot(ba, wup, preferred_element_type=jnp.float32) + balpha
        log_a = (jnp.minimum(pre, 0.0) - jnp.log1p(jnp.exp(-jnp.abs(pre)))) * (1.0 / GATE_TEMP)
        la_hi = log_a.astype(bf16)
        la_lo = (log_a - la_hi.astype(jnp.float32)).astype(bf16)
        bcum = (jnp.dot(tri_bf, la_hi, preferred_element_type=jnp.float32)
                + jnp.dot(tri_bf, la_lo, preferred_element_type=jnp.float32))
        ks = [slice(h * B_DK, (h + 1) * B_DK) for h in heads]
        vs = [slice(h * B_DV, (h + 1) * B_DV) for h in heads]
        qe, ke, kd, dec, v = [], [], [], [], []
        for h in heads:
            b = bcum[:, ks[h]]
            b_end = b.reshape(GLA_NCS, GLA_C, B_DK)[:, GLA_C - 1:GLA_C, :]
            b_end_rows = jnp.broadcast_to(b_end, (GLA_NCS, GLA_C, B_DK)).reshape(GLA_SC, B_DK)
            q = bq_ref[0, rows, ks[h]] * (B_DK ** -0.5)
            k = bk_ref[0, rows, ks[h]]
            qe.append((q * jnp.exp(b)).astype(bf16))
            ke.append((k * jnp.exp(-b)).astype(bf16))
            kd.append((k * jnp.exp(b_end_rows - b)).astype(bf16))
            dec.append(jnp.exp(b_end))
            v.append(bv_ref[0, rows, vs[h]].astype(bf16))
        attn = [jnp.where(tri, _dot_nt(qe[h], ke[h]), 0.0).astype(bf16) for h in heads]
        o_intra = [jnp.dot(attn[h], v[h], preferred_element_type=jnp.float32) for h in heads]
        upd = [_dot_tn(v[h], expand(kd[h])) for h in heads]
        o_inter = []
        for h in heads:
            st = st_scr[h]
            states = []
            for c in range(GLA_NCS):
                states.append(st.astype(bf16))
                st = st * dec[h][c] + upd[h][:, c * B_DK:(c + 1) * B_DK]
            st_scr[h] = st
            o_inter.append(_dot_nt(expand(qe[h]), jnp.concatenate(states, axis=1)))
        for h in heads:
            o = o_intra[h] + o_inter[h]
            on = o * lax.rsqrt(jnp.mean(o * o, axis=-1, keepdims=True) + EPS) * gain
            g = bg_ref[0, rows, vs[h]]
            o_ref[0, rows, vs[h]] = (on * (g * jax.nn.sigmoid(g))).astype(o_ref.dtype)
        return carry

    lax.fori_loop(0, GLA_CT // GLA_SC, body, 0)


def _gla(proj_f, w_alpha_up, b_alpha, gla_g):
    return pl.pallas_call(
        _gla_kernel,
        grid=(BATCH, SEQ // GLA_CT),
        in_specs=[
            pl.BlockSpec((1, GLA_CT, B_KEY_WIDTH), lambda b, t: (b, t, F32_OFF["bq"] // B_KEY_WIDTH)),
            pl.BlockSpec((1, GLA_CT, B_KEY_WIDTH), lambda b, t: (b, t, F32_OFF["bk"] // B_KEY_WIDTH)),
            pl.BlockSpec((1, GLA_CT, B_WIDTH), lambda b, t: (b, t, F32_OFF["bv"] // B_WIDTH)),
            pl.BlockSpec((1, GLA_CT, B_WIDTH), lambda b, t: (b, t, F32_OFF["bg"] // B_WIDTH)),
            pl.BlockSpec((1, GLA_CT, LANE), lambda b, t: (b, t, F32_OFF["ba"] // LANE)),
            pl.BlockSpec((GATE_RANK, B_KEY_WIDTH), lambda b, t: (0, 0)),
            pl.BlockSpec((1, B_KEY_WIDTH), lambda b, t: (0, 0)),
            pl.BlockSpec((1, B_DV), lambda b, t: (0, 0)),
        ],
        out_specs=pl.BlockSpec((1, GLA_CT, B_WIDTH), lambda b, t: (b, t, 0)),
        out_shape=jax.ShapeDtypeStruct((BATCH, SEQ, B_WIDTH), jnp.bfloat16),
        scratch_shapes=[pltpu.VMEM((B_HEADS, B_DV, B_DK), jnp.float32)],
        compiler_params=pltpu.CompilerParams(
            dimension_semantics=("arbitrary", "arbitrary"), vmem_limit_bytes=VMEM_LIMIT),
        name="gla",
    )(proj_f, proj_f, proj_f, proj_f, proj_f, w_alpha_up, b_alpha.reshape(1, B_KEY_WIDTH),
      gla_g.reshape(1, B_DV))


OUT_TM = 512


def _out_kernel(a_ref, b_ref, wa_ref, wb_ref, x_ref, gate_ref, fg_ref, o_ref, *, final_norm):
    y = (jnp.dot(a_ref[...], wa_ref[...], preferred_element_type=jnp.float32)
         + jnp.dot(b_ref[...], wb_ref[...], preferred_element_type=jnp.float32))
    xn = x_ref[...] + gate_ref[0] * y
    if final_norm:
        r = xn * lax.rsqrt(jnp.mean(xn * xn, axis=-1, keepdims=True) + EPS)
        xn = r * fg_ref[...]
    o_ref[...] = xn


def _out_proj(a_out, b_out, w_out_bf, x2d, gate, final_g, layer, final_norm):
    m = x2d.shape[0]
    tiles_per_batch = SEQ // OUT_TM
    return pl.pallas_call(
        functools.partial(_out_kernel, final_norm=final_norm),
        grid=(m // OUT_TM,),
        in_specs=[
            pl.BlockSpec((OUT_TM, A_WIDTH), lambda i: (i, 0)),
            pl.BlockSpec((OUT_TM, B_WIDTH), lambda i: (i, 0)),
            pl.BlockSpec((None, A_WIDTH, D_MODEL), lambda i: (layer, 0, 0)),
            pl.BlockSpec((None, B_WIDTH, D_MODEL), lambda i: (layer, 1, 0)),
            pl.BlockSpec((OUT_TM, D_MODEL), lambda i: (i, 0)),
            pl.BlockSpec((1, 1, D_MODEL), lambda i: (i // tiles_per_batch, 0, 0)),
            pl.BlockSpec((1, D_MODEL), lambda i: (0, 0)),
        ],
        out_specs=pl.BlockSpec((OUT_TM, D_MODEL), lambda i: (i, 0)),
        out_shape=jax.ShapeDtypeStruct((m, D_MODEL), jnp.float32),
        compiler_params=pltpu.CompilerParams(
            dimension_semantics=("arbitrary",), vmem_limit_bytes=VMEM_LIMIT),
        name="out_proj",
    )(a_out, b_out, w_out_bf, w_out_bf, x2d, gate, final_g.reshape(1, D_MODEL))


def kernel(x, c, w_ada, b_ada, norm_g, w_in, w_alpha_up, b_alpha, gla_g, w_out, rel_bias, final_g):
    mod = _adaln_mod(c, w_ada, b_ada)
    band = _bias_band(rel_bias)
    w_in_p = _pack_w_in(jnp.swapaxes(w_in, 1, 2))
    w_out_bf = w_out.astype(jnp.bfloat16)
    x2d = x.reshape(BATCH * SEQ, D_MODEL)
    for l in range(DEPTH):
        shift = mod[l, :, 0:D_MODEL].reshape(BATCH, 1, D_MODEL)
        scale = mod[l, :, D_MODEL:2 * D_MODEL].reshape(BATCH, 1, D_MODEL)
        gate = mod[l, :, 2 * D_MODEL:].reshape(BATCH, 1, D_MODEL)
        proj_b, proj_f = _norm_proj(x2d, shift, scale, norm_g[l], w_in_p, l)
        proj_b = proj_b.reshape(BATCH, SEQ, NB_COLS)
        proj_f = proj_f.reshape(BATCH, SEQ, NF_COLS)
        a_out = _dsa_attention(proj_b, proj_f, band)
        b_out = _gla(proj_f, w_alpha_up[l], b_alpha[l], gla_g[l])
        x2d = _out_proj(a_out.reshape(BATCH * SEQ, A_WIDTH), b_out.reshape(BATCH * SEQ, B_WIDTH),
                        w_out_bf, x2d, gate, final_g, l, final_norm=(l == DEPTH - 1))
    return x2d.reshape(BATCH, SEQ, D_MODEL)
```

```python
import functools
import math

import numpy as np
import jax
import jax.numpy as jnp
from jax import lax
from jax.experimental import pallas as pl
from jax.experimental.pallas import tpu as pltpu

D_MODEL = 2048
BATCH = 4
SEQ = 2048
DEPTH = 4
A_WIDTH = 1024
A_HEADS = 8
A_HEAD_DIM = 128
IDX_HEADS = 16
IDX_DIM = 64
TOPK = min(256, SEQ // 4)
B_WIDTH = 1024
B_HEADS = 4
B_KEY_WIDTH = 512
B_DK = 128
B_DV = 256
GATE_RANK = 16
GATE_TEMP = 16.0
NUM_BUCKETS = 32
MAX_DISTANCE = 128
EPS = 1e-6

IN_WIDTHS = (A_WIDTH, A_WIDTH, A_WIDTH, A_WIDTH, IDX_HEADS * IDX_DIM, IDX_DIM, IDX_HEADS,
             B_KEY_WIDTH, B_KEY_WIDTH, B_WIDTH, B_WIDTH, GATE_RANK)
IN_NAMES = ("aq", "ak", "av", "ag", "iq", "ik", "iw", "bq", "bk", "bv", "bg", "ba")
IN_OFFSETS = dict(zip(IN_NAMES, np.concatenate([[0], np.cumsum(IN_WIDTHS)[:-1]]).tolist()))
IN_WIDTH_OF = dict(zip(IN_NAMES, IN_WIDTHS))
IN_COLS = sum(IN_WIDTHS)

LANE = 128
VMEM_LIMIT = 52 * 1024 * 1024

PROJ_TN = 512
BF_SEGS = (("aq", 1024), ("ak", 1024), ("av", 1024))
F32_SEGS = (("ag", 1024), ("iq", 1024), ("bv", 1024), ("bg", 1024), ("bq", 512), ("bk", 512),
            ("ik", LANE), ("iw", LANE), ("ba", LANE))
NB_COLS = sum(w for _, w in BF_SEGS)
NF_USED = sum(w for _, w in F32_SEGS)
NF_COLS = -(-NF_USED // PROJ_TN) * PROJ_TN
NP_COLS = NB_COLS + NF_COLS


def _seg_offsets(segs):
    offs, o = {}, 0
    for name, w in segs:
        offs[name] = o
        o += w
    return offs


BF_OFF = _seg_offsets(BF_SEGS)
F32_OFF = _seg_offsets(F32_SEGS)

INT_MIN = -2 ** 31
NEG_BIG = -1e30
LOG2E = math.log2(math.e)
AQ_SCALE = A_HEAD_DIM ** -0.5 * LOG2E
I16_MIN = -2 ** 15
PACK16 = 16


PACK_COLS = 256


def _pack_kernel(w_ref, o_ref):
    dst = 0
    for name, width in BF_SEGS + F32_SEGS:
        src, used = IN_OFFSETS[name], IN_WIDTH_OF[name]
        o_ref[0, dst:dst + used, :] = w_ref[0, src:src + used, :].astype(jnp.bfloat16)
        if width > used:
            o_ref[0, dst + used:dst + width, :] = jnp.zeros((width - used, PACK_COLS), jnp.bfloat16)
        dst += width
    if dst < NP_COLS:
        o_ref[0, dst:, :] = jnp.zeros((NP_COLS - dst, PACK_COLS), jnp.bfloat16)


def _pack_w_in(w_in_t):
    return pl.pallas_call(
        _pack_kernel,
        grid=(DEPTH, D_MODEL // PACK_COLS),
        in_specs=[pl.BlockSpec((1, IN_COLS, PACK_COLS), lambda l, c: (l, 0, c))],
        out_specs=pl.BlockSpec((1, NP_COLS, PACK_COLS), lambda l, c: (l, 0, c)),
        out_shape=jax.ShapeDtypeStruct((DEPTH, NP_COLS, D_MODEL), jnp.bfloat16),
        compiler_params=pltpu.CompilerParams(
            dimension_semantics=("arbitrary", "arbitrary"), vmem_limit_bytes=VMEM_LIMIT),
        name="pack_w_in",
    )(w_in_t)


MOD_TN = 768


def _mod_kernel(c_ref, w_ref, b_ref, o_ref):
    c = c_ref[...]
    c_act = c * jax.nn.sigmoid(c)
    acc = jnp.dot(c_act.astype(jnp.bfloat16), w_ref[0].astype(jnp.bfloat16),
                  preferred_element_type=jnp.float32)
    o_ref[0] = acc + b_ref[0]


def _adaln_mod(c, w_ada, b_ada):
    cp = jnp.pad(c, ((0, 8 - BATCH), (0, 0)))
    out = pl.pallas_call(
        _mod_kernel,
        grid=(DEPTH, 3 * D_MODEL // MOD_TN),
        in_specs=[
            pl.BlockSpec((8, D_MODEL), lambda l, j: (0, 0)),
            pl.BlockSpec((1, D_MODEL, MOD_TN), lambda l, j: (l, 0, j)),
            pl.BlockSpec((1, 1, MOD_TN), lambda l, j: (l, 0, j)),
        ],
        out_specs=pl.BlockSpec((1, 8, MOD_TN), lambda l, j: (l, 0, j)),
        out_shape=jax.ShapeDtypeStruct((DEPTH, 8, 3 * D_MODEL), jnp.float32),
        compiler_params=pltpu.CompilerParams(
            dimension_semantics=("arbitrary", "arbitrary"), vmem_limit_bytes=VMEM_LIMIT),
        name="adaln_mod",
    )(cp, w_ada, b_ada.reshape(DEPTH, 1, 3 * D_MODEL))
    return out[:, :BATCH]


PROJ_TM = 1024
NB_TILES = NB_COLS // PROJ_TN
N_COL_TILES = NP_COLS // PROJ_TN
AHEAD_ROWS = 64
assert PROJ_TM // AHEAD_ROWS < N_COL_TILES


def _proj_kernel(x_ref, shift_ref, scale_ref, g_ref, w_ref, ob_ref, of_ref, ha_scr, hb_scr):
    i, j = pl.program_id(0), pl.program_id(1)

    def normed(x):
        r = x * lax.rsqrt(jnp.mean(x * x, axis=-1, keepdims=True) + EPS)
        return ((r * g_ref[...]) * (1.0 + scale_ref[0]) + shift_ref[0]).astype(jnp.bfloat16)

    @pl.when((i == 0) & (j == 0))
    def _():
        ha_scr[...] = normed(x_ref[...])

    def step(h_cur, h_next):
        def norm_ahead():
            c = jnp.clip(j - (i == 0).astype(jnp.int32), 0, PROJ_TM // AHEAD_ROWS - 1)
            rows = pl.ds(pl.multiple_of(c * AHEAD_ROWS, AHEAD_ROWS), AHEAD_ROWS)
            h_next[rows, :] = normed(x_ref[rows, :])

        @pl.when(j < NB_TILES)
        def _():
            norm_ahead()
            out_scale = jnp.where(j < A_WIDTH // PROJ_TN, AQ_SCALE, 1.0)
            ob_ref[...] = (_dot_nt(h_cur[...], w_ref[...]) * out_scale).astype(jnp.bfloat16)

        @pl.when(j >= NB_TILES)
        def _():
            norm_ahead()
            of_ref[...] = _dot_nt(h_cur[...], w_ref[...])

    @pl.when(i % 2 == 0)
    def _():
        step(ha_scr, hb_scr)

    @pl.when(i % 2 == 1)
    def _():
        step(hb_scr, ha_scr)


def _norm_proj(x2d, shift, scale, norm_g, w_p, layer):
    m = x2d.shape[0]
    n_row_tiles = m // PROJ_TM
    tiles_per_batch = SEQ // PROJ_TM

    def ahead(i, j):
        return jnp.where((i == 0) & (j == 0), 0, jnp.minimum(i + 1, n_row_tiles - 1))

    return pl.pallas_call(
        _proj_kernel,
        grid=(n_row_tiles, N_COL_TILES),
        in_specs=[
            pl.BlockSpec((PROJ_TM, D_MODEL), lambda i, j: (ahead(i, j), 0)),
            pl.BlockSpec((1, 1, D_MODEL), lambda i, j: (ahead(i, j) // tiles_per_batch, 0, 0)),
            pl.BlockSpec((1, 1, D_MODEL), lambda i, j: (ahead(i, j) // tiles_per_batch, 0, 0)),
            pl.BlockSpec((1, D_MODEL), lambda i, j: (0, 0)),
            pl.BlockSpec((None, PROJ_TN, D_MODEL), lambda i, j: (layer, j, 0)),
        ],
        out_specs=[
            pl.BlockSpec((PROJ_TM, PROJ_TN), lambda i, j: (i, jnp.minimum(j, NB_TILES - 1))),
            pl.BlockSpec((PROJ_TM, PROJ_TN), lambda i, j: (i, jnp.maximum(j - NB_TILES, 0))),
        ],
        out_shape=[
            jax.ShapeDtypeStruct((m, NB_COLS), jnp.bfloat16),
            jax.ShapeDtypeStruct((m, NF_COLS), jnp.float32),
        ],
        scratch_shapes=[pltpu.VMEM((PROJ_TM, D_MODEL), jnp.bfloat16),
                        pltpu.VMEM((PROJ_TM, D_MODEL), jnp.bfloat16)],
        compiler_params=pltpu.CompilerParams(
            dimension_semantics=("arbitrary", "arbitrary"), vmem_limit_bytes=VMEM_LIMIT),
        name="norm_proj",
    )(x2d, shift, scale, norm_g.reshape(1, D_MODEL), w_p)


TQ = 256
KB = 256
NQ = SEQ // TQ
NKB = SEQ // KB
HALF = KB // 2
RUN_LENGTHS = tuple(1 << b for b in reversed(range(NKB.bit_length())))
COUNT_CHAINS = 4


def _t5_bucket_np(d):
    max_exact = NUM_BUCKETS // 2
    d = np.maximum(d, 0)
    df = np.maximum(d, 1).astype(np.float32)
    large = max_exact + (np.log(df / np.float32(max_exact)) / np.float32(math.log(MAX_DISTANCE / max_exact))
                         * np.float32(NUM_BUCKETS - max_exact)).astype(np.int32)
    large = np.minimum(large, NUM_BUCKETS - 1)
    return np.where(d < max_exact, d, large).astype(np.int32)


assert int(_t5_bucket_np(np.arange(KB, 2 * SEQ)).min()) == NUM_BUCKETS - 1


def _band_buckets():
    tl = np.arange(TQ)[:, None]
    u = np.arange(KB)[None, :]
    prev = _t5_bucket_np(KB + tl - u)
    diag = _t5_bucket_np(tl - u)
    return np.stack([prev, diag]).astype(np.int32)


BAND_ROWS = 64


def _band_kernel(rb_ref, bucket_ref, o_ref):
    h = pl.program_id(0)
    far = rb_ref[NUM_BUCKETS - 1, h]

    def rows(r, carry):
        rs = pl.ds(pl.multiple_of(r * BAND_ROWS, BAND_ROWS), BAND_ROWS)
        bucket = bucket_ref[0, rs, :]
        acc = jnp.zeros(bucket.shape, jnp.float32)
        for b in range(NUM_BUCKETS):
            acc = jnp.where(bucket == b, rb_ref[b, h] - far, acc)
        o_ref[0, 0, rs, :] = acc * LOG2E
        return carry

    lax.fori_loop(0, TQ // BAND_ROWS, rows, 0)


def _bias_band(rel_bias):
    return pl.pallas_call(
        _band_kernel,
        grid=(A_HEADS, 2),
        in_specs=[
            pl.BlockSpec(memory_space=pltpu.SMEM),
            pl.BlockSpec((1, TQ, KB), lambda h, s: (s, 0, 0)),
        ],
        out_specs=pl.BlockSpec((1, 1, TQ, KB), lambda h, s: (h, s, 0, 0)),
        out_shape=jax.ShapeDtypeStruct((A_HEADS, 2, TQ, KB), jnp.float32),
        name="bias_band",
    )(rel_bias, jnp.asarray(_band_buckets()))


def _dot_nt(a, b):
    return lax.dot_general(a, b, (((1,), (1,)), ((), ())), preferred_element_type=jnp.float32)


def _dot_tn(a, b):
    return lax.dot_general(a, b, (((0,), (0,)), ((), ())), preferred_element_type=jnp.float32)


def _dsa_kernel(q_ref, k_ref, v_ref, iq_ref, ik_ref, iw_ref, ag_ref, band_ref, o_ref,
                iklo_scr, ikhi_scr, vext_scr, wb_scr, iq2_scr, key_scr, khi_scr, klo_scr, thrn_scr,
                cutn_scr, cnt_scr, madd_scr, m_scr, accv_scr, accl_scr):
    qi = pl.program_id(1)
    nkb = qi + 1
    bf16 = jnp.bfloat16

    def key_rows(kb):
        return pl.ds(pl.multiple_of(kb * KB, KB), KB)

    @pl.when(qi == 0)
    def _():
        def prep(c, carry):
            rs = key_rows(c)
            ik = ik_ref[0, rs, :]
            iklo_scr[rs, :] = ik.astype(bf16)
            ikhi_scr[rs, :] = pltpu.roll(ik, IDX_DIM, axis=1).astype(bf16)
            for h in range(A_HEADS):
                vext_scr[h, rs, :A_HEAD_DIM] = v_ref[0, rs, h * A_HEAD_DIM:(h + 1) * A_HEAD_DIM]
                vext_scr[h, rs, A_HEAD_DIM:] = jnp.ones((KB, A_HEAD_DIM), bf16)
            return carry

        lax.fori_loop(0, NKB, prep, 0)

    iw = iw_ref[0][:, :IDX_HEADS] * (IDX_HEADS ** -0.5 * IDX_DIM ** -0.5)
    for j in range(IDX_HEADS):
        wb_scr[j] = jnp.broadcast_to(iw[:, j:j + 1], (TQ, HALF))
    for jp in range(IDX_HEADS // 2):
        iq2_scr[jp * TQ:(jp + 1) * TQ, :] = iq_ref[0, :, jp * LANE:(jp + 1) * LANE].astype(bf16)

    row = lax.broadcasted_iota(jnp.int32, (TQ, HALF), 0)
    col = lax.broadcasted_iota(jnp.int32, (TQ, HALF), 1)

    def score_block(kb, carry):
        rs = key_rows(kb)
        s_lo = _dot_nt(iq2_scr[...], iklo_scr[rs, :])
        s_hi = _dot_nt(iq2_scr[...], ikhi_scr[rs, :])
        for half in range(2):
            ls = slice(half * HALF, (half + 1) * HALF)
            acc = jnp.zeros((TQ, HALF), jnp.float32)
            for jp in range(IDX_HEADS // 2):
                rj = slice(jp * TQ, (jp + 1) * TQ)
                acc = (acc + jnp.maximum(s_lo[rj, ls], 0.0) * wb_scr[2 * jp]
                       + jnp.maximum(s_hi[rj, ls], 0.0) * wb_scr[2 * jp + 1])
            bits = pltpu.bitcast(acc + 0.0, jnp.int32)
            key = jnp.where(bits >= 0, bits, bits ^ 0x7FFFFFFF)
            causal = (kb * KB + half * HALF + col) <= (qi * TQ + row)
            key_scr[kb, :, ls] = jnp.where(causal, key, INT_MIN)
        keyt = key_scr[kb].T
        khi_scr[kb] = lax.shift_right_arithmetic(keyt, 16).astype(jnp.int16)
        klo_scr[kb] = ((keyt & 0xFFFF) + I16_MIN).astype(jnp.int16)
        return carry

    lax.fori_loop(0, nkb, score_block, 0)

    def count(ref, cand, strictly=False):
        c16 = jnp.broadcast_to(cand.astype(jnp.int16), (PACK16, TQ))
        cnt_scr[...] = jnp.zeros((PACK16, TQ), jnp.int16)

        def run(base, n):
            parts = [jnp.zeros((PACK16, TQ), jnp.int16) for _ in range(COUNT_CHAINS)]
            for t in range(n):
                v = ref[base + t]
                for r in range(KB // PACK16):
                    blk = v[r * PACK16:(r + 1) * PACK16]
                    hit = (blk > c16) if strictly else (blk >= c16)
                    parts[r % COUNT_CHAINS] = parts[r % COUNT_CHAINS] + jnp.where(hit, jnp.int16(1), jnp.int16(0))
            cnt_scr[...] = cnt_scr[...] + functools.reduce(lambda a, b: a + b, parts)

        base = jnp.int32(0)
        for n in RUN_LENGTHS:
            take = (nkb & n) != 0
            pl.when(take)(functools.partial(run, base, n))
            base = base + jnp.where(take, n, 0)
        return jnp.sum(cnt_scr[...].astype(jnp.int32), axis=0, keepdims=True)

    def bisect16(ref, need):
        zero = jnp.zeros((1, TQ), jnp.int32)
        prefix = jnp.where(count(ref, zero) >= need, zero, I16_MIN)

        def bit_pass(it, prefix):
            cand = prefix + lax.shift_left(jnp.int32(1), 14 - it)
            return jnp.where(count(ref, cand) >= need, cand, prefix)

        return lax.fori_loop(0, 15, bit_pass, prefix)

    topk = jnp.full((1, TQ), TOPK, jnp.int32)
    p_hi = bisect16(khi_scr, topk)
    above = count(khi_scr, p_hi, strictly=True)
    p_hi16 = jnp.broadcast_to(p_hi.astype(jnp.int16), (PACK16, TQ))

    def narrow(kb, carry):
        hi, lo = khi_scr[kb], klo_scr[kb]
        for r in range(KB // PACK16):
            rs = slice(r * PACK16, (r + 1) * PACK16)
            klo_scr[kb, rs, :] = jnp.where(hi[rs] == p_hi16, lo[rs], jnp.int16(I16_MIN))
        return carry

    lax.fori_loop(0, nkb, narrow, 0)
    p_lo = bisect16(klo_scr, topk - above)
    prefix = p_hi * 65536 + (p_lo - I16_MIN)
    thr = jnp.maximum(prefix, INT_MIN + 1)
    thrn_scr[...] = jnp.broadcast_to(thr, (KB, TQ)).T

    p_lo16 = jnp.broadcast_to(p_lo.astype(jnp.int16), (PACK16, TQ))
    sub16 = lax.broadcasted_iota(jnp.int32, (PACK16, TQ), 0)

    def tie_blocks(kb, fn, carry):
        hi, lo = khi_scr[kb], klo_scr[kb]
        for r in range(KB // PACK16):
            rs = slice(r * PACK16, (r + 1) * PACK16)
            carry = fn(kb, r, rs, (hi[rs] == p_hi16) & (lo[rs] == p_lo16), carry)
        return carry

    def count_tied_and_above(kb, cnts):
        n_tied, n_above = cnts
        hi, lo = khi_scr[kb], klo_scr[kb]
        one, nil = jnp.int16(1), jnp.int16(0)
        for r in range(KB // PACK16):
            rs = slice(r * PACK16, (r + 1) * PACK16)
            n_tied = n_tied + jnp.where((hi[rs] == p_hi16) & (lo[rs] == p_lo16), one, nil)
            n_above = n_above + jnp.where(lo[rs] > p_lo16, one, nil)
        return n_tied, n_above

    zeros16 = jnp.zeros((PACK16, TQ), jnp.int16)
    tied, above_lo = lax.fori_loop(0, nkb, count_tied_and_above, (zeros16, zeros16))
    tied = jnp.sum(tied.astype(jnp.int32), axis=0, keepdims=True)
    above_lo = jnp.sum(above_lo.astype(jnp.int32), axis=0, keepdims=True)
    room = topk - above - above_lo
    trim = (tied > room) & (prefix != INT_MIN)

    def madd_block(kb, carry):
        madd_scr[kb] = jnp.where(key_scr[kb] >= thrn_scr[...], 0.0, NEG_BIG)
        return carry

    lax.fori_loop(0, nkb, madd_block, 0)

    @pl.when(jnp.max(trim.astype(jnp.int32)) > 0)
    def _():
        def mark(kb, carry):
            def put(kb, r, rs, tied, c):
                neg_pos = (-(kb * KB + r * PACK16) - sub16).astype(jnp.int16)
                khi_scr[kb, rs, :] = jnp.where(tied, neg_pos, jnp.int16(I16_MIN))
                return c
            return tie_blocks(kb, put, carry)

        lax.fori_loop(0, nkb, mark, 0)
        cut = jnp.where(trim, -bisect16(khi_scr, room), SEQ)
        cutn_scr[...] = jnp.broadcast_to(cut, (KB, TQ)).T

        def madd_ties(kb, carry):
            key = key_scr[kb]
            pos = kb * KB + lax.broadcasted_iota(jnp.int32, (TQ, KB), 1)
            keep_tie = jnp.where(pos <= cutn_scr[...], 0.0, NEG_BIG)
            madd_scr[kb] = jnp.where(key > thrn_scr[...], 0.0,
                                     jnp.where(key == thrn_scr[...], keep_tie, NEG_BIG))
            return carry

        lax.fori_loop(0, nkb, madd_ties, 0)

    m_scr[...] = jnp.full(m_scr.shape, NEG_BIG, jnp.float32)
    accv_scr[...] = jnp.zeros_like(accv_scr)
    accl_scr[...] = jnp.zeros_like(accl_scr)

    def attend(kb, slot):
        rs = key_rows(kb)
        for h in range(A_HEADS):
            hs = slice(h * A_HEAD_DIM, (h + 1) * A_HEAD_DIM)
            lg = _dot_nt(q_ref[0, :, hs], k_ref[0, rs, hs]) + madd_scr[kb]
            if slot is not None:
                lg = lg + band_ref[h, slot]
            m_old = m_scr[h]
            m_new = jnp.maximum(m_old, jnp.max(lg, axis=-1, keepdims=True))
            alpha = jnp.exp2(m_old - m_new)
            p = jnp.concatenate([jnp.exp2(lg[:, :HALF] - m_new), jnp.exp2(lg[:, HALF:] - m_new)], axis=1)
            pv = jnp.dot(p.astype(bf16), vext_scr[h, rs, :], preferred_element_type=jnp.float32)
            accv_scr[h] = accv_scr[h] * alpha + pv[:, :A_HEAD_DIM]
            accl_scr[h] = accl_scr[h] * alpha + pv[:, A_HEAD_DIM:]
            m_scr[h] = m_new

    def far_block(kb, carry):
        attend(kb, None)
        return carry

    lax.fori_loop(0, qi - 1, far_block, 0)

    @pl.when(qi >= 1)
    def _():
        attend(qi - 1, 0)

    attend(qi, 1)

    for h in range(A_HEADS):
        hs = slice(h * A_HEAD_DIM, (h + 1) * A_HEAD_DIM)
        g = ag_ref[0, :, hs]
        o_ref[0, :, hs] = (accv_scr[h] / accl_scr[h] * (g * jax.nn.sigmoid(g))).astype(o_ref.dtype)


def _dsa_attention(proj_b, proj_f, band):
    cb = lambda name: BF_OFF[name] // A_WIDTH
    once = pl.Buffered(1)
    return pl.pallas_call(
        _dsa_kernel,
        grid=(BATCH, NQ),
        in_specs=[
            pl.BlockSpec((1, TQ, A_WIDTH), lambda b, i: (b, i, cb("aq"))),
            pl.BlockSpec((1, SEQ, A_WIDTH), lambda b, i: (b, 0, cb("ak")), pipeline_mode=once),
            pl.BlockSpec((1, SEQ, A_WIDTH), lambda b, i: (b, 0, cb("av")), pipeline_mode=once),
            pl.BlockSpec((1, TQ, A_WIDTH), lambda b, i: (b, i, F32_OFF["iq"] // A_WIDTH)),
            pl.BlockSpec((1, SEQ, LANE), lambda b, i: (b, 0, F32_OFF["ik"] // LANE), pipeline_mode=once),
            pl.BlockSpec((1, TQ, LANE), lambda b, i: (b, i, F32_OFF["iw"] // LANE)),
            pl.BlockSpec((1, TQ, A_WIDTH), lambda b, i: (b, i, F32_OFF["ag"] // A_WIDTH)),
            pl.BlockSpec((A_HEADS, 2, TQ, KB), lambda b, i: (0, 0, 0, 0), pipeline_mode=once),
        ],
        out_specs=pl.BlockSpec((1, TQ, A_WIDTH), lambda b, i: (b, i, 0)),
        out_shape=jax.ShapeDtypeStruct((BATCH, SEQ, A_WIDTH), jnp.bfloat16),
        scratch_shapes=[
            pltpu.VMEM((SEQ, LANE), jnp.bfloat16),
            pltpu.VMEM((SEQ, LANE), jnp.bfloat16),
            pltpu.VMEM((A_HEADS, SEQ, 2 * A_HEAD_DIM), jnp.bfloat16),
            pltpu.VMEM((IDX_HEADS, TQ, HALF), jnp.float32),
            pltpu.VMEM((IDX_HEADS // 2 * TQ, LANE), jnp.bfloat16),
            pltpu.VMEM((NKB, TQ, KB), jnp.int32),
            pltpu.VMEM((NKB, KB, TQ), jnp.int16),
            pltpu.VMEM((NKB, KB, TQ), jnp.int16),
            pltpu.VMEM((TQ, KB), jnp.int32),
            pltpu.VMEM((TQ, KB), jnp.int32),
            pltpu.VMEM((PACK16, TQ), jnp.int16),
            pltpu.VMEM((NKB, TQ, KB), jnp.float32),
            pltpu.VMEM((A_HEADS, TQ, HALF), jnp.float32),
            pltpu.VMEM((A_HEADS, TQ, A_HEAD_DIM), jnp.float32),
            pltpu.VMEM((A_HEADS, TQ, A_HEAD_DIM), jnp.float32),
        ],
        compiler_params=pltpu.CompilerParams(
            dimension_semantics=("arbitrary", "arbitrary"), vmem_limit_bytes=VMEM_LIMIT),
        name="dsa_attention",
    )(proj_b, proj_b, proj_b, proj_f, proj_f, proj_f, proj_f, band)


GLA_CT = 512
GLA_C = 64
GLA_SC = 256
GLA_NCS = GLA_SC // GLA_C


def _gla_kernel(bq_ref, bk_ref, bv_ref, bg_ref, ba_ref, wup_ref, balpha_ref, g_ref, o_ref, st_scr):
    @pl.when(pl.program_id(1) == 0)
    def _():
        st_scr[...] = jnp.zeros_like(st_scr)

    bf16 = jnp.bfloat16
    rr = lax.broadcasted_iota(jnp.int32, (GLA_SC, GLA_SC), 0)
    cc = lax.broadcasted_iota(jnp.int32, (GLA_SC, GLA_SC), 1)
    tri = (rr >= cc) & (rr // GLA_C == cc // GLA_C)
    tri_bf = tri.astype(bf16)
    er = lax.broadcasted_iota(jnp.int32, (GLA_SC, GLA_NCS * B_DK), 0)
    ec = lax.broadcasted_iota(jnp.int32, (GLA_SC, GLA_NCS * B_DK), 1)
    own_block = (er // GLA_C) == (ec // B_DK)
    wup = wup_ref[...].astype(bf16)
    balpha = balpha_ref[...]
    gain = g_ref[...]
    heads = range(B_HEADS)

    def expand(a):
        return jnp.where(own_block, jnp.concatenate([a] * GLA_NCS, axis=1), jnp.zeros((), a.dtype))

    def body(sc, carry):
        rows = pl.ds(pl.multiple_of(sc * GLA_SC, GLA_SC), GLA_SC)
        ba = ba_ref[0, rows, :][:, :GATE_RANK].astype(bf16)
        pre = jnp.dot(ba, wup, preferred_element_type=jnp.float32) + balpha
        log_a = (jnp.minimum(pre, 0.0) - jnp.log1p(jnp.exp(-jnp.abs(pre)))) * (1.0 / GATE_TEMP)
        la_hi = log_a.astype(bf16)
        la_lo = (log_a - la_hi.astype(jnp.float32)).astype(bf16)
        bcum = (jnp.dot(tri_bf, la_hi, preferred_element_type=jnp.float32)
                + jnp.dot(tri_bf, la_lo, preferred_element_type=jnp.float32))
        ks = [slice(h * B_DK, (h + 1) * B_DK) for h in heads]
        vs = [slice(h * B_DV, (h + 1) * B_DV) for h in heads]
        qe, ke, kd, dec, v = [], [], [], [], []
        for h in heads:
            b = bcum[:, ks[h]]
            b_end = b.reshape(GLA_NCS, GLA_C, B_DK)[:, GLA_C - 1:GLA_C, :]
            b_end_rows = jnp.broadcast_to(b_end, (GLA_NCS, GLA_C, B_DK)).reshape(GLA_SC, B_DK)
            q = bq_ref[0, rows, ks[h]] * (B_DK ** -0.5)
            k = bk_ref[0, rows, ks[h]]
            qe.append((q * jnp.exp(b)).astype(bf16))
            ke.append((k * jnp.exp(-b)).astype(bf16))
            kd.append((k * jnp.exp(b_end_rows - b)).astype(bf16))
            dec.append(jnp.exp(b_end))
            v.append(bv_ref[0, rows, vs[h]].astype(bf16))
        attn = [jnp.where(tri, _dot_nt(qe[h], ke[h]), 0.0).astype(bf16) for h in heads]
        o_intra = [jnp.dot(attn[h], v[h], preferred_element_type=jnp.float32) for h in heads]
        upd = [_dot_tn(v[h], expand(kd[h])) for h in heads]
        o_inter = []
        for h in heads:
            st = st_scr[h]
            states = []
            for c in range(GLA_NCS):
                states.append(st.astype(bf16))
                st = st * dec[h][c] + upd[h][:, c * B_DK:(c + 1) * B_DK]
            st_scr[h] = st
            o_inter.append(_dot_nt(expand(qe[h]), jnp.concatenate(states, axis=1)))
        for h in heads:
            o = o_intra[h] + o_inter[h]
            on = o * lax.rsqrt(jnp.mean(o * o, axis=-1, keepdims=True) + EPS) * gain
            g = bg_ref[0, rows, vs[h]]
            o_ref[0, rows, vs[h]] = (on * (g * jax.nn.sigmoid(g))).astype(o_ref.dtype)
        return carry

    lax.fori_loop(0, GLA_CT // GLA_SC, body, 0)


def _gla(proj_f, w_alpha_up, b_alpha, gla_g):
    return pl.pallas_call(
        _gla_kernel,
        grid=(BATCH, SEQ // GLA_CT),
        in_specs=[
            pl.BlockSpec((1, GLA_CT, B_KEY_WIDTH), lambda b, t: (b, t, F32_OFF["bq"] // B_KEY_WIDTH)),
            pl.BlockSpec((1, GLA_CT, B_KEY_WIDTH), lambda b, t: (b, t, F32_OFF["bk"] // B_KEY_WIDTH)),
            pl.BlockSpec((1, GLA_CT, B_WIDTH), lambda b, t: (b, t, F32_OFF["bv"] // B_WIDTH)),
            pl.BlockSpec((1, GLA_CT, B_WIDTH), lambda b, t: (b, t, F32_OFF["bg"] // B_WIDTH)),
            pl.BlockSpec((1, GLA_CT, LANE), lambda b, t: (b, t, F32_OFF["ba"] // LANE)),
            pl.BlockSpec((GATE_RANK, B_KEY_WIDTH), lambda b, t: (0, 0)),
            pl.BlockSpec((1, B_KEY_WIDTH), lambda b, t: (0, 0)),
            pl.BlockSpec((1, B_DV), lambda b, t: (0, 0)),
        ],
        out_specs=pl.BlockSpec((1, GLA_CT, B_WIDTH), lambda b, t: (b, t, 0)),
        out_shape=jax.ShapeDtypeStruct((BATCH, SEQ, B_WIDTH), jnp.bfloat16),
        scratch_shapes=[pltpu.VMEM((B_HEADS, B_DV, B_DK), jnp.float32)],
        compiler_params=pltpu.CompilerParams(
            dimension_semantics=("arbitrary", "arbitrary"), vmem_limit_bytes=VMEM_LIMIT),
        name="gla",
    )(proj_f, proj_f, proj_f, proj_f, proj_f, w_alpha_up, b_alpha.reshape(1, B_KEY_WIDTH),
      gla_g.reshape(1, B_DV))


OUT_TM = 512


def _out_kernel(a_ref, b_ref, wa_ref, wb_ref, x_ref, gate_ref, fg_ref, o_ref, *, final_norm):
    y = (jnp.dot(a_ref[...], wa_ref[...], preferred_element_type=jnp.float32)
         + jnp.dot(b_ref[...], wb_ref[...], preferred_element_type=jnp.float32))
    xn = x_ref[...] + gate_ref[0] * y
    if final_norm:
        r = xn * lax.rsqrt(jnp.mean(xn * xn, axis=-1, keepdims=True) + EPS)
        xn = r * fg_ref[...]
    o_ref[...] = xn


def _out_proj(a_out, b_out, w_out_bf, x2d, gate, final_g, layer, final_norm):
    m = x2d.shape[0]
    tiles_per_batch = SEQ // OUT_TM
    return pl.pallas_call(
        functools.partial(_out_kernel, final_norm=final_norm),
        grid=(m // OUT_TM,),
        in_specs=[
            pl.BlockSpec((OUT_TM, A_WIDTH), lambda i: (i, 0)),
            pl.BlockSpec((OUT_TM, B_WIDTH), lambda i: (i, 0)),
            pl.BlockSpec((None, A_WIDTH, D_MODEL), lambda i: (layer, 0, 0)),
            pl.BlockSpec((None, B_WIDTH, D_MODEL), lambda i: (layer, 1, 0)),
            pl.BlockSpec((OUT_TM, D_MODEL), lambda i: (i, 0)),
            pl.BlockSpec((1, 1, D_MODEL), lambda i: (i // tiles_per_batch, 0, 0)),
            pl.BlockSpec((1, D_MODEL), lambda i: (0, 0)),
        ],
        out_specs=pl.BlockSpec((OUT_TM, D_MODEL), lambda i: (i, 0)),
        out_shape=jax.ShapeDtypeStruct((m, D_MODEL), jnp.float32),
        compiler_params=pltpu.CompilerParams(
            dimension_semantics=("arbitrary",), vmem_limit_bytes=VMEM_LIMIT),
        name="out_proj",
    )(a_out, b_out, w_out_bf, w_out_bf, x2d, gate, final_g.reshape(1, D_MODEL))


def kernel(x, c, w_ada, b_ada, norm_g, w_in, w_alpha_up, b_alpha, gla_g, w_out, rel_bias, final_g):
    mod = _adaln_mod(c, w_ada, b_ada)
    band = _bias_band(rel_bias)
    w_in_p = _pack_w_in(jnp.swapaxes(w_in, 1, 2))
    w_out_bf = w_out.astype(jnp.bfloat16)
    x2d = x.reshape(BATCH * SEQ, D_MODEL)
    for l in range(DEPTH):
        shift = mod[l, :, 0:D_MODEL].reshape(BATCH, 1, D_MODEL)
        scale = mod[l, :, D_MODEL:2 * D_MODEL].reshape(BATCH, 1, D_MODEL)
        gate = mod[l, :, 2 * D_MODEL:].reshape(BATCH, 1, D_MODEL)
        proj_b, proj_f = _norm_proj(x2d, shift, scale, norm_g[l], w_in_p, l)
        proj_b = proj_b.reshape(BATCH, SEQ, NB_COLS)
        proj_f = proj_f.reshape(BATCH, SEQ, NF_COLS)
        a_out = _dsa_attention(proj_b, proj_f, band)
        b_out = _gla(proj_f, w_alpha_up[l], b_alpha[l], gla_g[l])
        x2d = _out_proj(a_out.reshape(BATCH * SEQ, A_WIDTH), b_out.reshape(BATCH * SEQ, B_WIDTH),
                        w_out_bf, x2d, gate, final_g, l, final_norm=(l == DEPTH - 1))
    return x2d.reshape(BATCH, SEQ, D_MODEL)
```

```python
import functools
import math

import numpy as np
import jax
import jax.numpy as jnp
from jax import lax
from jax.experimental import pallas as pl
from jax.experimental.pallas import tpu as pltpu

D_MODEL = 2048
BATCH = 4
SEQ = 2048
DEPTH = 4
A_WIDTH = 1024
A_HEADS = 8
A_HEAD_DIM = 128
IDX_HEADS = 16
IDX_DIM = 64
TOPK = min(256, SEQ // 4)
B_WIDTH = 1024
B_HEADS = 4
B_KEY_WIDTH = 512
B_DK = 128
B_DV = 256
GATE_RANK = 16
GATE_TEMP = 16.0
NUM_BUCKETS = 32
MAX_DISTANCE = 128
EPS = 1e-6

IN_WIDTHS = (A_WIDTH, A_WIDTH, A_WIDTH, A_WIDTH, IDX_HEADS * IDX_DIM, IDX_DIM, IDX_HEADS,
             B_KEY_WIDTH, B_KEY_WIDTH, B_WIDTH, B_WIDTH, GATE_RANK)
IN_NAMES = ("aq", "ak", "av", "ag", "iq", "ik", "iw", "bq", "bk", "bv", "bg", "ba")
IN_OFFSETS = dict(zip(IN_NAMES, np.concatenate([[0], np.cumsum(IN_WIDTHS)[:-1]]).tolist()))
IN_WIDTH_OF = dict(zip(IN_NAMES, IN_WIDTHS))
IN_COLS = sum(IN_WIDTHS)

LANE = 128
VMEM_LIMIT = 52 * 1024 * 1024

PROJ_TN = 512
BF_SEGS = (("aq", 1024), ("ak", 1024), ("av", 1024))
F32_SEGS = (("ag", 1024), ("iq", 1024), ("bv", 1024), ("bg", 1024), ("bq", 512), ("bk", 512),
            ("ik", LANE), ("iw", LANE), ("ba", LANE))
NB_COLS = sum(w for _, w in BF_SEGS)
NF_USED = sum(w for _, w in F32_SEGS)
NF_COLS = -(-NF_USED // PROJ_TN) * PROJ_TN
NP_COLS = NB_COLS + NF_COLS


def _seg_offsets(segs):
    offs, o = {}, 0
    for name, w in segs:
        offs[name] = o
        o += w
    return offs


BF_OFF = _seg_offsets(BF_SEGS)
F32_OFF = _seg_offsets(F32_SEGS)

INT_MIN = -2 ** 31
NEG_BIG = -1e30
LOG2E = math.log2(math.e)
AQ_SCALE = A_HEAD_DIM ** -0.5 * LOG2E
I16_MIN = -2 ** 15
PACK16 = 16


PACK_COLS = 256


def _pack_kernel(w_ref, o_ref):
    dst = 0
    for name, width in BF_SEGS + F32_SEGS:
        src, used = IN_OFFSETS[name], IN_WIDTH_OF[name]
        o_ref[0, dst:dst + used, :] = w_ref[0, src:src + used, :].astype(jnp.bfloat16)
        if width > used:
            o_ref[0, dst + used:dst + width, :] = jnp.zeros((width - used, PACK_COLS), jnp.bfloat16)
        dst += width
    if dst < NP_COLS:
        o_ref[0, dst:, :] = jnp.zeros((NP_COLS - dst, PACK_COLS), jnp.bfloat16)


def _pack_w_in(w_in_t):
    return pl.pallas_call(
        _pack_kernel,
        grid=(DEPTH, D_MODEL // PACK_COLS),
        in_specs=[pl.BlockSpec((1, IN_COLS, PACK_COLS), lambda l, c: (l, 0, c))],
        out_specs=pl.BlockSpec((1, NP_COLS, PACK_COLS), lambda l, c: (l, 0, c)),
        out_shape=jax.ShapeDtypeStruct((DEPTH, NP_COLS, D_MODEL), jnp.bfloat16),
        compiler_params=pltpu.CompilerParams(
            dimension_semantics=("arbitrary", "arbitrary"), vmem_limit_bytes=VMEM_LIMIT),
        name="pack_w_in",
    )(w_in_t)


MOD_TN = 768


def _mod_kernel(c_ref, w_ref, b_ref, o_ref):
    c = c_ref[...]
    c_act = c * jax.nn.sigmoid(c)
    acc = jnp.dot(c_act.astype(jnp.bfloat16), w_ref[0].astype(jnp.bfloat16),
                  preferred_element_type=jnp.float32)
    o_ref[0] = acc + b_ref[0]


def _adaln_mod(c, w_ada, b_ada):
    cp = jnp.pad(c, ((0, 8 - BATCH), (0, 0)))
    out = pl.pallas_call(
        _mod_kernel,
        grid=(DEPTH, 3 * D_MODEL // MOD_TN),
        in_specs=[
            pl.BlockSpec((8, D_MODEL), lambda l, j: (0, 0)),
            pl.BlockSpec((1, D_MODEL, MOD_TN), lambda l, j: (l, 0, j)),
            pl.BlockSpec((1, 1, MOD_TN), lambda l, j: (l, 0, j)),
        ],
        out_specs=pl.BlockSpec((1, 8, MOD_TN), lambda l, j: (l, 0, j)),
        out_shape=jax.ShapeDtypeStruct((DEPTH, 8, 3 * D_MODEL), jnp.float32),
        compiler_params=pltpu.CompilerParams(
            dimension_semantics=("arbitrary", "arbitrary"), vmem_limit_bytes=VMEM_LIMIT),
        name="adaln_mod",
    )(cp, w_ada, b_ada.reshape(DEPTH, 1, 3 * D_MODEL))
    return out[:, :BATCH]


PROJ_TM = 1024
NB_TILES = NB_COLS // PROJ_TN
N_COL_TILES = NP_COLS // PROJ_TN
AHEAD_ROWS = 64
assert PROJ_TM // AHEAD_ROWS < N_COL_TILES


def _proj_kernel(x_ref, shift_ref, scale_ref, g_ref, w_ref, ob_ref, of_ref, ha_scr, hb_scr):
    i, j = pl.program_id(0), pl.program_id(1)

    def normed(x):
        r = x * lax.rsqrt(jnp.mean(x * x, axis=-1, keepdims=True) + EPS)
        return ((r * g_ref[...]) * (1.0 + scale_ref[0]) + shift_ref[0]).astype(jnp.bfloat16)

    @pl.when((i == 0) & (j == 0))
    def _():
        ha_scr[...] = normed(x_ref[...])

    def step(h_cur, h_next):
        def norm_ahead():
            c = jnp.clip(j - (i == 0).astype(jnp.int32), 0, PROJ_TM // AHEAD_ROWS - 1)
            rows = pl.ds(pl.multiple_of(c * AHEAD_ROWS, AHEAD_ROWS), AHEAD_ROWS)
            h_next[rows, :] = normed(x_ref[rows, :])

        @pl.when(j < NB_TILES)
        def _():
            norm_ahead()
            out_scale = jnp.where(j < A_WIDTH // PROJ_TN, AQ_SCALE, 1.0)
            ob_ref[...] = (_dot_nt(h_cur[...], w_ref[...]) * out_scale).astype(jnp.bfloat16)

        @pl.when(j >= NB_TILES)
        def _():
            norm_ahead()
            of_ref[...] = _dot_nt(h_cur[...], w_ref[...])

    @pl.when(i % 2 == 0)
    def _():
        step(ha_scr, hb_scr)

    @pl.when(i % 2 == 1)
    def _():
        step(hb_scr, ha_scr)


def _norm_proj(x2d, shift, scale, norm_g, w_p, layer):
    m = x2d.shape[0]
    n_row_tiles = m // PROJ_TM
    tiles_per_batch = SEQ // PROJ_TM

    def ahead(i, j):
        return jnp.where((i == 0) & (j == 0), 0, jnp.minimum(i + 1, n_row_tiles - 1))

    return pl.pallas_call(
        _proj_kernel,
        grid=(n_row_tiles, N_COL_TILES),
        in_specs=[
            pl.BlockSpec((PROJ_TM, D_MODEL), lambda i, j: (ahead(i, j), 0)),
            pl.BlockSpec((1, 1, D_MODEL), lambda i, j: (ahead(i, j) // tiles_per_batch, 0, 0)),
            pl.BlockSpec((1, 1, D_MODEL), lambda i, j: (ahead(i, j) // tiles_per_batch, 0, 0)),
            pl.BlockSpec((1, D_MODEL), lambda i, j: (0, 0)),
            pl.BlockSpec((None, PROJ_TN, D_MODEL), lambda i, j: (layer, j, 0)),
        ],
        out_specs=[
            pl.BlockSpec((PROJ_TM, PROJ_TN), lambda i, j: (i, jnp.minimum(j, NB_TILES - 1))),
            pl.BlockSpec((PROJ_TM, PROJ_TN), lambda i, j: (i, jnp.maximum(j - NB_TILES, 0))),
        ],
        out_shape=[
            jax.ShapeDtypeStruct((m, NB_COLS), jnp.bfloat16),
            jax.ShapeDtypeStruct((m, NF_COLS), jnp.float32),
        ],
        scratch_shapes=[pltpu.VMEM((PROJ_TM, D_MODEL), jnp.bfloat16),
                        pltpu.VMEM((PROJ_TM, D_MODEL), jnp.bfloat16)],
        compiler_params=pltpu.CompilerParams(
            dimension_semantics=("arbitrary", "arbitrary"), vmem_limit_bytes=VMEM_LIMIT),
        name="norm_proj",
    )(x2d, shift, scale, norm_g.reshape(1, D_MODEL), w_p)


TQ = 256
KB = 256
NQ = SEQ // TQ
NKB = SEQ // KB
HALF = KB // 2
RUN_LENGTHS = tuple(1 << b for b in reversed(range(NKB.bit_length())))
COUNT_CHAINS = 4


def _t5_bucket_np(d):
    max_exact = NUM_BUCKETS // 2
    d = np.maximum(d, 0)
    df = np.maximum(d, 1).astype(np.float32)
    large = max_exact + (np.log(df / np.float32(max_exact)) / np.float32(math.log(MAX_DISTANCE / max_exact))
                         * np.float32(NUM_BUCKETS - max_exact)).astype(np.int32)
    large = np.minimum(large, NUM_BUCKETS - 1)
    return np.where(d < max_exact, d, large).astype(np.int32)


assert int(_t5_bucket_np(np.arange(KB, 2 * SEQ)).min()) == NUM_BUCKETS - 1


def _band_buckets():
    tl = np.arange(TQ)[:, None]
    u = np.arange(KB)[None, :]
    prev = _t5_bucket_np(KB + tl - u)
    diag = _t5_bucket_np(tl - u)
    return np.stack([prev, diag]).astype(np.int32)


BAND_ROWS = 64


def _band_kernel(rb_ref, bucket_ref, o_ref):
    h = pl.program_id(0)
    far = rb_ref[NUM_BUCKETS - 1, h]

    def rows(r, carry):
        rs = pl.ds(pl.multiple_of(r * BAND_ROWS, BAND_ROWS), BAND_ROWS)
        bucket = bucket_ref[0, rs, :]
        acc = jnp.zeros(bucket.shape, jnp.float32)
        for b in range(NUM_BUCKETS):
            acc = jnp.where(bucket == b, rb_ref[b, h] - far, acc)
        o_ref[0, 0, rs, :] = acc * LOG2E
        return carry

    lax.fori_loop(0, TQ // BAND_ROWS, rows, 0)


def _bias_band(rel_bias):
    return pl.pallas_call(
        _band_kernel,
        grid=(A_HEADS, 2),
        in_specs=[
            pl.BlockSpec(memory_space=pltpu.SMEM),
            pl.BlockSpec((1, TQ, KB), lambda h, s: (s, 0, 0)),
        ],
        out_specs=pl.BlockSpec((1, 1, TQ, KB), lambda h, s: (h, s, 0, 0)),
        out_shape=jax.ShapeDtypeStruct((A_HEADS, 2, TQ, KB), jnp.float32),
        name="bias_band",
    )(rel_bias, jnp.asarray(_band_buckets()))


def _dot_nt(a, b):
    return lax.dot_general(a, b, (((1,), (1,)), ((), ())), preferred_element_type=jnp.float32)


def _dot_tn(a, b):
    return lax.dot_general(a, b, (((0,), (0,)), ((), ())), preferred_element_type=jnp.float32)


def _dsa_kernel(q_ref, k_ref, v_ref, iq_ref, ik_ref, iw_ref, ag_ref, band_ref, o_ref,
                iklo_scr, ikhi_scr, wb_scr, iq2_scr, key_scr, khi_scr, klo_scr, thrn_scr,
                cutn_scr, cnt_scr, madd_scr, m_scr, accv_scr, accl_scr):
    qi = pl.program_id(1)
    nkb = qi + 1
    bf16 = jnp.bfloat16

    def key_rows(kb):
        return pl.ds(pl.multiple_of(kb * KB, KB), KB)

    @pl.when(qi == 0)
    def _():
        def prep(c, carry):
            rs = key_rows(c)
            ik = ik_ref[0, rs, :]
            iklo_scr[rs, :] = ik.astype(bf16)
            ikhi_scr[rs, :] = pltpu.roll(ik, IDX_DIM, axis=1).astype(bf16)
            return carry

        lax.fori_loop(0, NKB, prep, 0)

    iw = iw_ref[0][:, :IDX_HEADS] * (IDX_HEADS ** -0.5 * IDX_DIM ** -0.5)
    for j in range(IDX_HEADS):
        wb_scr[j] = jnp.broadcast_to(iw[:, j:j + 1], (TQ, HALF))
    for jp in range(IDX_HEADS // 2):
        iq2_scr[jp * TQ:(jp + 1) * TQ, :] = iq_ref[0, :, jp * LANE:(jp + 1) * LANE].astype(bf16)

    row = lax.broadcasted_iota(jnp.int32, (TQ, HALF), 0)
    col = lax.broadcasted_iota(jnp.int32, (TQ, HALF), 1)

    def score_block(kb, carry):
        rs = key_rows(kb)
        s_lo = _dot_nt(iq2_scr[...], iklo_scr[rs, :])
        s_hi = _dot_nt(iq2_scr[...], ikhi_scr[rs, :])
        for half in range(2):
            ls = slice(half * HALF, (half + 1) * HALF)
            acc = jnp.zeros((TQ, HALF), jnp.float32)
            for jp in range(IDX_HEADS // 2):
                rj = slice(jp * TQ, (jp + 1) * TQ)
                acc = (acc + jnp.maximum(s_lo[rj, ls], 0.0) * wb_scr[2 * jp]
                       + jnp.maximum(s_hi[rj, ls], 0.0) * wb_scr[2 * jp + 1])
            bits = pltpu.bitcast(acc + 0.0, jnp.int32)
            key = jnp.where(bits >= 0, bits, bits ^ 0x7FFFFFFF)
            causal = (kb * KB + half * HALF + col) <= (qi * TQ + row)
            key_scr[kb, :, ls] = jnp.where(causal, key, INT_MIN)
        keyt = key_scr[kb].T
        khi_scr[kb] = lax.shift_right_arithmetic(keyt, 16).astype(jnp.int16)
        klo_scr[kb] = ((keyt & 0xFFFF) + I16_MIN).astype(jnp.int16)
        return carry

    lax.fori_loop(0, nkb, score_block, 0)

    def count(ref, cand, strictly=False):
        c16 = jnp.broadcast_to(cand.astype(jnp.int16), (PACK16, TQ))
        cnt_scr[...] = jnp.zeros((PACK16, TQ), jnp.int16)

        def run(base, n):
            parts = [jnp.zeros((PACK16, TQ), jnp.int16) for _ in range(COUNT_CHAINS)]
            for t in range(n):
                v = ref[base + t]
                for r in range(KB // PACK16):
                    blk = v[r * PACK16:(r + 1) * PACK16]
                    hit = (blk > c16) if strictly else (blk >= c16)
                    parts[r % COUNT_CHAINS] = parts[r % COUNT_CHAINS] + jnp.where(hit, jnp.int16(1), jnp.int16(0))
            cnt_scr[...] = cnt_scr[...] + functools.reduce(lambda a, b: a + b, parts)

        base = jnp.int32(0)
        for n in RUN_LENGTHS:
            take = (nkb & n) != 0
            pl.when(take)(functools.partial(run, base, n))
            base = base + jnp.where(take, n, 0)
        return jnp.sum(cnt_scr[...].astype(jnp.int32), axis=0, keepdims=True)

    def bisect16(ref, need):
        zero = jnp.zeros((1, TQ), jnp.int32)
        prefix = jnp.where(count(ref, zero) >= need, zero, I16_MIN)

        def bit_pass(it, prefix):
            cand = prefix + lax.shift_left(jnp.int32(1), 14 - it)
            return jnp.where(count(ref, cand) >= need, cand, prefix)

        return lax.fori_loop(0, 15, bit_pass, prefix)

    topk = jnp.full((1, TQ), TOPK, jnp.int32)
    p_hi = bisect16(khi_scr, topk)
    above = count(khi_scr, p_hi, strictly=True)
    p_hi16 = jnp.broadcast_to(p_hi.astype(jnp.int16), (PACK16, TQ))

    def narrow(kb, carry):
        hi, lo = khi_scr[kb], klo_scr[kb]
        for r in range(KB // PACK16):
            rs = slice(r * PACK16, (r + 1) * PACK16)
            klo_scr[kb, rs, :] = jnp.where(hi[rs] == p_hi16, lo[rs], jnp.int16(I16_MIN))
        return carry

    lax.fori_loop(0, nkb, narrow, 0)
    p_lo = bisect16(klo_scr, topk - above)
    prefix = p_hi * 65536 + (p_lo - I16_MIN)
    thr = jnp.maximum(prefix, INT_MIN + 1)
    thrn_scr[...] = jnp.broadcast_to(thr, (KB, TQ)).T

    p_lo16 = jnp.broadcast_to(p_lo.astype(jnp.int16), (PACK16, TQ))
    sub16 = lax.broadcasted_iota(jnp.int32, (PACK16, TQ), 0)

    def tie_blocks(kb, fn, carry):
        hi, lo = khi_scr[kb], klo_scr[kb]
        for r in range(KB // PACK16):
            rs = slice(r * PACK16, (r + 1) * PACK16)
            carry = fn(kb, r, rs, (hi[rs] == p_hi16) & (lo[rs] == p_lo16), carry)
        return carry

    def count_tied_and_above(kb, cnts):
        n_tied, n_above = cnts
        hi, lo = khi_scr[kb], klo_scr[kb]
        one, nil = jnp.int16(1), jnp.int16(0)
        for r in range(KB // PACK16):
            rs = slice(r * PACK16, (r + 1) * PACK16)
            n_tied = n_tied + jnp.where((hi[rs] == p_hi16) & (lo[rs] == p_lo16), one, nil)
            n_above = n_above + jnp.where(lo[rs] > p_lo16, one, nil)
        return n_tied, n_above

    zeros16 = jnp.zeros((PACK16, TQ), jnp.int16)
    tied, above_lo = lax.fori_loop(0, nkb, count_tied_and_above, (zeros16, zeros16))
    tied = jnp.sum(tied.astype(jnp.int32), axis=0, keepdims=True)
    above_lo = jnp.sum(above_lo.astype(jnp.int32), axis=0, keepdims=True)
    room = topk - above - above_lo
    trim = (tied > room) & (prefix != INT_MIN)

    def madd_block(kb, carry):
        madd_scr[kb] = jnp.where(key_scr[kb] >= thrn_scr[...], 0.0, NEG_BIG)
        return carry

    lax.fori_loop(0, nkb, madd_block, 0)

    @pl.when(jnp.max(trim.astype(jnp.int32)) > 0)
    def _():
        def mark(kb, carry):
            def put(kb, r, rs, tied, c):
                neg_pos = (-(kb * KB + r * PACK16) - sub16).astype(jnp.int16)
                khi_scr[kb, rs, :] = jnp.where(tied, neg_pos, jnp.int16(I16_MIN))
                return c
            return tie_blocks(kb, put, carry)

        lax.fori_loop(0, nkb, mark, 0)
        cut = jnp.where(trim, -bisect16(khi_scr, room), SEQ)
        cutn_scr[...] = jnp.broadcast_to(cut, (KB, TQ)).T

        def madd_ties(kb, carry):
            key = key_scr[kb]
            pos = kb * KB + lax.broadcasted_iota(jnp.int32, (TQ, KB), 1)
            keep_tie = jnp.where(pos <= cutn_scr[...], 0.0, NEG_BIG)
            madd_scr[kb] = jnp.where(key > thrn_scr[...], 0.0,
                                     jnp.where(key == thrn_scr[...], keep_tie, NEG_BIG))
            return carry

        lax.fori_loop(0, nkb, madd_ties, 0)

    m_scr[...] = jnp.full(m_scr.shape, NEG_BIG, jnp.float32)
    accv_scr[...] = jnp.zeros_like(accv_scr)
    accl_scr[...] = jnp.zeros_like(accl_scr)

    def attend(kb, slot):
        rs = key_rows(kb)
        ones_blk = jnp.ones((KB, A_HEAD_DIM), bf16)
        for h in range(A_HEADS):
            hs = slice(h * A_HEAD_DIM, (h + 1) * A_HEAD_DIM)
            lg = _dot_nt(q_ref[0, :, hs], k_ref[0, rs, hs]) + madd_scr[kb]
            if slot is not None:
                lg = lg + band_ref[h, slot]
            m_old = m_scr[h]
            m_new = jnp.maximum(m_old, jnp.max(lg, axis=-1, keepdims=True))
            alpha = jnp.exp2(m_old - m_new)
            p = jnp.concatenate([jnp.exp2(lg[:, :HALF] - m_new), jnp.exp2(lg[:, HALF:] - m_new)], axis=1)
            v_ones = jnp.concatenate([v_ref[0, rs, hs], ones_blk], axis=1)
            pv = jnp.dot(p.astype(bf16), v_ones, preferred_element_type=jnp.float32)
            accv_scr[h] = accv_scr[h] * alpha + pv[:, :A_HEAD_DIM]
            accl_scr[h] = accl_scr[h] * alpha + pv[:, A_HEAD_DIM:]
            m_scr[h] = m_new

    def far_block(kb, carry):
        attend(kb, None)
        return carry

    lax.fori_loop(0, qi - 1, far_block, 0)

    @pl.when(qi >= 1)
    def _():
        attend(qi - 1, 0)

    attend(qi, 1)

    for h in range(A_HEADS):
        hs = slice(h * A_HEAD_DIM, (h + 1) * A_HEAD_DIM)
        g = ag_ref[0, :, hs]
        o_ref[0, :, hs] = (accv_scr[h] / accl_scr[h] * (g * jax.nn.sigmoid(g))).astype(o_ref.dtype)


def _dsa_attention(proj_b, proj_f, band):
    cb = lambda name: BF_OFF[name] // A_WIDTH
    once = pl.Buffered(1)
    return pl.pallas_call(
        _dsa_kernel,
        grid=(BATCH, NQ),
        in_specs=[
            pl.BlockSpec((1, TQ, A_WIDTH), lambda b, i: (b, i, cb("aq"))),
            pl.BlockSpec((1, SEQ, A_WIDTH), lambda b, i: (b, 0, cb("ak"))),
            pl.BlockSpec((1, SEQ, A_WIDTH), lambda b, i: (b, 0, cb("av"))),
            pl.BlockSpec((1, TQ, A_WIDTH), lambda b, i: (b, i, F32_OFF["iq"] // A_WIDTH)),
            pl.BlockSpec((1, SEQ, LANE), lambda b, i: (b, 0, F32_OFF["ik"] // LANE)),
            pl.BlockSpec((1, TQ, LANE), lambda b, i: (b, i, F32_OFF["iw"] // LANE)),
            pl.BlockSpec((1, TQ, A_WIDTH), lambda b, i: (b, i, F32_OFF["ag"] // A_WIDTH)),
            pl.BlockSpec((A_HEADS, 2, TQ, KB), lambda b, i: (0, 0, 0, 0), pipeline_mode=once),
        ],
        out_specs=pl.BlockSpec((1, TQ, A_WIDTH), lambda b, i: (b, i, 0)),
        out_shape=jax.ShapeDtypeStruct((BATCH, SEQ, A_WIDTH), jnp.bfloat16),
        scratch_shapes=[
            pltpu.VMEM((SEQ, LANE), jnp.bfloat16),
            pltpu.VMEM((SEQ, LANE), jnp.bfloat16),
            pltpu.VMEM((IDX_HEADS, TQ, HALF), jnp.float32),
            pltpu.VMEM((IDX_HEADS // 2 * TQ, LANE), jnp.bfloat16),
            pltpu.VMEM((NKB, TQ, KB), jnp.int32),
            pltpu.VMEM((NKB, KB, TQ), jnp.int16),
            pltpu.VMEM((NKB, KB, TQ), jnp.int16),
            pltpu.VMEM((TQ, KB), jnp.int32),
            pltpu.VMEM((TQ, KB), jnp.int32),
            pltpu.VMEM((PACK16, TQ), jnp.int16),
            pltpu.VMEM((NKB, TQ, KB), jnp.float32),
            pltpu.VMEM((A_HEADS, TQ, HALF), jnp.float32),
            pltpu.VMEM((A_HEADS, TQ, A_HEAD_DIM), jnp.float32),
            pltpu.VMEM((A_HEADS, TQ, A_HEAD_DIM), jnp.float32),
        ],
        compiler_params=pltpu.CompilerParams(
            dimension_semantics=("arbitrary", "arbitrary"), vmem_limit_bytes=VMEM_LIMIT),
        name="dsa_attention",
    )(proj_b, proj_b, proj_b, proj_f, proj_f, proj_f, proj_f, band)


GLA_CT = 512
GLA_C = 64
GLA_SC = 256
GLA_NCS = GLA_SC // GLA_C


def _gla_kernel(bq_ref, bk_ref, bv_ref, bg_ref, ba_ref, wup_ref, balpha_ref, g_ref, o_ref, st_scr):
    @pl.when(pl.program_id(1) == 0)
    def _():
        st_scr[...] = jnp.zeros_like(st_scr)

    bf16 = jnp.bfloat16
    rr = lax.broadcasted_iota(jnp.int32, (GLA_SC, GLA_SC), 0)
    cc = lax.broadcasted_iota(jnp.int32, (GLA_SC, GLA_SC), 1)
    tri = (rr >= cc) & (rr // GLA_C == cc // GLA_C)
    tri_bf = tri.astype(bf16)
    er = lax.broadcasted_iota(jnp.int32, (GLA_SC, GLA_NCS * B_DK), 0)
    ec = lax.broadcasted_iota(jnp.int32, (GLA_SC, GLA_NCS * B_DK), 1)
    own_block = (er // GLA_C) == (ec // B_DK)
    wup = wup_ref[...].astype(bf16)
    balpha = balpha_ref[...]
    gain = g_ref[...]
    heads = range(B_HEADS)

    def expand(a):
        return jnp.where(own_block, jnp.concatenate([a] * GLA_NCS, axis=1), jnp.zeros((), a.dtype))

    def body(sc, carry):
        rows = pl.ds(pl.multiple_of(sc * GLA_SC, GLA_SC), GLA_SC)
        ba = ba_ref[0, rows, :][:, :GATE_RANK].astype(bf16)
        pre = jnp.dot(ba, wup, preferred_element_type=jnp.float32) + balpha
        log_a = (jnp.minimum(pre, 0.0) - jnp.log1p(jnp.exp(-jnp.abs(pre)))) * (1.0 / GATE_TEMP)
        la_hi = log_a.astype(bf16)
        la_lo = (log_a - la_hi.astype(jnp.float32)).astype(bf16)
        bcum = (jnp.dot(tri_bf, la_hi, preferred_element_type=jnp.float32)
                + jnp.dot(tri_bf, la_lo, preferred_element_type=jnp.float32))
        ks = [slice(h * B_DK, (h + 1) * B_DK) for h in heads]
        vs = [slice(h * B_DV, (h + 1) * B_DV) for h in heads]
        qe, ke, kd, dec, v = [], [], [], [], []
        for h in heads:
            b = bcum[:, ks[h]]
            b_end = b.reshape(GLA_NCS, GLA_C, B_DK)[:, GLA_C - 1:GLA_C, :]
            b_end_rows = jnp.broadcast_to(b_end, (GLA_NCS, GLA_C, B_DK)).reshape(GLA_SC, B_DK)
            q = bq_ref[0, rows, ks[h]] * (B_DK ** -0.5)
            k = bk_ref[0, rows, ks[h]]
            qe.append((q * jnp.exp(b)).astype(bf16))
            ke.append((k * jnp.exp(-b)).astype(bf16))
            kd.append((k * jnp.exp(b_end_rows - b)).astype(bf16))
            dec.append(jnp.exp(b_end))
            v.append(bv_ref[0, rows, vs[h]].astype(bf16))
        attn = [jnp.where(tri, _dot_nt(qe[h], ke[h]), 0.0).astype(bf16) for h in heads]
        o_intra = [jnp.dot(attn[h], v[h], preferred_element_type=jnp.float32) for h in heads]
        upd = [_dot_tn(v[h], expand(kd[h])) for h in heads]
        o_inter = []
        for h in heads:
            st = st_scr[h]
            states = []
            for c in range(GLA_NCS):
                states.append(st.astype(bf16))
                st = st * dec[h][c] + upd[h][:, c * B_DK:(c + 1) * B_DK]
            st_scr[h] = st
            o_inter.append(_dot_nt(expand(qe[h]), jnp.concatenate(states, axis=1)))
        for h in heads:
            o = o_intra[h] + o_inter[h]
            on = o * lax.rsqrt(jnp.mean(o * o, axis=-1, keepdims=True) + EPS) * gain
            g = bg_ref[0, rows, vs[h]]
            o_ref[0, rows, vs[h]] = (on * (g * jax.nn.sigmoid(g))).astype(o_ref.dtype)
        return carry

    lax.fori_loop(0, GLA_CT // GLA_SC, body, 0)


def _gla(proj_f, w_alpha_up, b_alpha, gla_g):
    return pl.pallas_call(
        _gla_kernel,
        grid=(BATCH, SEQ // GLA_CT),
        in_specs=[
            pl.BlockSpec((1, GLA_CT, B_KEY_WIDTH), lambda b, t: (b, t, F32_OFF["bq"] // B_KEY_WIDTH)),
            pl.BlockSpec((1, GLA_CT, B_KEY_WIDTH), lambda b, t: (b, t, F32_OFF["bk"] // B_KEY_WIDTH)),
            pl.BlockSpec((1, GLA_CT, B_WIDTH), lambda b, t: (b, t, F32_OFF["bv"] // B_WIDTH)),
            pl.BlockSpec((1, GLA_CT, B_WIDTH), lambda b, t: (b, t, F32_OFF["bg"] // B_WIDTH)),
            pl.BlockSpec((1, GLA_CT, LANE), lambda b, t: (b, t, F32_OFF["ba"] // LANE)),
            pl.BlockSpec((GATE_RANK, B_KEY_WIDTH), lambda b, t: (0, 0)),
            pl.BlockSpec((1, B_KEY_WIDTH), lambda b, t: (0, 0)),
            pl.BlockSpec((1, B_DV), lambda b, t: (0, 0)),
        ],
        out_specs=pl.BlockSpec((1, GLA_CT, B_WIDTH), lambda b, t: (b, t, 0)),
        out_shape=jax.ShapeDtypeStruct((BATCH, SEQ, B_WIDTH), jnp.bfloat16),
        scratch_shapes=[pltpu.VMEM((B_HEADS, B_DV, B_DK), jnp.float32)],
        compiler_params=pltpu.CompilerParams(
            dimension_semantics=("arbitrary", "arbitrary"), vmem_limit_bytes=VMEM_LIMIT),
        name="gla",
    )(proj_f, proj_f, proj_f, proj_f, proj_f, w_alpha_up, b_alpha.reshape(1, B_KEY_WIDTH),
      gla_g.reshape(1, B_DV))


OUT_TM = 512


def _out_kernel(a_ref, b_ref, wa_ref, wb_ref, x_ref, gate_ref, fg_ref, o_ref, *, final_norm):
    y = (jnp.dot(a_ref[...], wa_ref[...], preferred_element_type=jnp.float32)
         + jnp.dot(b_ref[...], wb_ref[...], preferred_element_type=jnp.float32))
    xn = x_ref[...] + gate_ref[0] * y
    if final_norm:
        r = xn * lax.rsqrt(jnp.mean(xn * xn, axis=-1, keepdims=True) + EPS)
        xn = r * fg_ref[...]
    o_ref[...] = xn


def _out_proj(a_out, b_out, w_out_bf, x2d, gate, final_g, layer, final_norm):
    m = x2d.shape[0]
    tiles_per_batch = SEQ // OUT_TM
    return pl.pallas_call(
        functools.partial(_out_kernel, final_norm=final_norm),
        grid=(m // OUT_TM,),
        in_specs=[
            pl.BlockSpec((OUT_TM, A_WIDTH), lambda i: (i, 0)),
            pl.BlockSpec((OUT_TM, B_WIDTH), lambda i: (i, 0)),
            pl.BlockSpec((None, A_WIDTH, D_MODEL), lambda i: (layer, 0, 0)),
            pl.BlockSpec((None, B_WIDTH, D_MODEL), lambda i: (layer, 1, 0)),
            pl.BlockSpec((OUT_TM, D_MODEL), lambda i: (i, 0)),
            pl.BlockSpec((1, 1, D_MODEL), lambda i: (i // tiles_per_batch, 0, 0)),
            pl.BlockSpec((1, D_MODEL), lambda i: (0, 0)),
        ],
        out_specs=pl.BlockSpec((OUT_TM, D_MODEL), lambda i: (i, 0)),
        out_shape=jax.ShapeDtypeStruct((m, D_MODEL), jnp.float32),
        compiler_params=pltpu.CompilerParams(
            dimension_semantics=("arbitrary",), vmem_limit_bytes=VMEM_LIMIT),
        name="out_proj",
    )(a_out, b_out, w_out_bf, w_out_bf, x2d, gate, final_g.reshape(1, D_MODEL))


def kernel(x, c, w_ada, b_ada, norm_g, w_in, w_alpha_up, b_alpha, gla_g, w_out, rel_bias, final_g):
    mod = _adaln_mod(c, w_ada, b_ada)
    band = _bias_band(rel_bias)
    w_in_p = _pack_w_in(jnp.swapaxes(w_in, 1, 2))
    w_out_bf = w_out.astype(jnp.bfloat16)
    x2d = x.reshape(BATCH * SEQ, D_MODEL)
    for l in range(DEPTH):
        shift = mod[l, :, 0:D_MODEL].reshape(BATCH, 1, D_MODEL)
        scale = mod[l, :, D_MODEL:2 * D_MODEL].reshape(BATCH, 1, D_MODEL)
        gate = mod[l, :, 2 * D_MODEL:].reshape(BATCH, 1, D_MODEL)
        proj_b, proj_f = _norm_proj(x2d, shift, scale, norm_g[l], w_in_p, l)
        proj_b = proj_b.reshape(BATCH, SEQ, NB_COLS)
        proj_f = proj_f.reshape(BATCH, SEQ, NF_COLS)
        a_out = _dsa_attention(proj_b, proj_f, band)
        b_out = _gla(proj_f, w_alpha_up[l], b_alpha[l], gla_g[l])
        x2d = _out_proj(a_out.reshape(BATCH * SEQ, A_WIDTH), b_out.reshape(BATCH * SEQ, B_WIDTH),
                        w_out_bf, x2d, gate, final_g, l, final_norm=(l == DEPTH - 1))
    return x2d.reshape(BATCH, SEQ, D_MODEL)
```

```python
import functools
import math

import numpy as np
import jax
import jax.numpy as jnp
from jax import lax
from jax.experimental import pallas as pl
from jax.experimental.pallas import tpu as pltpu

D_MODEL = 2048
BATCH = 4
SEQ = 2048
DEPTH = 4
A_WIDTH = 1024
A_HEADS = 8
A_HEAD_DIM = 128
IDX_HEADS = 16
IDX_DIM = 64
TOPK = min(256, SEQ // 4)
B_WIDTH = 1024
B_HEADS = 4
B_KEY_WIDTH = 512
B_DK = 128
B_DV = 256
GATE_RANK = 16
GATE_TEMP = 16.0
NUM_BUCKETS = 32
MAX_DISTANCE = 128
EPS = 1e-6

IN_WIDTHS = (A_WIDTH, A_WIDTH, A_WIDTH, A_WIDTH, IDX_HEADS * IDX_DIM, IDX_DIM, IDX_HEADS,
             B_KEY_WIDTH, B_KEY_WIDTH, B_WIDTH, B_WIDTH, GATE_RANK)
IN_NAMES = ("aq", "ak", "av", "ag", "iq", "ik", "iw", "bq", "bk", "bv", "bg", "ba")
IN_OFFSETS = dict(zip(IN_NAMES, np.concatenate([[0], np.cumsum(IN_WIDTHS)[:-1]]).tolist()))
IN_WIDTH_OF = dict(zip(IN_NAMES, IN_WIDTHS))
IN_COLS = sum(IN_WIDTHS)

LANE = 128
VMEM_LIMIT = 52 * 1024 * 1024

PROJ_TN = 512
BF_SEGS = (("aq", 1024), ("ak", 1024), ("av", 1024))
F32_SEGS = (("ag", 1024), ("iq", 1024), ("bv", 1024), ("bg", 1024), ("bq", 512), ("bk", 512),
            ("ik", LANE), ("iw", LANE), ("ba", LANE))
NB_COLS = sum(w for _, w in BF_SEGS)
NF_USED = sum(w for _, w in F32_SEGS)
NF_COLS = -(-NF_USED // PROJ_TN) * PROJ_TN
NP_COLS = NB_COLS + NF_COLS


def _seg_offsets(segs):
    offs, o = {}, 0
    for name, w in segs:
        offs[name] = o
        o += w
    return offs


BF_OFF = _seg_offsets(BF_SEGS)
F32_OFF = _seg_offsets(F32_SEGS)

INT_MIN = -2 ** 31
NEG_BIG = -1e30
LOG2E = math.log2(math.e)
AQ_SCALE = A_HEAD_DIM ** -0.5 * LOG2E
I16_MIN = -2 ** 15
PACK16 = 16


PACK_COLS = 256


def _pack_kernel(w_ref, o_ref):
    dst = 0
    for name, width in BF_SEGS + F32_SEGS:
        src, used = IN_OFFSETS[name], IN_WIDTH_OF[name]
        o_ref[0, dst:dst + used, :] = w_ref[0, src:src + used, :].astype(jnp.bfloat16)
        if width > used:
            o_ref[0, dst + used:dst + width, :] = jnp.zeros((width - used, PACK_COLS), jnp.bfloat16)
        dst += width
    if dst < NP_COLS:
        o_ref[0, dst:, :] = jnp.zeros((NP_COLS - dst, PACK_COLS), jnp.bfloat16)


def _pack_w_in(w_in_t):
    return pl.pallas_call(
        _pack_kernel,
        grid=(DEPTH, D_MODEL // PACK_COLS),
        in_specs=[pl.BlockSpec((1, IN_COLS, PACK_COLS), lambda l, c: (l, 0, c))],
        out_specs=pl.BlockSpec((1, NP_COLS, PACK_COLS), lambda l, c: (l, 0, c)),
        out_shape=jax.ShapeDtypeStruct((DEPTH, NP_COLS, D_MODEL), jnp.bfloat16),
        compiler_params=pltpu.CompilerParams(
            dimension_semantics=("arbitrary", "arbitrary"), vmem_limit_bytes=VMEM_LIMIT),
        name="pack_w_in",
    )(w_in_t)


MOD_TN = 768


def _mod_kernel(c_ref, w_ref, b_ref, o_ref):
    c = c_ref[...]
    c_act = c * jax.nn.sigmoid(c)
    acc = jnp.dot(c_act.astype(jnp.bfloat16), w_ref[0].astype(jnp.bfloat16),
                  preferred_element_type=jnp.float32)
    o_ref[0] = acc + b_ref[0]


def _adaln_mod(c, w_ada, b_ada):
    cp = jnp.pad(c, ((0, 8 - BATCH), (0, 0)))
    out = pl.pallas_call(
        _mod_kernel,
        grid=(DEPTH, 3 * D_MODEL // MOD_TN),
        in_specs=[
            pl.BlockSpec((8, D_MODEL), lambda l, j: (0, 0)),
            pl.BlockSpec((1, D_MODEL, MOD_TN), lambda l, j: (l, 0, j)),
            pl.BlockSpec((1, 1, MOD_TN), lambda l, j: (l, 0, j)),
        ],
        out_specs=pl.BlockSpec((1, 8, MOD_TN), lambda l, j: (l, 0, j)),
        out_shape=jax.ShapeDtypeStruct((DEPTH, 8, 3 * D_MODEL), jnp.float32),
        compiler_params=pltpu.CompilerParams(
            dimension_semantics=("arbitrary", "arbitrary"), vmem_limit_bytes=VMEM_LIMIT),
        name="adaln_mod",
    )(cp, w_ada, b_ada.reshape(DEPTH, 1, 3 * D_MODEL))
    return out[:, :BATCH]


PROJ_TM = 1024
NB_TILES = NB_COLS // PROJ_TN
N_COL_TILES = NP_COLS // PROJ_TN
AHEAD_ROWS = 64
assert PROJ_TM // AHEAD_ROWS < N_COL_TILES


def _proj_kernel(x_ref, shift_ref, scale_ref, g_ref, w_ref, ob_ref, of_ref, ha_scr, hb_scr):
    i, j = pl.program_id(0), pl.program_id(1)

    def normed(x):
        r = x * lax.rsqrt(jnp.mean(x * x, axis=-1, keepdims=True) + EPS)
        return ((r * g_ref[...]) * (1.0 + scale_ref[0]) + shift_ref[0]).astype(jnp.bfloat16)

    @pl.when((i == 0) & (j == 0))
    def _():
        ha_scr[...] = normed(x_ref[...])

    def step(h_cur, h_next):
        def norm_ahead():
            c = jnp.clip(j - (i == 0).astype(jnp.int32), 0, PROJ_TM // AHEAD_ROWS - 1)
            rows = pl.ds(pl.multiple_of(c * AHEAD_ROWS, AHEAD_ROWS), AHEAD_ROWS)
            h_next[rows, :] = normed(x_ref[rows, :])

        @pl.when(j < NB_TILES)
        def _():
            norm_ahead()
            out_scale = jnp.where(j < A_WIDTH // PROJ_TN, AQ_SCALE, 1.0)
            ob_ref[...] = (_dot_nt(h_cur[...], w_ref[...]) * out_scale).astype(jnp.bfloat16)

        @pl.when(j >= NB_TILES)
        def _():
            norm_ahead()
            of_ref[...] = _dot_nt(h_cur[...], w_ref[...])

    @pl.when(i % 2 == 0)
    def _():
        step(ha_scr, hb_scr)

    @pl.when(i % 2 == 1)
    def _():
        step(hb_scr, ha_scr)


def _norm_proj(x2d, shift, scale, norm_g, w_p, layer):
    m = x2d.shape[0]
    n_row_tiles = m // PROJ_TM
    tiles_per_batch = SEQ // PROJ_TM

    def ahead(i, j):
        return jnp.where((i == 0) & (j == 0), 0, jnp.minimum(i + 1, n_row_tiles - 1))

    return pl.pallas_call(
        _proj_kernel,
        grid=(n_row_tiles, N_COL_TILES),
        in_specs=[
            pl.BlockSpec((PROJ_TM, D_MODEL), lambda i, j: (ahead(i, j), 0)),
            pl.BlockSpec((1, 1, D_MODEL), lambda i, j: (ahead(i, j) // tiles_per_batch, 0, 0)),
            pl.BlockSpec((1, 1, D_MODEL), lambda i, j: (ahead(i, j) // tiles_per_batch, 0, 0)),
            pl.BlockSpec((1, D_MODEL), lambda i, j: (0, 0)),
            pl.BlockSpec((None, PROJ_TN, D_MODEL), lambda i, j: (layer, j, 0)),
        ],
        out_specs=[
            pl.BlockSpec((PROJ_TM, PROJ_TN), lambda i, j: (i, jnp.minimum(j, NB_TILES - 1))),
            pl.BlockSpec((PROJ_TM, PROJ_TN), lambda i, j: (i, jnp.maximum(j - NB_TILES, 0))),
        ],
        out_shape=[
            jax.ShapeDtypeStruct((m, NB_COLS), jnp.bfloat16),
            jax.ShapeDtypeStruct((m, NF_COLS), jnp.float32),
        ],
        scratch_shapes=[pltpu.VMEM((PROJ_TM, D_MODEL), jnp.bfloat16),
                        pltpu.VMEM((PROJ_TM, D_MODEL), jnp.bfloat16)],
        compiler_params=pltpu.CompilerParams(
            dimension_semantics=("arbitrary", "arbitrary"), vmem_limit_bytes=VMEM_LIMIT),
        name="norm_proj",
    )(x2d, shift, scale, norm_g.reshape(1, D_MODEL), w_p)


TQ = 256
KB = 256
NQ = SEQ // TQ
NKB = SEQ // KB
HALF = KB // 2
RUN_LENGTHS = tuple(1 << b for b in reversed(range(NKB.bit_length())))
COUNT_CHAINS = 4


def _t5_bucket_np(d):
    max_exact = NUM_BUCKETS // 2
    d = np.maximum(d, 0)
    df = np.maximum(d, 1).astype(np.float32)
    large = max_exact + (np.log(df / np.float32(max_exact)) / np.float32(math.log(MAX_DISTANCE / max_exact))
                         * np.float32(NUM_BUCKETS - max_exact)).astype(np.int32)
    large = np.minimum(large, NUM_BUCKETS - 1)
    return np.where(d < max_exact, d, large).astype(np.int32)


assert int(_t5_bucket_np(np.arange(KB, 2 * SEQ)).min()) == NUM_BUCKETS - 1


def _band_buckets():
    tl = np.arange(TQ)[:, None]
    u = np.arange(KB)[None, :]
    prev = _t5_bucket_np(KB + tl - u)
    diag = _t5_bucket_np(tl - u)
    return np.stack([prev, diag]).astype(np.int32)


BAND_ROWS = 64


def _band_kernel(rb_ref, bucket_ref, o_ref):
    h = pl.program_id(0)
    far = rb_ref[NUM_BUCKETS - 1, h]

    def rows(r, carry):
        rs = pl.ds(pl.multiple_of(r * BAND_ROWS, BAND_ROWS), BAND_ROWS)
        bucket = bucket_ref[0, rs, :]
        acc = jnp.zeros(bucket.shape, jnp.float32)
        for b in range(NUM_BUCKETS):
            acc = jnp.where(bucket == b, rb_ref[b, h] - far, acc)
        o_ref[0, 0, rs, :] = acc * LOG2E
        return carry

    lax.fori_loop(0, TQ // BAND_ROWS, rows, 0)


def _bias_band(rel_bias):
    return pl.pallas_call(
        _band_kernel,
        grid=(A_HEADS, 2),
        in_specs=[
            pl.BlockSpec(memory_space=pltpu.SMEM),
            pl.BlockSpec((1, TQ, KB), lambda h, s: (s, 0, 0)),
        ],
        out_specs=pl.BlockSpec((1, 1, TQ, KB), lambda h, s: (h, s, 0, 0)),
        out_shape=jax.ShapeDtypeStruct((A_HEADS, 2, TQ, KB), jnp.float32),
        name="bias_band",
    )(rel_bias, jnp.asarray(_band_buckets()))


def _dot_nt(a, b):
    return lax.dot_general(a, b, (((1,), (1,)), ((), ())), preferred_element_type=jnp.float32)


def _dot_tn(a, b):
    return lax.dot_general(a, b, (((0,), (0,)), ((), ())), preferred_element_type=jnp.float32)


def _dsa_kernel(q_ref, k_ref, v_ref, iq_ref, ik_ref, iw_ref, ag_ref, band_ref, o_ref,
                iklo_scr, ikhi_scr, wb_scr, iq2_scr, key_scr, khi_scr, klo_scr, thrn_scr,
                cutn_scr, cnt_scr, madd_scr, m_scr, accv_scr, accl_scr):
    qi = pl.program_id(1)
    nkb = qi + 1
    bf16 = jnp.bfloat16

    def key_rows(kb):
        return pl.ds(pl.multiple_of(kb * KB, KB), KB)

    @pl.when(qi == 0)
    def _():
        def prep(c, carry):
            rs = key_rows(c)
            ik = ik_ref[0, rs, :]
            iklo_scr[rs, :] = ik.astype(bf16)
            ikhi_scr[rs, :] = pltpu.roll(ik, IDX_DIM, axis=1).astype(bf16)
            return carry

        lax.fori_loop(0, NKB, prep, 0)

    iw = iw_ref[0][:, :IDX_HEADS] * (IDX_HEADS ** -0.5 * IDX_DIM ** -0.5)
    for j in range(IDX_HEADS):
        wb_scr[j] = jnp.broadcast_to(iw[:, j:j + 1], (TQ, HALF))
    for jp in range(IDX_HEADS // 2):
        iq2_scr[jp * TQ:(jp + 1) * TQ, :] = iq_ref[0, :, jp * LANE:(jp + 1) * LANE].astype(bf16)

    row = lax.broadcasted_iota(jnp.int32, (TQ, HALF), 0)
    col = lax.broadcasted_iota(jnp.int32, (TQ, HALF), 1)

    def score_block(kb, carry):
        rs = key_rows(kb)
        s_lo = _dot_nt(iq2_scr[...], iklo_scr[rs, :])
        s_hi = _dot_nt(iq2_scr[...], ikhi_scr[rs, :])
        for half in range(2):
            ls = slice(half * HALF, (half + 1) * HALF)
            acc = jnp.zeros((TQ, HALF), jnp.float32)
            for jp in range(IDX_HEADS // 2):
                rj = slice(jp * TQ, (jp + 1) * TQ)
                acc = (acc + jnp.maximum(s_lo[rj, ls], 0.0) * wb_scr[2 * jp]
                       + jnp.maximum(s_hi[rj, ls], 0.0) * wb_scr[2 * jp + 1])
            bits = pltpu.bitcast(acc + 0.0, jnp.int32)
            key = jnp.where(bits >= 0, bits, bits ^ 0x7FFFFFFF)
            causal = (kb * KB + half * HALF + col) <= (qi * TQ + row)
            key_scr[kb, :, ls] = jnp.where(causal, key, INT_MIN)
        keyt = key_scr[kb].T
        khi_scr[kb] = lax.shift_right_arithmetic(keyt, 16).astype(jnp.int16)
        klo_scr[kb] = ((keyt & 0xFFFF) + I16_MIN).astype(jnp.int16)
        return carry

    lax.fori_loop(0, nkb, score_block, 0)

    def count(ref, cand, strictly=False):
        c16 = jnp.broadcast_to(cand.astype(jnp.int16), (PACK16, TQ))
        cnt_scr[...] = jnp.zeros((PACK16, TQ), jnp.int16)

        def run(base, n):
            parts = [jnp.zeros((PACK16, TQ), jnp.int16) for _ in range(COUNT_CHAINS)]
            for t in range(n):
                v = ref[base + t]
                for r in range(KB // PACK16):
                    blk = v[r * PACK16:(r + 1) * PACK16]
                    hit = (blk > c16) if strictly else (blk >= c16)
                    parts[r % COUNT_CHAINS] = parts[r % COUNT_CHAINS] + jnp.where(hit, jnp.int16(1), jnp.int16(0))
            cnt_scr[...] = cnt_scr[...] + functools.reduce(lambda a, b: a + b, parts)

        base = jnp.int32(0)
        for n in RUN_LENGTHS:
            take = (nkb & n) != 0
            pl.when(take)(functools.partial(run, base, n))
            base = base + jnp.where(take, n, 0)
        return jnp.sum(cnt_scr[...].astype(jnp.int32), axis=0, keepdims=True)

    def bisect16(ref, need):
        zero = jnp.zeros((1, TQ), jnp.int32)
        prefix = jnp.where(count(ref, zero) >= need, zero, I16_MIN)

        def bit_pass(it, prefix):
            cand = prefix + lax.shift_left(jnp.int32(1), 14 - it)
            return jnp.where(count(ref, cand) >= need, cand, prefix)

        return lax.fori_loop(0, 15, bit_pass, prefix)

    topk = jnp.full((1, TQ), TOPK, jnp.int32)
    p_hi = bisect16(khi_scr, topk)
    above = count(khi_scr, p_hi, strictly=True)
    p_hi16 = jnp.broadcast_to(p_hi.astype(jnp.int16), (PACK16, TQ))

    def narrow(kb, carry):
        hi, lo = khi_scr[kb], klo_scr[kb]
        for r in range(KB // PACK16):
            rs = slice(r * PACK16, (r + 1) * PACK16)
            klo_scr[kb, rs, :] = jnp.where(hi[rs] == p_hi16, lo[rs], jnp.int16(I16_MIN))
        return carry

    lax.fori_loop(0, nkb, narrow, 0)
    p_lo = bisect16(klo_scr, topk - above)
    prefix = p_hi * 65536 + (p_lo - I16_MIN)
    thr = jnp.maximum(prefix, INT_MIN + 1)
    thrn_scr[...] = jnp.broadcast_to(thr, (KB, TQ)).T

    p_lo16 = jnp.broadcast_to(p_lo.astype(jnp.int16), (PACK16, TQ))
    sub16 = lax.broadcasted_iota(jnp.int32, (PACK16, TQ), 0)

    def tie_blocks(kb, fn, carry):
        hi, lo = khi_scr[kb], klo_scr[kb]
        for r in range(KB // PACK16):
            rs = slice(r * PACK16, (r + 1) * PACK16)
            carry = fn(kb, r, rs, (hi[rs] == p_hi16) & (lo[rs] == p_lo16), carry)
        return carry

    def count_tied_and_above(kb, cnts):
        n_tied, n_above = cnts
        hi, lo = khi_scr[kb], klo_scr[kb]
        one, nil = jnp.int16(1), jnp.int16(0)
        for r in range(KB // PACK16):
            rs = slice(r * PACK16, (r + 1) * PACK16)
            n_tied = n_tied + jnp.where((hi[rs] == p_hi16) & (lo[rs] == p_lo16), one, nil)
            n_above = n_above + jnp.where(lo[rs] > p_lo16, one, nil)
        return n_tied, n_above

    zeros16 = jnp.zeros((PACK16, TQ), jnp.int16)
    tied, above_lo = lax.fori_loop(0, nkb, count_tied_and_above, (zeros16, zeros16))
    tied = jnp.sum(tied.astype(jnp.int32), axis=0, keepdims=True)
    above_lo = jnp.sum(above_lo.astype(jnp.int32), axis=0, keepdims=True)
    room = topk - above - above_lo
    trim = (tied > room) & (prefix != INT_MIN)

    def madd_block(kb, carry):
        madd_scr[kb] = jnp.where(key_scr[kb] >= thrn_scr[...], 0.0, NEG_BIG)
        return carry

    lax.fori_loop(0, nkb, madd_block, 0)

    @pl.when(jnp.max(trim.astype(jnp.int32)) > 0)
    def _():
        def mark(kb, carry):
            def put(kb, r, rs, tied, c):
                neg_pos = (-(kb * KB + r * PACK16) - sub16).astype(jnp.int16)
                khi_scr[kb, rs, :] = jnp.where(tied, neg_pos, jnp.int16(I16_MIN))
                return c
            return tie_blocks(kb, put, carry)

        lax.fori_loop(0, nkb, mark, 0)
        cut = jnp.where(trim, -bisect16(khi_scr, room), SEQ)
        cutn_scr[...] = jnp.broadcast_to(cut, (KB, TQ)).T

        def madd_ties(kb, carry):
            key = key_scr[kb]
            pos = kb * KB + lax.broadcasted_iota(jnp.int32, (TQ, KB), 1)
            keep_tie = jnp.where(pos <= cutn_scr[...], 0.0, NEG_BIG)
            madd_scr[kb] = jnp.where(key > thrn_scr[...], 0.0,
                                     jnp.where(key == thrn_scr[...], keep_tie, NEG_BIG))
            return carry

        lax.fori_loop(0, nkb, madd_ties, 0)

    def attend(kb, slot, first=False):
        rs = key_rows(kb)
        ones_blk = jnp.ones((KB, A_HEAD_DIM), bf16)
        for h in range(A_HEADS):
            hs = slice(h * A_HEAD_DIM, (h + 1) * A_HEAD_DIM)
            lg = _dot_nt(q_ref[0, :, hs], k_ref[0, rs, hs]) + madd_scr[kb]
            if slot is not None:
                lg = lg + band_ref[h, slot]
            m_new = jnp.max(lg, axis=-1, keepdims=True)
            if first:
                m_new = jnp.broadcast_to(m_new, (TQ, HALF))
            else:
                m_old = m_scr[h]
                m_new = jnp.maximum(m_old, m_new)
                alpha = jnp.exp2(m_old - m_new)
            p = jnp.concatenate([jnp.exp2(lg[:, :HALF] - m_new), jnp.exp2(lg[:, HALF:] - m_new)], axis=1)
            v_ones = jnp.concatenate([v_ref[0, rs, hs], ones_blk], axis=1)
            pv = jnp.dot(p.astype(bf16), v_ones, preferred_element_type=jnp.float32)
            if first:
                accv_scr[h] = pv[:, :A_HEAD_DIM]
                accl_scr[h] = pv[:, A_HEAD_DIM:]
            else:
                accv_scr[h] = accv_scr[h] * alpha + pv[:, :A_HEAD_DIM]
                accl_scr[h] = accl_scr[h] * alpha + pv[:, A_HEAD_DIM:]
            m_scr[h] = m_new

    attend(qi, 1, first=True)

    @pl.when(qi >= 1)
    def _():
        attend(qi - 1, 0)

    def far_block(kb, carry):
        attend(kb, None)
        return carry

    lax.fori_loop(0, qi - 1, far_block, 0)

    for h in range(A_HEADS):
        hs = slice(h * A_HEAD_DIM, (h + 1) * A_HEAD_DIM)
        g = ag_ref[0, :, hs]
        o_ref[0, :, hs] = (accv_scr[h] / accl_scr[h] * (g * jax.nn.sigmoid(g))).astype(o_ref.dtype)


def _dsa_attention(proj_b, proj_f, band):
    cb = lambda name: BF_OFF[name] // A_WIDTH
    once = pl.Buffered(1)
    return pl.pallas_call(
        _dsa_kernel,
        grid=(BATCH, NQ),
        in_specs=[
            pl.BlockSpec((1, TQ, A_WIDTH), lambda b, i: (b, i, cb("aq"))),
            pl.BlockSpec((1, SEQ, A_WIDTH), lambda b, i: (b, 0, cb("ak"))),
            pl.BlockSpec((1, SEQ, A_WIDTH), lambda b, i: (b, 0, cb("av"))),
            pl.BlockSpec((1, TQ, A_WIDTH), lambda b, i: (b, i, F32_OFF["iq"] // A_WIDTH)),
            pl.BlockSpec((1, SEQ, LANE), lambda b, i: (b, 0, F32_OFF["ik"] // LANE)),
            pl.BlockSpec((1, TQ, LANE), lambda b, i: (b, i, F32_OFF["iw"] // LANE)),
            pl.BlockSpec((1, TQ, A_WIDTH), lambda b, i: (b, i, F32_OFF["ag"] // A_WIDTH)),
            pl.BlockSpec((A_HEADS, 2, TQ, KB), lambda b, i: (0, 0, 0, 0), pipeline_mode=once),
        ],
        out_specs=pl.BlockSpec((1, TQ, A_WIDTH), lambda b, i: (b, i, 0)),
        out_shape=jax.ShapeDtypeStruct((BATCH, SEQ, A_WIDTH), jnp.bfloat16),
        scratch_shapes=[
            pltpu.VMEM((SEQ, LANE), jnp.bfloat16),
            pltpu.VMEM((SEQ, LANE), jnp.bfloat16),
            pltpu.VMEM((IDX_HEADS, TQ, HALF), jnp.float32),
            pltpu.VMEM((IDX_HEADS // 2 * TQ, LANE), jnp.bfloat16),
            pltpu.VMEM((NKB, TQ, KB), jnp.int32),
            pltpu.VMEM((NKB, KB, TQ), jnp.int16),
            pltpu.VMEM((NKB, KB, TQ), jnp.int16),
            pltpu.VMEM((TQ, KB), jnp.int32),
            pltpu.VMEM((TQ, KB), jnp.int32),
            pltpu.VMEM((PACK16, TQ), jnp.int16),
            pltpu.VMEM((NKB, TQ, KB), jnp.float32),
            pltpu.VMEM((A_HEADS, TQ, HALF), jnp.float32),
            pltpu.VMEM((A_HEADS, TQ, A_HEAD_DIM), jnp.float32),
            pltpu.VMEM((A_HEADS, TQ, A_HEAD_DIM), jnp.float32),
        ],
        compiler_params=pltpu.CompilerParams(
            dimension_semantics=("arbitrary", "arbitrary"), vmem_limit_bytes=VMEM_LIMIT),
        name="dsa_attention",
    )(proj_b, proj_b, proj_b, proj_f, proj_f, proj_f, proj_f, band)


GLA_CT = 512
GLA_C = 64
GLA_SC = 256
GLA_NCS = GLA_SC // GLA_C


def _gla_kernel(bq_ref, bk_ref, bv_ref, bg_ref, ba_ref, wup_ref, balpha_ref, g_ref, o_ref, st_scr):
    @pl.when(pl.program_id(1) == 0)
    def _():
        st_scr[...] = jnp.zeros_like(st_scr)

    bf16 = jnp.bfloat16
    rr = lax.broadcasted_iota(jnp.int32, (GLA_SC, GLA_SC), 0)
    cc = lax.broadcasted_iota(jnp.int32, (GLA_SC, GLA_SC), 1)
    tri = (rr >= cc) & (rr // GLA_C == cc // GLA_C)
    tri_bf = tri.astype(bf16)
    er = lax.broadcasted_iota(jnp.int32, (GLA_SC, GLA_NCS * B_DK), 0)
    ec = lax.broadcasted_iota(jnp.int32, (GLA_SC, GLA_NCS * B_DK), 1)
    own_block = (er // GLA_C) == (ec // B_DK)
    wup = wup_ref[...].astype(bf16)
    balpha = balpha_ref[...]
    gain = g_ref[...]
    heads = range(B_HEADS)

    def expand(a):
        return jnp.where(own_block, jnp.concatenate([a] * GLA_NCS, axis=1), jnp.zeros((), a.dtype))

    def body(sc, carry):
        rows = pl.ds(pl.multiple_of(sc * GLA_SC, GLA_SC), GLA_SC)
        ba = ba_ref[0, rows, :][:, :GATE_RANK].astype(bf16)
        pre = jnp.dot(ba, wup, preferred_element_type=jnp.float32) + balpha
        log_a = (jnp.minimum(pre, 0.0) - jnp.log1p(jnp.exp(-jnp.abs(pre)))) * (1.0 / GATE_TEMP)
        la_hi = log_a.astype(bf16)
        la_lo = (log_a - la_hi.astype(jnp.float32)).astype(bf16)
        bcum = (jnp.dot(tri_bf, la_hi, preferred_element_type=jnp.float32)
                + jnp.dot(tri_bf, la_lo, preferred_element_type=jnp.float32))
        ks = [slice(h * B_DK, (h + 1) * B_DK) for h in heads]
        vs = [slice(h * B_DV, (h + 1) * B_DV) for h in heads]
        qe, ke, kd, dec, v = [], [], [], [], []
        for h in heads:
            b = bcum[:, ks[h]]
            b_end = b.reshape(GLA_NCS, GLA_C, B_DK)[:, GLA_C - 1:GLA_C, :]
            b_end_rows = jnp.broadcast_to(b_end, (GLA_NCS, GLA_C, B_DK)).reshape(GLA_SC, B_DK)
            q = bq_ref[0, rows, ks[h]] * (B_DK ** -0.5)
            k = bk_ref[0, rows, ks[h]]
            qe.append((q * jnp.exp(b)).astype(bf16))
            ke.append((k * jnp.exp(-b)).astype(bf16))
            kd.append((k * jnp.exp(b_end_rows - b)).astype(bf16))
            dec.append(jnp.exp(b_end))
            v.append(bv_ref[0, rows, vs[h]].astype(bf16))
        attn = [jnp.where(tri, _dot_nt(qe[h], ke[h]), 0.0).astype(bf16) for h in heads]
        o_intra = [jnp.dot(attn[h], v[h], preferred_element_type=jnp.float32) for h in heads]
        upd = [_dot_tn(v[h], expand(kd[h])) for h in heads]
        o_inter = []
        for h in heads:
            st = st_scr[h]
            states = []
            for c in range(GLA_NCS):
                states.append(st.astype(bf16))
                st = st * dec[h][c] + upd[h][:, c * B_DK:(c + 1) * B_DK]
            st_scr[h] = st
            o_inter.append(_dot_nt(expand(qe[h]), jnp.concatenate(states, axis=1)))
        for h in heads:
            o = o_intra[h] + o_inter[h]
            on = o * lax.rsqrt(jnp.mean(o * o, axis=-1, keepdims=True) + EPS) * gain
            g = bg_ref[0, rows, vs[h]]
            o_ref[0, rows, vs[h]] = (on * (g * jax.nn.sigmoid(g))).astype(o_ref.dtype)
        return carry

    lax.fori_loop(0, GLA_CT // GLA_SC, body, 0)


def _gla(proj_f, w_alpha_up, b_alpha, gla_g):
    return pl.pallas_call(
        _gla_kernel,
        grid=(BATCH, SEQ // GLA_CT),
        in_specs=[
            pl.BlockSpec((1, GLA_CT, B_KEY_WIDTH), lambda b, t: (b, t, F32_OFF["bq"] // B_KEY_WIDTH)),
            pl.BlockSpec((1, GLA_CT, B_KEY_WIDTH), lambda b, t: (b, t, F32_OFF["bk"] // B_KEY_WIDTH)),
            pl.BlockSpec((1, GLA_CT, B_WIDTH), lambda b, t: (b, t, F32_OFF["bv"] // B_WIDTH)),
            pl.BlockSpec((1, GLA_CT, B_WIDTH), lambda b, t: (b, t, F32_OFF["bg"] // B_WIDTH)),
            pl.BlockSpec((1, GLA_CT, LANE), lambda b, t: (b, t, F32_OFF["ba"] // LANE)),
            pl.BlockSpec((GATE_RANK, B_KEY_WIDTH), lambda b, t: (0, 0)),
            pl.BlockSpec((1, B_KEY_WIDTH), lambda b, t: (0, 0)),
            pl.BlockSpec((1, B_DV), lambda b, t: (0, 0)),
        ],
        out_specs=pl.BlockSpec((1, GLA_CT, B_WIDTH), lambda b, t: (b, t, 0)),
        out_shape=jax.ShapeDtypeStruct((BATCH, SEQ, B_WIDTH), jnp.bfloat16),
        scratch_shapes=[pltpu.VMEM((B_HEADS, B_DV, B_DK), jnp.float32)],
        compiler_params=pltpu.CompilerParams(
            dimension_semantics=("arbitrary", "arbitrary"), vmem_limit_bytes=VMEM_LIMIT),
        name="gla",
    )(proj_f, proj_f, proj_f, proj_f, proj_f, w_alpha_up, b_alpha.reshape(1, B_KEY_WIDTH),
      gla_g.reshape(1, B_DV))


OUT_TM = 512


def _out_kernel(a_ref, b_ref, wa_ref, wb_ref, x_ref, gate_ref, fg_ref, o_ref, *, final_norm):
    y = (jnp.dot(a_ref[...], wa_ref[...], preferred_element_type=jnp.float32)
         + jnp.dot(b_ref[...], wb_ref[...], preferred_element_type=jnp.float32))
    xn = x_ref[...] + gate_ref[0] * y
    if final_norm:
        r = xn * lax.rsqrt(jnp.mean(xn * xn, axis=-1, keepdims=True) + EPS)
        xn = r * fg_ref[...]
    o_ref[...] = xn


def _out_proj(a_out, b_out, w_out_bf, x2d, gate, final_g, layer, final_norm):
    m = x2d.shape[0]
    tiles_per_batch = SEQ // OUT_TM
    return pl.pallas_call(
        functools.partial(_out_kernel, final_norm=final_norm),
        grid=(m // OUT_TM,),
        in_specs=[
            pl.BlockSpec((OUT_TM, A_WIDTH), lambda i: (i, 0)),
            pl.BlockSpec((OUT_TM, B_WIDTH), lambda i: (i, 0)),
            pl.BlockSpec((None, A_WIDTH, D_MODEL), lambda i: (layer, 0, 0)),
            pl.BlockSpec((None, B_WIDTH, D_MODEL), lambda i: (layer, 1, 0)),
            pl.BlockSpec((OUT_TM, D_MODEL), lambda i: (i, 0)),
            pl.BlockSpec((1, 1, D_MODEL), lambda i: (i // tiles_per_batch, 0, 0)),
            pl.BlockSpec((1, D_MODEL), lambda i: (0, 0)),
        ],
        out_specs=pl.BlockSpec((OUT_TM, D_MODEL), lambda i: (i, 0)),
        out_shape=jax.ShapeDtypeStruct((m, D_MODEL), jnp.float32),
        compiler_params=pltpu.CompilerParams(
            dimension_semantics=("arbitrary",), vmem_limit_bytes=VMEM_LIMIT),
        name="out_proj",
    )(a_out, b_out, w_out_bf, w_out_bf, x2d, gate, final_g.reshape(1, D_MODEL))


def kernel(x, c, w_ada, b_ada, norm_g, w_in, w_alpha_up, b_alpha, gla_g, w_out, rel_bias, final_g):
    mod = _adaln_mod(c, w_ada, b_ada)
    band = _bias_band(rel_bias)
    w_in_p = _pack_w_in(jnp.swapaxes(w_in, 1, 2))
    w_out_bf = w_out.astype(jnp.bfloat16)
    x2d = x.reshape(BATCH * SEQ, D_MODEL)
    for l in range(DEPTH):
        shift = mod[l, :, 0:D_MODEL].reshape(BATCH, 1, D_MODEL)
        scale = mod[l, :, D_MODEL:2 * D_MODEL].reshape(BATCH, 1, D_MODEL)
        gate = mod[l, :, 2 * D_MODEL:].reshape(BATCH, 1, D_MODEL)
        proj_b, proj_f = _norm_proj(x2d, shift, scale, norm_g[l], w_in_p, l)
        proj_b = proj_b.reshape(BATCH, SEQ, NB_COLS)
        proj_f = proj_f.reshape(BATCH, SEQ, NF_COLS)
        a_out = _dsa_attention(proj_b, proj_f, band)
        b_out = _gla(proj_f, w_alpha_up[l], b_alpha[l], gla_g[l])
        x2d = _out_proj(a_out.reshape(BATCH * SEQ, A_WIDTH), b_out.reshape(BATCH * SEQ, B_WIDTH),
                        w_out_bf, x2d, gate, final_g, l, final_norm=(l == DEPTH - 1))
    return x2d.reshape(BATCH, SEQ, D_MODEL)
```

```python
import functools
import math

import numpy as np
import jax
import jax.numpy as jnp
from jax import lax
from jax.experimental import pallas as pl
from jax.experimental.pallas import tpu as pltpu

D_MODEL = 2048
BATCH = 4
SEQ = 2048
DEPTH = 4
A_WIDTH = 1024
A_HEADS = 8
A_HEAD_DIM = 128
IDX_HEADS = 16
IDX_DIM = 64
TOPK = min(256, SEQ // 4)
B_WIDTH = 1024
B_HEADS = 4
B_KEY_WIDTH = 512
B_DK = 128
B_DV = 256
GATE_RANK = 16
GATE_TEMP = 16.0
NUM_BUCKETS = 32
MAX_DISTANCE = 128
EPS = 1e-6

IN_WIDTHS = (A_WIDTH, A_WIDTH, A_WIDTH, A_WIDTH, IDX_HEADS * IDX_DIM, IDX_DIM, IDX_HEADS,
             B_KEY_WIDTH, B_KEY_WIDTH, B_WIDTH, B_WIDTH, GATE_RANK)
IN_NAMES = ("aq", "ak", "av", "ag", "iq", "ik", "iw", "bq", "bk", "bv", "bg", "ba")
IN_OFFSETS = dict(zip(IN_NAMES, np.concatenate([[0], np.cumsum(IN_WIDTHS)[:-1]]).tolist()))
IN_WIDTH_OF = dict(zip(IN_NAMES, IN_WIDTHS))
IN_COLS = sum(IN_WIDTHS)

LANE = 128
VMEM_LIMIT = 52 * 1024 * 1024

PROJ_TN = 512
BF_SEGS = (("aq", 1024), ("ak", 1024), ("av", 1024), ("iq", 1024))
F32_SEGS = (("ag", 1024), ("bv", 1024), ("bg", 1024), ("bq", 512), ("bk", 512),
            ("ik", LANE), ("iw", LANE), ("ba", LANE))
NB_COLS = sum(w for _, w in BF_SEGS)
NF_USED = sum(w for _, w in F32_SEGS)
NF_COLS = -(-NF_USED // PROJ_TN) * PROJ_TN
NP_COLS = NB_COLS + NF_COLS


def _seg_offsets(segs):
    offs, o = {}, 0
    for name, w in segs:
        offs[name] = o
        o += w
    return offs


BF_OFF = _seg_offsets(BF_SEGS)
F32_OFF = _seg_offsets(F32_SEGS)

INT_MIN = -2 ** 31
NEG_BIG = -1e30
LOG2E = math.log2(math.e)
AQ_SCALE = A_HEAD_DIM ** -0.5 * LOG2E
I16_MIN = -2 ** 15
PACK16 = 16


PACK_COLS = 256


def _pack_kernel(w_ref, o_ref):
    dst = 0
    for name, width in BF_SEGS + F32_SEGS:
        src, used = IN_OFFSETS[name], IN_WIDTH_OF[name]
        o_ref[0, dst:dst + used, :] = w_ref[0, src:src + used, :].astype(jnp.bfloat16)
        if width > used:
            o_ref[0, dst + used:dst + width, :] = jnp.zeros((width - used, PACK_COLS), jnp.bfloat16)
        dst += width
    if dst < NP_COLS:
        o_ref[0, dst:, :] = jnp.zeros((NP_COLS - dst, PACK_COLS), jnp.bfloat16)


def _pack_w_in(w_in_t):
    return pl.pallas_call(
        _pack_kernel,
        grid=(DEPTH, D_MODEL // PACK_COLS),
        in_specs=[pl.BlockSpec((1, IN_COLS, PACK_COLS), lambda l, c: (l, 0, c))],
        out_specs=pl.BlockSpec((1, NP_COLS, PACK_COLS), lambda l, c: (l, 0, c)),
        out_shape=jax.ShapeDtypeStruct((DEPTH, NP_COLS, D_MODEL), jnp.bfloat16),
        compiler_params=pltpu.CompilerParams(
            dimension_semantics=("arbitrary", "arbitrary"), vmem_limit_bytes=VMEM_LIMIT),
        name="pack_w_in",
    )(w_in_t)


MOD_TN = 768


def _mod_kernel(c_ref, w_ref, b_ref, o_ref):
    c = c_ref[...]
    c_act = c * jax.nn.sigmoid(c)
    acc = jnp.dot(c_act.astype(jnp.bfloat16), w_ref[0].astype(jnp.bfloat16),
                  preferred_element_type=jnp.float32)
    o_ref[0] = acc + b_ref[0]


def _adaln_mod(c, w_ada, b_ada):
    cp = jnp.pad(c, ((0, 8 - BATCH), (0, 0)))
    out = pl.pallas_call(
        _mod_kernel,
        grid=(DEPTH, 3 * D_MODEL // MOD_TN),
        in_specs=[
            pl.BlockSpec((8, D_MODEL), lambda l, j: (0, 0)),
            pl.BlockSpec((1, D_MODEL, MOD_TN), lambda l, j: (l, 0, j)),
            pl.BlockSpec((1, 1, MOD_TN), lambda l, j: (l, 0, j)),
        ],
        out_specs=pl.BlockSpec((1, 8, MOD_TN), lambda l, j: (l, 0, j)),
        out_shape=jax.ShapeDtypeStruct((DEPTH, 8, 3 * D_MODEL), jnp.float32),
        compiler_params=pltpu.CompilerParams(
            dimension_semantics=("arbitrary", "arbitrary"), vmem_limit_bytes=VMEM_LIMIT),
        name="adaln_mod",
    )(cp, w_ada, b_ada.reshape(DEPTH, 1, 3 * D_MODEL))
    return out[:, :BATCH]


PROJ_TM = 1024
NB_TILES = NB_COLS // PROJ_TN
N_COL_TILES = NP_COLS // PROJ_TN
AHEAD_ROWS = 64
assert PROJ_TM // AHEAD_ROWS < N_COL_TILES


def _proj_kernel(x_ref, shift_ref, scale_ref, g_ref, w_ref, ob_ref, of_ref, ha_scr, hb_scr):
    i, j = pl.program_id(0), pl.program_id(1)

    def normed(x):
        r = x * lax.rsqrt(jnp.mean(x * x, axis=-1, keepdims=True) + EPS)
        return ((r * g_ref[...]) * (1.0 + scale_ref[0]) + shift_ref[0]).astype(jnp.bfloat16)

    @pl.when((i == 0) & (j == 0))
    def _():
        ha_scr[...] = normed(x_ref[...])

    def step(h_cur, h_next):
        def norm_ahead():
            c = jnp.clip(j - (i == 0).astype(jnp.int32), 0, PROJ_TM // AHEAD_ROWS - 1)
            rows = pl.ds(pl.multiple_of(c * AHEAD_ROWS, AHEAD_ROWS), AHEAD_ROWS)
            h_next[rows, :] = normed(x_ref[rows, :])

        @pl.when(j < NB_TILES)
        def _():
            norm_ahead()
            out_scale = jnp.where(j < A_WIDTH // PROJ_TN, AQ_SCALE, 1.0)
            ob_ref[...] = (_dot_nt(h_cur[...], w_ref[...]) * out_scale).astype(jnp.bfloat16)

        @pl.when(j >= NB_TILES)
        def _():
            norm_ahead()
            of_ref[...] = _dot_nt(h_cur[...], w_ref[...])

    @pl.when(i % 2 == 0)
    def _():
        step(ha_scr, hb_scr)

    @pl.when(i % 2 == 1)
    def _():
        step(hb_scr, ha_scr)


def _norm_proj(x2d, shift, scale, norm_g, w_p, layer):
    m = x2d.shape[0]
    n_row_tiles = m // PROJ_TM
    tiles_per_batch = SEQ // PROJ_TM

    def ahead(i, j):
        return jnp.where((i == 0) & (j == 0), 0, jnp.minimum(i + 1, n_row_tiles - 1))

    return pl.pallas_call(
        _proj_kernel,
        grid=(n_row_tiles, N_COL_TILES),
        in_specs=[
            pl.BlockSpec((PROJ_TM, D_MODEL), lambda i, j: (ahead(i, j), 0)),
            pl.BlockSpec((1, 1, D_MODEL), lambda i, j: (ahead(i, j) // tiles_per_batch, 0, 0)),
            pl.BlockSpec((1, 1, D_MODEL), lambda i, j: (ahead(i, j) // tiles_per_batch, 0, 0)),
            pl.BlockSpec((1, D_MODEL), lambda i, j: (0, 0)),
            pl.BlockSpec((None, PROJ_TN, D_MODEL), lambda i, j: (layer, j, 0)),
        ],
        out_specs=[
            pl.BlockSpec((PROJ_TM, PROJ_TN), lambda i, j: (i, jnp.minimum(j, NB_TILES - 1))),
            pl.BlockSpec((PROJ_TM, PROJ_TN), lambda i, j: (i, jnp.maximum(j - NB_TILES, 0))),
        ],
        out_shape=[
            jax.ShapeDtypeStruct((m, NB_COLS), jnp.bfloat16),
            jax.ShapeDtypeStruct((m, NF_COLS), jnp.float32),
        ],
        scratch_shapes=[pltpu.VMEM((PROJ_TM, D_MODEL), jnp.bfloat16),
                        pltpu.VMEM((PROJ_TM, D_MODEL), jnp.bfloat16)],
        compiler_params=pltpu.CompilerParams(
            dimension_semantics=("arbitrary", "arbitrary"), vmem_limit_bytes=VMEM_LIMIT),
        name="norm_proj",
    )(x2d, shift, scale, norm_g.reshape(1, D_MODEL), w_p)


TQ = 256
KB = 256
NQ = SEQ // TQ
NKB = SEQ // KB
HALF = KB // 2
RUN_LENGTHS = tuple(1 << b for b in reversed(range(NKB.bit_length())))
COUNT_CHAINS = 4


def _t5_bucket_np(d):
    max_exact = NUM_BUCKETS // 2
    d = np.maximum(d, 0)
    df = np.maximum(d, 1).astype(np.float32)
    large = max_exact + (np.log(df / np.float32(max_exact)) / np.float32(math.log(MAX_DISTANCE / max_exact))
                         * np.float32(NUM_BUCKETS - max_exact)).astype(np.int32)
    large = np.minimum(large, NUM_BUCKETS - 1)
    return np.where(d < max_exact, d, large).astype(np.int32)


assert int(_t5_bucket_np(np.arange(KB, 2 * SEQ)).min()) == NUM_BUCKETS - 1


def _band_buckets():
    tl = np.arange(TQ)[:, None]
    u = np.arange(KB)[None, :]
    prev = _t5_bucket_np(KB + tl - u)
    diag = _t5_bucket_np(tl - u)
    return np.stack([prev, diag]).astype(np.int32)


BAND_ROWS = 64


def _band_kernel(rb_ref, bucket_ref, o_ref):
    h = pl.program_id(0)
    far = rb_ref[NUM_BUCKETS - 1, h]

    def rows(r, carry):
        rs = pl.ds(pl.multiple_of(r * BAND_ROWS, BAND_ROWS), BAND_ROWS)
        bucket = bucket_ref[0, rs, :]
        acc = jnp.zeros(bucket.shape, jnp.float32)
        for b in range(NUM_BUCKETS):
            acc = jnp.where(bucket == b, rb_ref[b, h] - far, acc)
        o_ref[0, 0, rs, :] = acc * LOG2E
        return carry

    lax.fori_loop(0, TQ // BAND_ROWS, rows, 0)


def _bias_band(rel_bias):
    return pl.pallas_call(
        _band_kernel,
        grid=(A_HEADS, 2),
        in_specs=[
            pl.BlockSpec(memory_space=pltpu.SMEM),
            pl.BlockSpec((1, TQ, KB), lambda h, s: (s, 0, 0)),
        ],
        out_specs=pl.BlockSpec((1, 1, TQ, KB), lambda h, s: (h, s, 0, 0)),
        out_shape=jax.ShapeDtypeStruct((A_HEADS, 2, TQ, KB), jnp.float32),
        name="bias_band",
    )(rel_bias, jnp.asarray(_band_buckets()))


def _dot_nt(a, b):
    return lax.dot_general(a, b, (((1,), (1,)), ((), ())), preferred_element_type=jnp.float32)


def _dot_tn(a, b):
    return lax.dot_general(a, b, (((0,), (0,)), ((), ())), preferred_element_type=jnp.float32)


def _dsa_kernel(q_ref, k_ref, v_ref, iq_ref, ik_ref, iw_ref, ag_ref, band_ref, o_ref,
                iklo_scr, ikhi_scr, wb_scr, iq2_scr, key_scr, khi_scr, klo_scr, thrn_scr,
                cutn_scr, cnt_scr, madd_scr, m_scr, accv_scr, accl_scr):
    qi = pl.program_id(1)
    skip_first = TQ <= TOPK
    nkb = jnp.where(qi == 0, 0, qi + 1) if skip_first else qi + 1
    bf16 = jnp.bfloat16

    def key_rows(kb):
        return pl.ds(pl.multiple_of(kb * KB, KB), KB)

    @pl.when(qi == 0)
    def _():
        def prep(c, carry):
            rs = key_rows(c)
            ik = ik_ref[0, rs, :]
            iklo_scr[rs, :] = ik.astype(bf16)
            ikhi_scr[rs, :] = pltpu.roll(ik, IDX_DIM, axis=1).astype(bf16)
            return carry

        lax.fori_loop(0, NKB, prep, 0)

    iw = iw_ref[0][:, :IDX_HEADS] * (IDX_HEADS ** -0.5 * IDX_DIM ** -0.5)
    for j in range(IDX_HEADS):
        wb_scr[j] = jnp.broadcast_to(iw[:, j:j + 1], (TQ, HALF))
    for jp in range(IDX_HEADS // 2):
        iq2_scr[jp * TQ:(jp + 1) * TQ, :] = iq_ref[0, :, jp * LANE:(jp + 1) * LANE]

    row = lax.broadcasted_iota(jnp.int32, (TQ, HALF), 0)
    col = lax.broadcasted_iota(jnp.int32, (TQ, HALF), 1)

    def score_block(kb, carry):
        rs = key_rows(kb)
        s_lo = _dot_nt(iq2_scr[...], iklo_scr[rs, :])
        s_hi = _dot_nt(iq2_scr[...], ikhi_scr[rs, :])
        for half in range(2):
            ls = slice(half * HALF, (half + 1) * HALF)
            acc = jnp.zeros((TQ, HALF), jnp.float32)
            for jp in range(IDX_HEADS // 2):
                rj = slice(jp * TQ, (jp + 1) * TQ)
                acc = (acc + jnp.maximum(s_lo[rj, ls], 0.0) * wb_scr[2 * jp]
                       + jnp.maximum(s_hi[rj, ls], 0.0) * wb_scr[2 * jp + 1])
            bits = pltpu.bitcast(acc + 0.0, jnp.int32)
            key = jnp.where(bits >= 0, bits, bits ^ 0x7FFFFFFF)
            causal = (kb * KB + half * HALF + col) <= (qi * TQ + row)
            key_scr[kb, :, ls] = jnp.where(causal, key, INT_MIN)
        keyt = key_scr[kb].T
        khi_scr[kb] = lax.shift_right_arithmetic(keyt, 16).astype(jnp.int16)
        klo_scr[kb] = ((keyt & 0xFFFF) + I16_MIN).astype(jnp.int16)
        return carry

    lax.fori_loop(0, nkb, score_block, 0)

    def count(ref, cand, strictly=False):
        c16 = jnp.broadcast_to(cand.astype(jnp.int16), (PACK16, TQ))
        cnt_scr[...] = jnp.zeros((PACK16, TQ), jnp.int16)

        def run(base, n):
            parts = [jnp.zeros((PACK16, TQ), jnp.int16) for _ in range(COUNT_CHAINS)]
            for t in range(n):
                v = ref[base + t]
                for r in range(KB // PACK16):
                    blk = v[r * PACK16:(r + 1) * PACK16]
                    hit = (blk > c16) if strictly else (blk >= c16)
                    parts[r % COUNT_CHAINS] = parts[r % COUNT_CHAINS] + jnp.where(hit, jnp.int16(1), jnp.int16(0))
            cnt_scr[...] = cnt_scr[...] + functools.reduce(lambda a, b: a + b, parts)

        base = jnp.int32(0)
        for n in RUN_LENGTHS:
            take = (nkb & n) != 0
            pl.when(take)(functools.partial(run, base, n))
            base = base + jnp.where(take, n, 0)
        return jnp.sum(cnt_scr[...].astype(jnp.int32), axis=0, keepdims=True)

    def bisect16(ref, need):
        zero = jnp.zeros((1, TQ), jnp.int32)
        prefix = jnp.where(count(ref, zero) >= need, zero, I16_MIN)

        def bit_pass(it, prefix):
            cand = prefix + lax.shift_left(jnp.int32(1), 14 - it)
            return jnp.where(count(ref, cand) >= need, cand, prefix)

        return lax.fori_loop(0, 15, bit_pass, prefix)

    topk = jnp.full((1, TQ), TOPK, jnp.int32)
    p_hi = bisect16(khi_scr, topk)
    above = count(khi_scr, p_hi, strictly=True)
    p_hi16 = jnp.broadcast_to(p_hi.astype(jnp.int16), (PACK16, TQ))

    def narrow(kb, carry):
        hi, lo = khi_scr[kb], klo_scr[kb]
        for r in range(KB // PACK16):
            rs = slice(r * PACK16, (r + 1) * PACK16)
            klo_scr[kb, rs, :] = jnp.where(hi[rs] == p_hi16, lo[rs], jnp.int16(I16_MIN))
        return carry

    lax.fori_loop(0, nkb, narrow, 0)
    p_lo = bisect16(klo_scr, topk - above)
    prefix = p_hi * 65536 + (p_lo - I16_MIN)
    thr = jnp.maximum(prefix, INT_MIN + 1)
    thrn_scr[...] = jnp.broadcast_to(thr, (KB, TQ)).T

    p_lo16 = jnp.broadcast_to(p_lo.astype(jnp.int16), (PACK16, TQ))
    sub16 = lax.broadcasted_iota(jnp.int32, (PACK16, TQ), 0)

    def tie_blocks(kb, fn, carry):
        hi, lo = khi_scr[kb], klo_scr[kb]
        for r in range(KB // PACK16):
            rs = slice(r * PACK16, (r + 1) * PACK16)
            carry = fn(kb, r, rs, (hi[rs] == p_hi16) & (lo[rs] == p_lo16), carry)
        return carry

    def count_tied_and_above(kb, cnts):
        n_tied, n_above = cnts
        hi, lo = khi_scr[kb], klo_scr[kb]
        one, nil = jnp.int16(1), jnp.int16(0)
        for r in range(KB // PACK16):
            rs = slice(r * PACK16, (r + 1) * PACK16)
            n_tied = n_tied + jnp.where((hi[rs] == p_hi16) & (lo[rs] == p_lo16), one, nil)
            n_above = n_above + jnp.where(lo[rs] > p_lo16, one, nil)
        return n_tied, n_above

    zeros16 = jnp.zeros((PACK16, TQ), jnp.int16)
    tied, above_lo = lax.fori_loop(0, nkb, count_tied_and_above, (zeros16, zeros16))
    tied = jnp.sum(tied.astype(jnp.int32), axis=0, keepdims=True)
    above_lo = jnp.sum(above_lo.astype(jnp.int32), axis=0, keepdims=True)
    room = topk - above - above_lo
    trim = (tied > room) & (prefix != INT_MIN)

    def madd_block(kb, carry):
        madd_scr[kb] = jnp.where(key_scr[kb] >= thrn_scr[...], 0.0, NEG_BIG)
        return carry

    lax.fori_loop(0, nkb, madd_block, 0)

    if skip_first:
        @pl.when(qi == 0)
        def _():
            t_pos = lax.broadcasted_iota(jnp.int32, (TQ, KB), 0)
            s_pos = lax.broadcasted_iota(jnp.int32, (TQ, KB), 1)
            madd_scr[0] = jnp.where(s_pos <= t_pos, 0.0, NEG_BIG)

    @pl.when(jnp.max(trim.astype(jnp.int32)) > 0)
    def _():
        def mark(kb, carry):
            def put(kb, r, rs, tied, c):
                neg_pos = (-(kb * KB + r * PACK16) - sub16).astype(jnp.int16)
                khi_scr[kb, rs, :] = jnp.where(tied, neg_pos, jnp.int16(I16_MIN))
                return c
            return tie_blocks(kb, put, carry)

        lax.fori_loop(0, nkb, mark, 0)
        cut = jnp.where(trim, -bisect16(khi_scr, room), SEQ)
        cutn_scr[...] = jnp.broadcast_to(cut, (KB, TQ)).T

        def madd_ties(kb, carry):
            key = key_scr[kb]
            pos = kb * KB + lax.broadcasted_iota(jnp.int32, (TQ, KB), 1)
            keep_tie = jnp.where(pos <= cutn_scr[...], 0.0, NEG_BIG)
            madd_scr[kb] = jnp.where(key > thrn_scr[...], 0.0,
                                     jnp.where(key == thrn_scr[...], keep_tie, NEG_BIG))
            return carry

        lax.fori_loop(0, nkb, madd_ties, 0)

    def attend(kb, slot, first=False):
        rs = key_rows(kb)
        ones_blk = jnp.ones((KB, A_HEAD_DIM), bf16)
        for h in range(A_HEADS):
            hs = slice(h * A_HEAD_DIM, (h + 1) * A_HEAD_DIM)
            lg = _dot_nt(q_ref[0, :, hs], k_ref[0, rs, hs]) + madd_scr[kb]
            if slot is not None:
                lg = lg + band_ref[h, slot]
            m_new = jnp.max(lg, axis=-1, keepdims=True)
            if first:
                m_new = jnp.broadcast_to(m_new, (TQ, HALF))
            else:
                m_old = m_scr[h]
                m_new = jnp.maximum(m_old, m_new)
                alpha = jnp.exp2(m_old - m_new)
            p = jnp.concatenate([jnp.exp2(lg[:, :HALF] - m_new), jnp.exp2(lg[:, HALF:] - m_new)], axis=1)
            v_ones = jnp.concatenate([v_ref[0, rs, hs], ones_blk], axis=1)
            pv = jnp.dot(p.astype(bf16), v_ones, preferred_element_type=jnp.float32)
            if first:
                accv_scr[h] = pv[:, :A_HEAD_DIM]
                accl_scr[h] = pv[:, A_HEAD_DIM:]
            else:
                accv_scr[h] = accv_scr[h] * alpha + pv[:, :A_HEAD_DIM]
                accl_scr[h] = accl_scr[h] * alpha + pv[:, A_HEAD_DIM:]
            m_scr[h] = m_new

    attend(qi, 1, first=True)

    @pl.when(qi >= 1)
    def _():
        attend(qi - 1, 0)

    def far_block(kb, carry):
        attend(kb, None)
        return carry

    lax.fori_loop(0, qi - 1, far_block, 0)

    for h in range(A_HEADS):
        hs = slice(h * A_HEAD_DIM, (h + 1) * A_HEAD_DIM)
        g = ag_ref[0, :, hs]
        o_ref[0, :, hs] = (accv_scr[h] / accl_scr[h] * (g * jax.nn.sigmoid(g))).astype(o_ref.dtype)


def _dsa_attention(proj_b, proj_f, band):
    cb = lambda name: BF_OFF[name] // A_WIDTH
    once = pl.Buffered(1)
    return pl.pallas_call(
        _dsa_kernel,
        grid=(BATCH, NQ),
        in_specs=[
            pl.BlockSpec((1, TQ, A_WIDTH), lambda b, i: (b, i, cb("aq"))),
            pl.BlockSpec((1, SEQ, A_WIDTH), lambda b, i: (b, 0, cb("ak"))),
            pl.BlockSpec((1, SEQ, A_WIDTH), lambda b, i: (b, 0, cb("av"))),
            pl.BlockSpec((1, TQ, A_WIDTH), lambda b, i: (b, i, cb("iq"))),
            pl.BlockSpec((1, SEQ, LANE), lambda b, i: (b, 0, F32_OFF["ik"] // LANE)),
            pl.BlockSpec((1, TQ, LANE), lambda b, i: (b, i, F32_OFF["iw"] // LANE)),
            pl.BlockSpec((1, TQ, A_WIDTH), lambda b, i: (b, i, F32_OFF["ag"] // A_WIDTH)),
            pl.BlockSpec((A_HEADS, 2, TQ, KB), lambda b, i: (0, 0, 0, 0), pipeline_mode=once),
        ],
        out_specs=pl.BlockSpec((1, TQ, A_WIDTH), lambda b, i: (b, i, 0)),
        out_shape=jax.ShapeDtypeStruct((BATCH, SEQ, A_WIDTH), jnp.bfloat16),
        scratch_shapes=[
            pltpu.VMEM((SEQ, LANE), jnp.bfloat16),
            pltpu.VMEM((SEQ, LANE), jnp.bfloat16),
            pltpu.VMEM((IDX_HEADS, TQ, HALF), jnp.float32),
            pltpu.VMEM((IDX_HEADS // 2 * TQ, LANE), jnp.bfloat16),
            pltpu.VMEM((NKB, TQ, KB), jnp.int32),
            pltpu.VMEM((NKB, KB, TQ), jnp.int16),
            pltpu.VMEM((NKB, KB, TQ), jnp.int16),
            pltpu.VMEM((TQ, KB), jnp.int32),
            pltpu.VMEM((TQ, KB), jnp.int32),
            pltpu.VMEM((PACK16, TQ), jnp.int16),
            pltpu.VMEM((NKB, TQ, KB), jnp.float32),
            pltpu.VMEM((A_HEADS, TQ, HALF), jnp.float32),
            pltpu.VMEM((A_HEADS, TQ, A_HEAD_DIM), jnp.float32),
            pltpu.VMEM((A_HEADS, TQ, A_HEAD_DIM), jnp.float32),
        ],
        compiler_params=pltpu.CompilerParams(
            dimension_semantics=("arbitrary", "arbitrary"), vmem_limit_bytes=VMEM_LIMIT),
        name="dsa_attention",
    )(proj_b, proj_b, proj_b, proj_b, proj_f, proj_f, proj_f, band)


GLA_CT = 512
GLA_C = 64
GLA_SC = 256
GLA_NCS = GLA_SC // GLA_C


def _gla_kernel(bq_ref, bk_ref, bv_ref, bg_ref, ba_ref, wup_ref, balpha_ref, g_ref, o_ref, st_scr):
    @pl.when(pl.program_id(1) == 0)
    def _():
        st_scr[...] = jnp.zeros_like(st_scr)

    bf16 = jnp.bfloat16
    rr = lax.broadcasted_iota(jnp.int32, (GLA_SC, GLA_SC), 0)
    cc = lax.broadcasted_iota(jnp.int32, (GLA_SC, GLA_SC), 1)
    tri = (rr >= cc) & (rr // GLA_C == cc // GLA_C)
    tri_bf = tri.astype(bf16)
    er = lax.broadcasted_iota(jnp.int32, (GLA_SC, GLA_NCS * B_DK), 0)
    ec = lax.broadcasted_iota(jnp.int32, (GLA_SC, GLA_NCS * B_DK), 1)
    own_block = (er // GLA_C) == (ec // B_DK)
    wup = wup_ref[...].astype(bf16)
    balpha = balpha_ref[...]
    gain = g_ref[...]
    heads = range(B_HEADS)

    def expand(a):
        return jnp.where(own_block, jnp.concatenate([a] * GLA_NCS, axis=1), jnp.zeros((), a.dtype))

    def body(sc, carry):
        rows = pl.ds(pl.multiple_of(sc * GLA_SC, GLA_SC), GLA_SC)
        ba = ba_ref[0, rows, :][:, :GATE_RANK].astype(bf16)
        pre = jnp.dot(ba, wup, preferred_element_type=jnp.float32) + balpha
        log_a = (jnp.minimum(pre, 0.0) - jnp.log1p(jnp.exp(-jnp.abs(pre)))) * (1.0 / GATE_TEMP)
        la_hi = log_a.astype(bf16)
        la_lo = (log_a - la_hi.astype(jnp.float32)).astype(bf16)
        bcum = (jnp.dot(tri_bf, la_hi, preferred_element_type=jnp.float32)
                + jnp.dot(tri_bf, la_lo, preferred_element_type=jnp.float32))
        ks = [slice(h * B_DK, (h + 1) * B_DK) for h in heads]
        vs = [slice(h * B_DV, (h + 1) * B_DV) for h in heads]
        qe, ke, kd, dec, v = [], [], [], [], []
        for h in heads:
            b = bcum[:, ks[h]]
            b_end = b.reshape(GLA_NCS, GLA_C, B_DK)[:, GLA_C - 1:GLA_C, :]
            b_end_rows = jnp.broadcast_to(b_end, (GLA_NCS, GLA_C, B_DK)).reshape(GLA_SC, B_DK)
            q = bq_ref[0, rows, ks[h]] * (B_DK ** -0.5)
            k = bk_ref[0, rows, ks[h]]
            qe.append((q * jnp.exp(b)).astype(bf16))
            ke.append((k * jnp.exp(-b)).astype(bf16))
            kd.append((k * jnp.exp(b_end_rows - b)).astype(bf16))
            dec.append(jnp.exp(b_end))
            v.append(bv_ref[0, rows, vs[h]].astype(bf16))
        attn = [jnp.where(tri, _dot_nt(qe[h], ke[h]), 0.0).astype(bf16) for h in heads]
        o_intra = [jnp.dot(attn[h], v[h], preferred_element_type=jnp.float32) for h in heads]
        upd = [_dot_tn(v[h], expand(kd[h])) for h in heads]
        o_inter = []
        for h in heads:
            st = st_scr[h]
            states = []
            for c in range(GLA_NCS):
                states.append(st.astype(bf16))
                st = st * dec[h][c] + upd[h][:, c * B_DK:(c + 1) * B_DK]
            st_scr[h] = st
            o_inter.append(_dot_nt(expand(qe[h]), jnp.concatenate(states, axis=1)))
        for h in heads:
            o = o_intra[h] + o_inter[h]
            on = o * lax.rsqrt(jnp.mean(o * o, axis=-1, keepdims=True) + EPS) * gain
            g = bg_ref[0, rows, vs[h]]
            o_ref[0, rows, vs[h]] = (on * (g * jax.nn.sigmoid(g))).astype(o_ref.dtype)
        return carry

    lax.fori_loop(0, GLA_CT // GLA_SC, body, 0)


def _gla(proj_f, w_alpha_up, b_alpha, gla_g):
    return pl.pallas_call(
        _gla_kernel,
        grid=(BATCH, SEQ // GLA_CT),
        in_specs=[
            pl.BlockSpec((1, GLA_CT, B_KEY_WIDTH), lambda b, t: (b, t, F32_OFF["bq"] // B_KEY_WIDTH)),
            pl.BlockSpec((1, GLA_CT, B_KEY_WIDTH), lambda b, t: (b, t, F32_OFF["bk"] // B_KEY_WIDTH)),
            pl.BlockSpec((1, GLA_CT, B_WIDTH), lambda b, t: (b, t, F32_OFF["bv"] // B_WIDTH)),
            pl.BlockSpec((1, GLA_CT, B_WIDTH), lambda b, t: (b, t, F32_OFF["bg"] // B_WIDTH)),
            pl.BlockSpec((1, GLA_CT, LANE), lambda b, t: (b, t, F32_OFF["ba"] // LANE)),
            pl.BlockSpec((GATE_RANK, B_KEY_WIDTH), lambda b, t: (0, 0)),
            pl.BlockSpec((1, B_KEY_WIDTH), lambda b, t: (0, 0)),
            pl.BlockSpec((1, B_DV), lambda b, t: (0, 0)),
        ],
        out_specs=pl.BlockSpec((1, GLA_CT, B_WIDTH), lambda b, t: (b, t, 0)),
        out_shape=jax.ShapeDtypeStruct((BATCH, SEQ, B_WIDTH), jnp.bfloat16),
        scratch_shapes=[pltpu.VMEM((B_HEADS, B_DV, B_DK), jnp.float32)],
        compiler_params=pltpu.CompilerParams(
            dimension_semantics=("arbitrary", "arbitrary"), vmem_limit_bytes=VMEM_LIMIT),
        name="gla",
    )(proj_f, proj_f, proj_f, proj_f, proj_f, w_alpha_up, b_alpha.reshape(1, B_KEY_WIDTH),
      gla_g.reshape(1, B_DV))


OUT_TM = 512


def _out_kernel(a_ref, b_ref, wa_ref, wb_ref, x_ref, gate_ref, fg_ref, o_ref, *, final_norm):
    y = (jnp.dot(a_ref[...], wa_ref[...], preferred_element_type=jnp.float32)
         + jnp.dot(b_ref[...], wb_ref[...], preferred_element_type=jnp.float32))
    xn = x_ref[...] + gate_ref[0] * y
    if final_norm:
        r = xn * lax.rsqrt(jnp.mean(xn * xn, axis=-1, keepdims=True) + EPS)
        xn = r * fg_ref[...]
    o_ref[...] = xn


def _out_proj(a_out, b_out, w_out_bf, x2d, gate, final_g, layer, final_norm):
    m = x2d.shape[0]
    tiles_per_batch = SEQ // OUT_TM
    return pl.pallas_call(
        functools.partial(_out_kernel, final_norm=final_norm),
        grid=(m // OUT_TM,),
        in_specs=[
            pl.BlockSpec((OUT_TM, A_WIDTH), lambda i: (i, 0)),
            pl.BlockSpec((OUT_TM, B_WIDTH), lambda i: (i, 0)),
            pl.BlockSpec((None, A_WIDTH, D_MODEL), lambda i: (layer, 0, 0)),
            pl.BlockSpec((None, B_WIDTH, D_MODEL), lambda i: (layer, 1, 0)),
            pl.BlockSpec((OUT_TM, D_MODEL), lambda i: (i, 0)),
            pl.BlockSpec((1, 1, D_MODEL), lambda i: (i // tiles_per_batch, 0, 0)),
            pl.BlockSpec((1, D_MODEL), lambda i: (0, 0)),
        ],
        out_specs=pl.BlockSpec((OUT_TM, D_MODEL), lambda i: (i, 0)),
        out_shape=jax.ShapeDtypeStruct((m, D_MODEL), jnp.float32),
        compiler_params=pltpu.CompilerParams(
            dimension_semantics=("arbitrary",), vmem_limit_bytes=VMEM_LIMIT),
        name="out_proj",
    )(a_out, b_out, w_out_bf, w_out_bf, x2d, gate, final_g.reshape(1, D_MODEL))


def kernel(x, c, w_ada, b_ada, norm_g, w_in, w_alpha_up, b_alpha, gla_g, w_out, rel_bias, final_g):
    mod = _adaln_mod(c, w_ada, b_ada)
    band = _bias_band(rel_bias)
    w_in_p = _pack_w_in(jnp.swapaxes(w_in, 1, 2))
    w_out_bf = w_out.astype(jnp.bfloat16)
    x2d = x.reshape(BATCH * SEQ, D_MODEL)
    for l in range(DEPTH):
        shift = mod[l, :, 0:D_MODEL].reshape(BATCH, 1, D_MODEL)
        scale = mod[l, :, D_MODEL:2 * D_MODEL].reshape(BATCH, 1, D_MODEL)
        gate = mod[l, :, 2 * D_MODEL:].reshape(BATCH, 1, D_MODEL)
        proj_b, proj_f = _norm_proj(x2d, shift, scale, norm_g[l], w_in_p, l)
        proj_b = proj_b.reshape(BATCH, SEQ, NB_COLS)
        proj_f = proj_f.reshape(BATCH, SEQ, NF_COLS)
        a_out = _dsa_attention(proj_b, proj_f, band)
        b_out = _gla(proj_f, w_alpha_up[l], b_alpha[l], gla_g[l])
        x2d = _out_proj(a_out.reshape(BATCH * SEQ, A_WIDTH), b_out.reshape(BATCH * SEQ, B_WIDTH),
                        w_out_bf, x2d, gate, final_g, l, final_norm=(l == DEPTH - 1))
    return x2d.reshape(BATCH, SEQ, D_MODEL)
```

```python
import functools
import math

import numpy as np
import jax
import jax.numpy as jnp
from jax import lax
from jax.experimental import pallas as pl
from jax.experimental.pallas import tpu as pltpu

D_MODEL = 2048
BATCH = 4
SEQ = 2048
DEPTH = 4
A_WIDTH = 1024
A_HEADS = 8
A_HEAD_DIM = 128
IDX_HEADS = 16
IDX_DIM = 64
TOPK = min(256, SEQ // 4)
B_WIDTH = 1024
B_HEADS = 4
B_KEY_WIDTH = 512
B_DK = 128
B_DV = 256
GATE_RANK = 16
GATE_TEMP = 16.0
NUM_BUCKETS = 32
MAX_DISTANCE = 128
EPS = 1e-6

IN_WIDTHS = (A_WIDTH, A_WIDTH, A_WIDTH, A_WIDTH, IDX_HEADS * IDX_DIM, IDX_DIM, IDX_HEADS,
             B_KEY_WIDTH, B_KEY_WIDTH, B_WIDTH, B_WIDTH, GATE_RANK)
IN_NAMES = ("aq", "ak", "av", "ag", "iq", "ik", "iw", "bq", "bk", "bv", "bg", "ba")
IN_OFFSETS = dict(zip(IN_NAMES, np.concatenate([[0], np.cumsum(IN_WIDTHS)[:-1]]).tolist()))
IN_WIDTH_OF = dict(zip(IN_NAMES, IN_WIDTHS))
IN_COLS = sum(IN_WIDTHS)

LANE = 128
VMEM_LIMIT = 52 * 1024 * 1024

PROJ_TN = 512
BF_SEGS = (("aq", 1024), ("ak", 1024), ("av", 1024), ("iq", 1024))
F32_SEGS = (("ag", 1024), ("bv", 1024), ("bg", 1024), ("bq", 512), ("bk", 512),
            ("ik", LANE), ("iw", LANE), ("ba", LANE))
NB_COLS = sum(w for _, w in BF_SEGS)
NF_USED = sum(w for _, w in F32_SEGS)
NF_COLS = -(-NF_USED // PROJ_TN) * PROJ_TN
NP_COLS = NB_COLS + NF_COLS


def _seg_offsets(segs):
    offs, o = {}, 0
    for name, w in segs:
        offs[name] = o
        o += w
    return offs


BF_OFF = _seg_offsets(BF_SEGS)
F32_OFF = _seg_offsets(F32_SEGS)

INT_MIN = -2 ** 31
NEG_BIG = -1e30
LOG2E = math.log2(math.e)
AQ_SCALE = A_HEAD_DIM ** -0.5 * LOG2E
I16_MIN = -2 ** 15
PACK16 = 16


PACK_COLS = 256


def _pack_kernel(w_ref, o_ref):
    dst = 0
    for name, width in BF_SEGS + F32_SEGS:
        src, used = IN_OFFSETS[name], IN_WIDTH_OF[name]
        o_ref[0, dst:dst + used, :] = w_ref[0, src:src + used, :].astype(jnp.bfloat16)
        if width > used:
            o_ref[0, dst + used:dst + width, :] = jnp.zeros((width - used, PACK_COLS), jnp.bfloat16)
        dst += width
    if dst < NP_COLS:
        o_ref[0, dst:, :] = jnp.zeros((NP_COLS - dst, PACK_COLS), jnp.bfloat16)


def _pack_w_in(w_in_t):
    return pl.pallas_call(
        _pack_kernel,
        grid=(DEPTH, D_MODEL // PACK_COLS),
        in_specs=[pl.BlockSpec((1, IN_COLS, PACK_COLS), lambda l, c: (l, 0, c))],
        out_specs=pl.BlockSpec((1, NP_COLS, PACK_COLS), lambda l, c: (l, 0, c)),
        out_shape=jax.ShapeDtypeStruct((DEPTH, NP_COLS, D_MODEL), jnp.bfloat16),
        compiler_params=pltpu.CompilerParams(
            dimension_semantics=("arbitrary", "arbitrary"), vmem_limit_bytes=VMEM_LIMIT),
        name="pack_w_in",
    )(w_in_t)


MOD_TN = 768


def _mod_kernel(c_ref, w_ref, b_ref, o_ref):
    c = c_ref[...]
    c_act = c * jax.nn.sigmoid(c)
    acc = jnp.dot(c_act.astype(jnp.bfloat16), w_ref[0].astype(jnp.bfloat16),
                  preferred_element_type=jnp.float32)
    o_ref[0] = acc + b_ref[0]


def _adaln_mod(c, w_ada, b_ada):
    cp = jnp.pad(c, ((0, 8 - BATCH), (0, 0)))
    out = pl.pallas_call(
        _mod_kernel,
        grid=(DEPTH, 3 * D_MODEL // MOD_TN),
        in_specs=[
            pl.BlockSpec((8, D_MODEL), lambda l, j: (0, 0)),
            pl.BlockSpec((1, D_MODEL, MOD_TN), lambda l, j: (l, 0, j)),
            pl.BlockSpec((1, 1, MOD_TN), lambda l, j: (l, 0, j)),
        ],
        out_specs=pl.BlockSpec((1, 8, MOD_TN), lambda l, j: (l, 0, j)),
        out_shape=jax.ShapeDtypeStruct((DEPTH, 8, 3 * D_MODEL), jnp.float32),
        compiler_params=pltpu.CompilerParams(
            dimension_semantics=("arbitrary", "arbitrary"), vmem_limit_bytes=VMEM_LIMIT),
        name="adaln_mod",
    )(cp, w_ada, b_ada.reshape(DEPTH, 1, 3 * D_MODEL))
    return out[:, :BATCH]


PROJ_TM = 1024
NB_TILES = NB_COLS // PROJ_TN
N_COL_TILES = NP_COLS // PROJ_TN
AHEAD_ROWS = 64
LATE_SWITCH_STEP = 4
assert PROJ_TM // AHEAD_ROWS < N_COL_TILES


def _proj_kernel(xa_ref, xb_ref, shift_ref, scale_ref, g_ref, w_ref, ob_ref, of_ref, ha_scr, hb_scr):
    half_rows = PROJ_TM // 2
    half_chunks = half_rows // AHEAD_ROWS
    i, j = pl.program_id(0), pl.program_id(1)

    def normed(x):
        r = x * lax.rsqrt(jnp.mean(x * x, axis=-1, keepdims=True) + EPS)
        return ((r * g_ref[...]) * (1.0 + scale_ref[0]) + shift_ref[0]).astype(jnp.bfloat16)

    @pl.when((i == 0) & (j == 0))
    def _():
        ha_scr[:half_rows, :] = normed(xa_ref[...])
        ha_scr[half_rows:, :] = normed(xb_ref[...])

    def step(h_cur, h_next):
        def norm_ahead():
            c = jnp.clip(j - (i == 0).astype(jnp.int32), 0, PROJ_TM // AHEAD_ROWS - 1)
            rows = pl.ds(pl.multiple_of(c * AHEAD_ROWS, AHEAD_ROWS), AHEAD_ROWS)
            src = pl.ds(pl.multiple_of((c % half_chunks) * AHEAD_ROWS, AHEAD_ROWS), AHEAD_ROWS)
            h_next[rows, :] = normed(jnp.where(c < half_chunks, xa_ref[src, :], xb_ref[src, :]))

        @pl.when(j < NB_TILES)
        def _():
            norm_ahead()
            out_scale = jnp.where(j < A_WIDTH // PROJ_TN, AQ_SCALE, 1.0)
            ob_ref[...] = (_dot_nt(h_cur[...], w_ref[...]) * out_scale).astype(jnp.bfloat16)

        @pl.when(j >= NB_TILES)
        def _():
            norm_ahead()
            of_ref[...] = _dot_nt(h_cur[...], w_ref[...])

    @pl.when(i % 2 == 0)
    def _():
        step(ha_scr, hb_scr)

    @pl.when(i % 2 == 1)
    def _():
        step(hb_scr, ha_scr)


def _norm_proj(x2d, shift, scale, norm_g, w_p, layer):
    m = x2d.shape[0]
    n_row_tiles = m // PROJ_TM
    tiles_per_batch = SEQ // PROJ_TM

    def ahead(i, j):
        return jnp.where((i == 0) & (j == 0), 0, jnp.minimum(i + 1, n_row_tiles - 1))

    def ahead_late(i, j):
        return jnp.where(j < LATE_SWITCH_STEP, i, jnp.minimum(i + 1, n_row_tiles - 1))

    return pl.pallas_call(
        _proj_kernel,
        grid=(n_row_tiles, N_COL_TILES),
        in_specs=[
            pl.BlockSpec((PROJ_TM // 2, D_MODEL), lambda i, j: (2 * ahead(i, j), 0)),
            pl.BlockSpec((PROJ_TM // 2, D_MODEL), lambda i, j: (2 * ahead_late(i, j) + 1, 0)),
            pl.BlockSpec((1, 1, D_MODEL), lambda i, j: (ahead(i, j) // tiles_per_batch, 0, 0)),
            pl.BlockSpec((1, 1, D_MODEL), lambda i, j: (ahead(i, j) // tiles_per_batch, 0, 0)),
            pl.BlockSpec((1, D_MODEL), lambda i, j: (0, 0)),
            pl.BlockSpec((None, PROJ_TN, D_MODEL), lambda i, j: (layer, j, 0)),
        ],
        out_specs=[
            pl.BlockSpec((PROJ_TM, PROJ_TN), lambda i, j: (i, jnp.minimum(j, NB_TILES - 1))),
            pl.BlockSpec((PROJ_TM, PROJ_TN), lambda i, j: (i, jnp.maximum(j - NB_TILES, 0))),
        ],
        out_shape=[
            jax.ShapeDtypeStruct((m, NB_COLS), jnp.bfloat16),
            jax.ShapeDtypeStruct((m, NF_COLS), jnp.float32),
        ],
        scratch_shapes=[pltpu.VMEM((PROJ_TM, D_MODEL), jnp.bfloat16),
                        pltpu.VMEM((PROJ_TM, D_MODEL), jnp.bfloat16)],
        compiler_params=pltpu.CompilerParams(
            dimension_semantics=("arbitrary", "arbitrary"), vmem_limit_bytes=VMEM_LIMIT),
        name="norm_proj",
    )(x2d, x2d, shift, scale, norm_g.reshape(1, D_MODEL), w_p)


TQ = 256
KB = 256
NQ = SEQ // TQ
NKB = SEQ // KB
HALF = KB // 2
RUN_LENGTHS = tuple(1 << b for b in reversed(range(NKB.bit_length())))
COUNT_CHAINS = 4


def _t5_bucket_np(d):
    max_exact = NUM_BUCKETS // 2
    d = np.maximum(d, 0)
    df = np.maximum(d, 1).astype(np.float32)
    large = max_exact + (np.log(df / np.float32(max_exact)) / np.float32(math.log(MAX_DISTANCE / max_exact))
                         * np.float32(NUM_BUCKETS - max_exact)).astype(np.int32)
    large = np.minimum(large, NUM_BUCKETS - 1)
    return np.where(d < max_exact, d, large).astype(np.int32)


assert int(_t5_bucket_np(np.arange(KB, 2 * SEQ)).min()) == NUM_BUCKETS - 1


def _band_buckets():
    tl = np.arange(TQ)[:, None]
    u = np.arange(KB)[None, :]
    prev = _t5_bucket_np(KB + tl - u)
    diag = _t5_bucket_np(tl - u)
    return np.stack([prev, diag]).astype(np.int32)


BAND_ROWS = 64


def _band_kernel(rb_ref, bucket_ref, o_ref):
    h = pl.program_id(0)
    far = rb_ref[NUM_BUCKETS - 1, h]

    def rows(r, carry):
        rs = pl.ds(pl.multiple_of(r * BAND_ROWS, BAND_ROWS), BAND_ROWS)
        bucket = bucket_ref[0, rs, :]
        acc = jnp.zeros(bucket.shape, jnp.float32)
        for b in range(NUM_BUCKETS):
            acc = jnp.where(bucket == b, rb_ref[b, h] - far, acc)
        o_ref[0, 0, rs, :] = acc * LOG2E
        return carry

    lax.fori_loop(0, TQ // BAND_ROWS, rows, 0)


def _bias_band(rel_bias):
    return pl.pallas_call(
        _band_kernel,
        grid=(A_HEADS, 2),
        in_specs=[
            pl.BlockSpec(memory_space=pltpu.SMEM),
            pl.BlockSpec((1, TQ, KB), lambda h, s: (s, 0, 0)),
        ],
        out_specs=pl.BlockSpec((1, 1, TQ, KB), lambda h, s: (h, s, 0, 0)),
        out_shape=jax.ShapeDtypeStruct((A_HEADS, 2, TQ, KB), jnp.float32),
        name="bias_band",
    )(rel_bias, jnp.asarray(_band_buckets()))


def _dot_nt(a, b):
    return lax.dot_general(a, b, (((1,), (1,)), ((), ())), preferred_element_type=jnp.float32)


def _dot_tn(a, b):
    return lax.dot_general(a, b, (((0,), (0,)), ((), ())), preferred_element_type=jnp.float32)


def _dsa_kernel(q_ref, k_ref, v_ref, iq_ref, ik_ref, iw_ref, ag_ref, band_ref, o_ref,
                iklo_scr, ikhi_scr, wb_scr, iq2_scr, key_scr, khi_scr, klo_scr, thrn_scr,
                cutn_scr, cnt_scr, madd_scr, m_scr, accv_scr, accl_scr):
    qi = pl.program_id(1)
    skip_first = TQ <= TOPK
    nkb = jnp.where(qi == 0, 0, qi + 1) if skip_first else qi + 1
    bf16 = jnp.bfloat16

    def key_rows(kb):
        return pl.ds(pl.multiple_of(kb * KB, KB), KB)

    @pl.when(qi == 0)
    def _():
        def prep(c, carry):
            rs = key_rows(c)
            ik = ik_ref[0, rs, :]
            iklo_scr[rs, :] = ik.astype(bf16)
            ikhi_scr[rs, :] = pltpu.roll(ik, IDX_DIM, axis=1).astype(bf16)
            return carry

        lax.fori_loop(0, NKB, prep, 0)

    iw = iw_ref[0][:, :IDX_HEADS] * (IDX_HEADS ** -0.5 * IDX_DIM ** -0.5)
    for j in range(IDX_HEADS):
        wb_scr[j] = jnp.broadcast_to(iw[:, j:j + 1], (TQ, HALF))
    for jp in range(IDX_HEADS // 2):
        iq2_scr[jp * TQ:(jp + 1) * TQ, :] = iq_ref[0, :, jp * LANE:(jp + 1) * LANE]

    row = lax.broadcasted_iota(jnp.int32, (TQ, HALF), 0)
    col = lax.broadcasted_iota(jnp.int32, (TQ, HALF), 1)

    def score_block(kb, carry):
        rs = key_rows(kb)
        s_lo = _dot_nt(iq2_scr[...], iklo_scr[rs, :])
        s_hi = _dot_nt(iq2_scr[...], ikhi_scr[rs, :])
        for half in range(2):
            ls = slice(half * HALF, (half + 1) * HALF)
            acc = jnp.zeros((TQ, HALF), jnp.float32)
            for jp in range(IDX_HEADS // 2):
                rj = slice(jp * TQ, (jp + 1) * TQ)
                acc = (acc + jnp.maximum(s_lo[rj, ls], 0.0) * wb_scr[2 * jp]
                       + jnp.maximum(s_hi[rj, ls], 0.0) * wb_scr[2 * jp + 1])
            bits = pltpu.bitcast(acc + 0.0, jnp.int32)
            key = jnp.where(bits >= 0, bits, bits ^ 0x7FFFFFFF)
            causal = (kb * KB + half * HALF + col) <= (qi * TQ + row)
            key_scr[kb, :, ls] = jnp.where(causal, key, INT_MIN)
        keyt = key_scr[kb].T
        khi_scr[kb] = lax.shift_right_arithmetic(keyt, 16).astype(jnp.int16)
        klo_scr[kb] = ((keyt & 0xFFFF) + I16_MIN).astype(jnp.int16)
        return carry

    lax.fori_loop(0, nkb, score_block, 0)

    def count(ref, cand, strictly=False):
        c16 = jnp.broadcast_to(cand.astype(jnp.int16), (PACK16, TQ))
        cnt_scr[...] = jnp.zeros((PACK16, TQ), jnp.int16)

        def run(base, n):
            parts = [jnp.zeros((PACK16, TQ), jnp.int16) for _ in range(COUNT_CHAINS)]
            for t in range(n):
                v = ref[base + t]
                for r in range(KB // PACK16):
                    blk = v[r * PACK16:(r + 1) * PACK16]
                    hit = (blk > c16) if strictly else (blk >= c16)
                    parts[r % COUNT_CHAINS] = parts[r % COUNT_CHAINS] + jnp.where(hit, jnp.int16(1), jnp.int16(0))
            cnt_scr[...] = cnt_scr[...] + functools.reduce(lambda a, b: a + b, parts)

        base = jnp.int32(0)
        for n in RUN_LENGTHS:
            take = (nkb & n) != 0
            pl.when(take)(functools.partial(run, base, n))
            base = base + jnp.where(take, n, 0)
        return jnp.sum(cnt_scr[...].astype(jnp.int32), axis=0, keepdims=True)

    def bisect16(ref, need):
        zero = jnp.zeros((1, TQ), jnp.int32)
        prefix = jnp.where(count(ref, zero) >= need, zero, I16_MIN)

        def bit_pass(it, prefix):
            cand = prefix + lax.shift_left(jnp.int32(1), 14 - it)
            return jnp.where(count(ref, cand) >= need, cand, prefix)

        return lax.fori_loop(0, 15, bit_pass, prefix)

    topk = jnp.full((1, TQ), TOPK, jnp.int32)
    p_hi = bisect16(khi_scr, topk)
    above = count(khi_scr, p_hi, strictly=True)
    p_hi16 = jnp.broadcast_to(p_hi.astype(jnp.int16), (PACK16, TQ))

    def narrow(kb, carry):
        hi, lo = khi_scr[kb], klo_scr[kb]
        for r in range(KB // PACK16):
            rs = slice(r * PACK16, (r + 1) * PACK16)
            klo_scr[kb, rs, :] = jnp.where(hi[rs] == p_hi16, lo[rs], jnp.int16(I16_MIN))
        return carry

    lax.fori_loop(0, nkb, narrow, 0)
    p_lo = bisect16(klo_scr, topk - above)
    prefix = p_hi * 65536 + (p_lo - I16_MIN)
    thr = jnp.maximum(prefix, INT_MIN + 1)
    thrn_scr[...] = jnp.broadcast_to(thr, (KB, TQ)).T

    p_lo16 = jnp.broadcast_to(p_lo.astype(jnp.int16), (PACK16, TQ))
    sub16 = lax.broadcasted_iota(jnp.int32, (PACK16, TQ), 0)

    def tie_blocks(kb, fn, carry):
        hi, lo = khi_scr[kb], klo_scr[kb]
        for r in range(KB // PACK16):
            rs = slice(r * PACK16, (r + 1) * PACK16)
            carry = fn(kb, r, rs, (hi[rs] == p_hi16) & (lo[rs] == p_lo16), carry)
        return carry

    def count_tied_and_above(kb, cnts):
        n_tied, n_above = cnts
        hi, lo = khi_scr[kb], klo_scr[kb]
        one, nil = jnp.int16(1), jnp.int16(0)
        for r in range(KB // PACK16):
            rs = slice(r * PACK16, (r + 1) * PACK16)
            n_tied = n_tied + jnp.where((hi[rs] == p_hi16) & (lo[rs] == p_lo16), one, nil)
            n_above = n_above + jnp.where(lo[rs] > p_lo16, one, nil)
        return n_tied, n_above

    zeros16 = jnp.zeros((PACK16, TQ), jnp.int16)
    tied, above_lo = lax.fori_loop(0, nkb, count_tied_and_above, (zeros16, zeros16))
    tied = jnp.sum(tied.astype(jnp.int32), axis=0, keepdims=True)
    above_lo = jnp.sum(above_lo.astype(jnp.int32), axis=0, keepdims=True)
    room = topk - above - above_lo
    trim = (tied > room) & (prefix != INT_MIN)

    def madd_block(kb, carry):
        madd_scr[kb] = jnp.where(key_scr[kb] >= thrn_scr[...], 0.0, NEG_BIG)
        return carry

    lax.fori_loop(0, nkb, madd_block, 0)

    if skip_first:
        @pl.when(qi == 0)
        def _():
            t_pos = lax.broadcasted_iota(jnp.int32, (TQ, KB), 0)
            s_pos = lax.broadcasted_iota(jnp.int32, (TQ, KB), 1)
            madd_scr[0] = jnp.where(s_pos <= t_pos, 0.0, NEG_BIG)

    @pl.when(jnp.max(trim.astype(jnp.int32)) > 0)
    def _():
        def mark(kb, carry):
            def put(kb, r, rs, tied, c):
                neg_pos = (-(kb * KB + r * PACK16) - sub16).astype(jnp.int16)
                khi_scr[kb, rs, :] = jnp.where(tied, neg_pos, jnp.int16(I16_MIN))
                return c
            return tie_blocks(kb, put, carry)

        lax.fori_loop(0, nkb, mark, 0)
        cut = jnp.where(trim, -bisect16(khi_scr, room), SEQ)
        cutn_scr[...] = jnp.broadcast_to(cut, (KB, TQ)).T

        def madd_ties(kb, carry):
            key = key_scr[kb]
            pos = kb * KB + lax.broadcasted_iota(jnp.int32, (TQ, KB), 1)
            keep_tie = jnp.where(pos <= cutn_scr[...], 0.0, NEG_BIG)
            madd_scr[kb] = jnp.where(key > thrn_scr[...], 0.0,
                                     jnp.where(key == thrn_scr[...], keep_tie, NEG_BIG))
            return carry

        lax.fori_loop(0, nkb, madd_ties, 0)

    def attend(kb, slot, first=False):
        rs = key_rows(kb)
        ones_blk = jnp.ones((KB, A_HEAD_DIM), bf16)
        for h in range(A_HEADS):
            hs = slice(h * A_HEAD_DIM, (h + 1) * A_HEAD_DIM)
            lg = _dot_nt(q_ref[0, :, hs], k_ref[0, rs, hs]) + madd_scr[kb]
            if slot is not None:
                lg = lg + band_ref[h, slot]
            m_new = jnp.max(lg, axis=-1, keepdims=True)
            if first:
                m_new = jnp.broadcast_to(m_new, (TQ, HALF))
            else:
                m_old = m_scr[h]
                m_new = jnp.maximum(m_old, m_new)
                alpha = jnp.exp2(m_old - m_new)
            p = jnp.concatenate([jnp.exp2(lg[:, :HALF] - m_new), jnp.exp2(lg[:, HALF:] - m_new)], axis=1)
            v_ones = jnp.concatenate([v_ref[0, rs, hs], ones_blk], axis=1)
            pv = jnp.dot(p.astype(bf16), v_ones, preferred_element_type=jnp.float32)
            if first:
                accv_scr[h] = pv[:, :A_HEAD_DIM]
                accl_scr[h] = pv[:, A_HEAD_DIM:]
            else:
                accv_scr[h] = accv_scr[h] * alpha + pv[:, :A_HEAD_DIM]
                accl_scr[h] = accl_scr[h] * alpha + pv[:, A_HEAD_DIM:]
            m_scr[h] = m_new

    attend(qi, 1, first=True)

    @pl.when(qi >= 1)
    def _():
        attend(qi - 1, 0)

    def far_block(kb, carry):
        attend(kb, None)
        return carry

    lax.fori_loop(0, qi - 1, far_block, 0)

    for h in range(A_HEADS):
        hs = slice(h * A_HEAD_DIM, (h + 1) * A_HEAD_DIM)
        g = ag_ref[0, :, hs]
        o_ref[0, :, hs] = (accv_scr[h] / accl_scr[h] * (g * jax.nn.sigmoid(g))).astype(o_ref.dtype)


def _dsa_attention(proj_b, proj_f, band):
    cb = lambda name: BF_OFF[name] // A_WIDTH
    once = pl.Buffered(1)
    return pl.pallas_call(
        _dsa_kernel,
        grid=(BATCH, NQ),
        in_specs=[
            pl.BlockSpec((1, TQ, A_WIDTH), lambda b, i: (b, i, cb("aq"))),
            pl.BlockSpec((1, SEQ, A_WIDTH), lambda b, i: (b, 0, cb("ak"))),
            pl.BlockSpec((1, SEQ, A_WIDTH), lambda b, i: (b, 0, cb("av"))),
            pl.BlockSpec((1, TQ, A_WIDTH), lambda b, i: (b, i, cb("iq"))),
            pl.BlockSpec((1, SEQ, LANE), lambda b, i: (b, 0, F32_OFF["ik"] // LANE)),
            pl.BlockSpec((1, TQ, LANE), lambda b, i: (b, i, F32_OFF["iw"] // LANE)),
            pl.BlockSpec((1, TQ, A_WIDTH), lambda b, i: (b, i, F32_OFF["ag"] // A_WIDTH)),
            pl.BlockSpec((A_HEADS, 2, TQ, KB), lambda b, i: (0, 0, 0, 0), pipeline_mode=once),
        ],
        out_specs=pl.BlockSpec((1, TQ, A_WIDTH), lambda b, i: (b, i, 0)),
        out_shape=jax.ShapeDtypeStruct((BATCH, SEQ, A_WIDTH), jnp.bfloat16),
        scratch_shapes=[
            pltpu.VMEM((SEQ, LANE), jnp.bfloat16),
            pltpu.VMEM((SEQ, LANE), jnp.bfloat16),
            pltpu.VMEM((IDX_HEADS, TQ, HALF), jnp.float32),
            pltpu.VMEM((IDX_HEADS // 2 * TQ, LANE), jnp.bfloat16),
            pltpu.VMEM((NKB, TQ, KB), jnp.int32),
            pltpu.VMEM((NKB, KB, TQ), jnp.int16),
            pltpu.VMEM((NKB, KB, TQ), jnp.int16),
            pltpu.VMEM((TQ, KB), jnp.int32),
            pltpu.VMEM((TQ, KB), jnp.int32),
            pltpu.VMEM((PACK16, TQ), jnp.int16),
            pltpu.VMEM((NKB, TQ, KB), jnp.float32),
            pltpu.VMEM((A_HEADS, TQ, HALF), jnp.float32),
            pltpu.VMEM((A_HEADS, TQ, A_HEAD_DIM), jnp.float32),
            pltpu.VMEM((A_HEADS, TQ, A_HEAD_DIM), jnp.float32),
        ],
        compiler_params=pltpu.CompilerParams(
            dimension_semantics=("arbitrary", "arbitrary"), vmem_limit_bytes=VMEM_LIMIT),
        name="dsa_attention",
    )(proj_b, proj_b, proj_b, proj_b, proj_f, proj_f, proj_f, band)


GLA_CT = 512
GLA_C = 64
GLA_SC = 256
GLA_NCS = GLA_SC // GLA_C


def _gla_kernel(bq_ref, bk_ref, bv_ref, bg_ref, ba_ref, wup_ref, balpha_ref, g_ref, o_ref, st_scr):
    @pl.when(pl.program_id(1) == 0)
    def _():
        st_scr[...] = jnp.zeros_like(st_scr)

    bf16 = jnp.bfloat16
    rr = lax.broadcasted_iota(jnp.int32, (GLA_SC, GLA_SC), 0)
    cc = lax.broadcasted_iota(jnp.int32, (GLA_SC, GLA_SC), 1)
    tri = (rr >= cc) & (rr // GLA_C == cc // GLA_C)
    tri_bf = tri.astype(bf16)
    er = lax.broadcasted_iota(jnp.int32, (GLA_SC, GLA_NCS * B_DK), 0)
    ec = lax.broadcasted_iota(jnp.int32, (GLA_SC, GLA_NCS * B_DK), 1)
    own_block = (er // GLA_C) == (ec // B_DK)
    wup = wup_ref[...].astype(bf16)
    balpha = balpha_ref[...]
    gain = g_ref[...]
    heads = range(B_HEADS)

    def expand(a):
        return jnp.where(own_block, jnp.concatenate([a] * GLA_NCS, axis=1), jnp.zeros((), a.dtype))

    def body(sc, carry):
        rows = pl.ds(pl.multiple_of(sc * GLA_SC, GLA_SC), GLA_SC)
        ba = ba_ref[0, rows, :][:, :GATE_RANK].astype(bf16)
        pre = jnp.dot(ba, wup, preferred_element_type=jnp.float32) + balpha
        log_a = (jnp.minimum(pre, 0.0) - jnp.log1p(jnp.exp(-jnp.abs(pre)))) * (1.0 / GATE_TEMP)
        la_hi = log_a.astype(bf16)
        la_lo = (log_a - la_hi.astype(jnp.float32)).astype(bf16)
        bcum = (jnp.dot(tri_bf, la_hi, preferred_element_type=jnp.float32)
                + jnp.dot(tri_bf, la_lo, preferred_element_type=jnp.float32))
        ks = [slice(h * B_DK, (h + 1) * B_DK) for h in heads]
        vs = [slice(h * B_DV, (h + 1) * B_DV) for h in heads]
        qe, ke, kd, dec, v = [], [], [], [], []
        for h in heads:
            b = bcum[:, ks[h]]
            b_end = b.reshape(GLA_NCS, GLA_C, B_DK)[:, GLA_C - 1:GLA_C, :]
            b_end_rows = jnp.broadcast_to(b_end, (GLA_NCS, GLA_C, B_DK)).reshape(GLA_SC, B_DK)
            q = bq_ref[0, rows, ks[h]] * (B_DK ** -0.5)
            k = bk_ref[0, rows, ks[h]]
            qe.append((q * jnp.exp(b)).astype(bf16))
            ke.append((k * jnp.exp(-b)).astype(bf16))
            kd.append((k * jnp.exp(b_end_rows - b)).astype(bf16))
            dec.append(jnp.exp(b_end))
            v.append(bv_ref[0, rows, vs[h]].astype(bf16))
        attn = [jnp.where(tri, _dot_nt(qe[h], ke[h]), 0.0).astype(bf16) for h in heads]
        o_intra = [jnp.dot(attn[h], v[h], preferred_element_type=jnp.float32) for h in heads]
        upd = [_dot_tn(v[h], expand(kd[h])) for h in heads]
        o_inter = []
        for h in heads:
            st = st_scr[h]
            states = []
            for c in range(GLA_NCS):
                states.append(st.astype(bf16))
                st = st * dec[h][c] + upd[h][:, c * B_DK:(c + 1) * B_DK]
            st_scr[h] = st
            o_inter.append(_dot_nt(expand(qe[h]), jnp.concatenate(states, axis=1)))
        for h in heads:
            o = o_intra[h] + o_inter[h]
            on = o * lax.rsqrt(jnp.mean(o * o, axis=-1, keepdims=True) + EPS) * gain
            g = bg_ref[0, rows, vs[h]]
            o_ref[0, rows, vs[h]] = (on * (g * jax.nn.sigmoid(g))).astype(o_ref.dtype)
        return carry

    lax.fori_loop(0, GLA_CT // GLA_SC, body, 0)


def _gla(proj_f, w_alpha_up, b_alpha, gla_g):
    return pl.pallas_call(
        _gla_kernel,
        grid=(BATCH, SEQ // GLA_CT),
        in_specs=[
            pl.BlockSpec((1, GLA_CT, B_KEY_WIDTH), lambda b, t: (b, t, F32_OFF["bq"] // B_KEY_WIDTH)),
            pl.BlockSpec((1, GLA_CT, B_KEY_WIDTH), lambda b, t: (b, t, F32_OFF["bk"] // B_KEY_WIDTH)),
            pl.BlockSpec((1, GLA_CT, B_WIDTH), lambda b, t: (b, t, F32_OFF["bv"] // B_WIDTH)),
            pl.BlockSpec((1, GLA_CT, B_WIDTH), lambda b, t: (b, t, F32_OFF["bg"] // B_WIDTH)),
            pl.BlockSpec((1, GLA_CT, LANE), lambda b, t: (b, t, F32_OFF["ba"] // LANE)),
            pl.BlockSpec((GATE_RANK, B_KEY_WIDTH), lambda b, t: (0, 0)),
            pl.BlockSpec((1, B_KEY_WIDTH), lambda b, t: (0, 0)),
            pl.BlockSpec((1, B_DV), lambda b, t: (0, 0)),
        ],
        out_specs=pl.BlockSpec((1, GLA_CT, B_WIDTH), lambda b, t: (b, t, 0)),
        out_shape=jax.ShapeDtypeStruct((BATCH, SEQ, B_WIDTH), jnp.bfloat16),
        scratch_shapes=[pltpu.VMEM((B_HEADS, B_DV, B_DK), jnp.float32)],
        compiler_params=pltpu.CompilerParams(
            dimension_semantics=("arbitrary", "arbitrary"), vmem_limit_bytes=VMEM_LIMIT),
        name="gla",
    )(proj_f, proj_f, proj_f, proj_f, proj_f, w_alpha_up, b_alpha.reshape(1, B_KEY_WIDTH),
      gla_g.reshape(1, B_DV))


OUT_TM = 512


def _out_kernel(a_ref, b_ref, wa_ref, wb_ref, x_ref, gate_ref, fg_ref, o_ref, *, final_norm):
    y = (jnp.dot(a_ref[...], wa_ref[...], preferred_element_type=jnp.float32)
         + jnp.dot(b_ref[...], wb_ref[...], preferred_element_type=jnp.float32))
    xn = x_ref[...] + gate_ref[0] * y
    if final_norm:
        r = xn * lax.rsqrt(jnp.mean(xn * xn, axis=-1, keepdims=True) + EPS)
        xn = r * fg_ref[...]
    o_ref[...] = xn


def _out_proj(a_out, b_out, w_out_bf, x2d, gate, final_g, layer, final_norm):
    m = x2d.shape[0]
    tiles_per_batch = SEQ // OUT_TM
    return pl.pallas_call(
        functools.partial(_out_kernel, final_norm=final_norm),
        grid=(m // OUT_TM,),
        in_specs=[
            pl.BlockSpec((OUT_TM, A_WIDTH), lambda i: (i, 0)),
            pl.BlockSpec((OUT_TM, B_WIDTH), lambda i: (i, 0)),
            pl.BlockSpec((None, A_WIDTH, D_MODEL), lambda i: (layer, 0, 0)),
            pl.BlockSpec((None, B_WIDTH, D_MODEL), lambda i: (layer, 1, 0)),
            pl.BlockSpec((OUT_TM, D_MODEL), lambda i: (i, 0)),
            pl.BlockSpec((1, 1, D_MODEL), lambda i: (i // tiles_per_batch, 0, 0)),
            pl.BlockSpec((1, D_MODEL), lambda i: (0, 0)),
        ],
        out_specs=pl.BlockSpec((OUT_TM, D_MODEL), lambda i: (i, 0)),
        out_shape=jax.ShapeDtypeStruct((m, D_MODEL), jnp.float32),
        compiler_params=pltpu.CompilerParams(
            dimension_semantics=("arbitrary",), vmem_limit_bytes=VMEM_LIMIT),
        name="out_proj",
    )(a_out, b_out, w_out_bf, w_out_bf, x2d, gate, final_g.reshape(1, D_MODEL))


def kernel(x, c, w_ada, b_ada, norm_g, w_in, w_alpha_up, b_alpha, gla_g, w_out, rel_bias, final_g):
    mod = _adaln_mod(c, w_ada, b_ada)
    band = _bias_band(rel_bias)
    w_in_p = _pack_w_in(jnp.swapaxes(w_in, 1, 2))
    w_out_bf = w_out.astype(jnp.bfloat16)
    x2d = x.reshape(BATCH * SEQ, D_MODEL)
    for l in range(DEPTH):
        shift = mod[l, :, 0:D_MODEL].reshape(BATCH, 1, D_MODEL)
        scale = mod[l, :, D_MODEL:2 * D_MODEL].reshape(BATCH, 1, D_MODEL)
        gate = mod[l, :, 2 * D_MODEL:].reshape(BATCH, 1, D_MODEL)
        proj_b, proj_f = _norm_proj(x2d, shift, scale, norm_g[l], w_in_p, l)
        proj_b = proj_b.reshape(BATCH, SEQ, NB_COLS)
        proj_f = proj_f.reshape(BATCH, SEQ, NF_COLS)
        a_out = _dsa_attention(proj_b, proj_f, band)
        b_out = _gla(proj_f, w_alpha_up[l], b_alpha[l], gla_g[l])
        x2d = _out_proj(a_out.reshape(BATCH * SEQ, A_WIDTH), b_out.reshape(BATCH * SEQ, B_WIDTH),
                        w_out_bf, x2d, gate, final_g, l, final_norm=(l == DEPTH - 1))
    return x2d.reshape(BATCH, SEQ, D_MODEL)
```
